```python
import math
import jax
import jax.numpy as jnp
from jax import lax
import numpy as np

D_MODEL = 1024
BATCH = 32
SEQ = 2048
DEPTH = 2

GRID_W = 64
CTX_LEN = 256
F32 = jnp.float32
EPS = 1e-6
HEAD_DIM = 64
ROPE_FREQS = HEAD_DIM // 4
ROPE_BASE = 10000.0
QUERY_BLOCK = 128
S5_WIDTH = D_MODEL // 2
S5_GROUP_CH = 16
S5_GROUPS = S5_WIDTH // S5_GROUP_CH
S5_STATE = 64
S5_DT_MIN = 1e-3
S5_DT_MAX = 1e-1
GQA_HEADS = (D_MODEL // 2) // HEAD_DIM
GQA_KV_HEADS = GQA_HEADS // 2
HYENA_WIDTH = D_MODEL // 2
HYENA_ORDER = 2
HYENA_BANDS = 16
HYENA_POS_DIM = 1 + 2 * HYENA_BANDS
HYENA_HIDDEN = 64
SHORT_CONV = 3
HYENA_DECAY_MIN = 3.07
HYENA_DECAY_MAX = 15.35
NA_HEADS = (D_MODEL // 2) // HEAD_DIM
NA_WIN_ROWS = 8
NA_WIN_COLS = 16
FFN_DIM = 7 * D_MODEL // 2
MOE_EXPERTS = 8
MOE_TOP_K = 2
MOE_FFN_DIM = 7 * D_MODEL // 2
MOE_BLOCK = 256
EVEN_IN = S5_WIDTH + (GQA_HEADS + 2 * GQA_KV_HEADS) * HEAD_DIM
EVEN_MIX = S5_WIDTH + GQA_HEADS * HEAD_DIM
ODD_IN = 3 * HYENA_WIDTH + 3 * NA_HEADS * HEAD_DIM
ODD_MIX = HYENA_WIDTH + NA_HEADS * HEAD_DIM

kernel_name = 'hybrid_s5_gqa_hyena_natten_moe_flow_block'


def rmsnorm(x, g):
    xf = x.astype(F32)
    y = xf * lax.rsqrt(jnp.mean(xf * xf, axis=-1, keepdims=True) + EPS)
    return (y * g.astype(F32)).astype(x.dtype)


def adaln(cvec, w, b):
    m = jax.nn.silu(cvec) @ w + b
    return jnp.split(m[:, None, :], 6, axis=-1)


def modulate(h, shift, scale):
    return h * (1.0 + scale) + shift


def swiglu(h, w1, w3, w2):
    return (jax.nn.silu(h @ w1) * (h @ w3)) @ w2


def s5_discretise(lam_re, lam_im, log_dt, b_re, b_im):
    dt = jnp.exp(log_dt.astype(F32))[:, None]
    lr = lam_re.astype(F32)
    li = lam_im.astype(F32)
    mag = jnp.exp(lr * dt)
    ar = mag * jnp.cos(li * dt)
    ai = mag * jnp.sin(li * dt)
    nr = ar - 1.0
    den = lr * lr + li * li
    kr = (nr * lr + ai * li) / den
    ki = (ai * lr - nr * li) / den
    br = b_re.astype(F32)
    bi = b_im.astype(F32)
    bbr = kr[..., None] * br - ki[..., None] * bi
    bbi = kr[..., None] * bi + ki[..., None] * br
    return ar, ai, bbr, bbi


def s5_drive(u, bbr, bbi):
    return (jnp.einsum('blgp,gnp->blgn', u, bbr), jnp.einsum('blgp,gnp->blgn', u, bbi))


def complex_affine_combine(e1, e2):
    a1r, a1i, b1r, b1i = e1
    a2r, a2i, b2r, b2i = e2
    return (a2r * a1r - a2i * a1i,
            a2r * a1i + a2i * a1r,
            a2r * b1r - a2i * b1i + b2r,
            a2r * b1i + a2i * b1r + b2i)


def s5_scan(ar, ai, br, bi, reverse):
    L = br.shape[1]
    a_r = jnp.broadcast_to(ar, (1, L) + ar.shape)
    a_i = jnp.broadcast_to(ai, (1, L) + ai.shape)
    _, _, hr, hi = lax.associative_scan(complex_affine_combine, (a_r, a_i, br, bi), reverse=reverse, axis=1)
    return hr, hi


def s5_readout(hr, hi, cr, ci):
    return (jnp.einsum('blgn,gpn->blgp', hr, cr.astype(F32))
            - jnp.einsum('blgn,gpn->blgp', hi, ci.astype(F32)))


def s5_glu(y, w_glu):
    g = jax.nn.gelu(y)
    return g * jax.nn.sigmoid(g @ w_glu.astype(F32))


def s5_mixer(u_x, u_c, lam_re, lam_im, log_dt, b_re, b_im, c_re, c_im, d_skip, w_glu, ctx_out):
    bsz, L, width = u_x.shape
    lc = u_c.shape[1]
    ux = u_x.astype(F32).reshape(bsz, L, S5_GROUPS, S5_GROUP_CH)
    uc = u_c.astype(F32).reshape(bsz, lc, S5_GROUPS, S5_GROUP_CH)
    d = d_skip.astype(F32).reshape(S5_GROUPS, S5_GROUP_CH)
    y_x = ux * d
    y_c = uc * d if ctx_out else None
    for direction in range(2):
        rev = direction == 1
        ar, ai, bbr, bbi = s5_discretise(lam_re[direction], lam_im[direction], log_dt[direction],
                                         b_re[direction], b_im[direction])
        hcr, hci = s5_scan(ar, ai, *s5_drive(uc, bbr, bbi), rev)
        end = 0 if rev else lc - 1
        h0r, h0i = hcr[:, end], hci[:, end]
        bxr, bxi = s5_drive(ux, bbr, bbi)
        first = L - 1 if rev else 0
        bxr = bxr.at[:, first].add(ar * h0r - ai * h0i)
        bxi = bxi.at[:, first].add(ar * h0i + ai * h0r)
        hxr, hxi = s5_scan(ar, ai, bxr, bxi, rev)
        y_x = y_x + s5_readout(hxr, hxi, c_re[direction], c_im[direction])
        if ctx_out:
            y_c = y_c + s5_readout(hcr, hci, c_re[direction], c_im[direction])
    out_x = s5_glu(y_x.reshape(bsz, L, width), w_glu)
    out_c = s5_glu(y_c.reshape(bsz, lc, width), w_glu) if ctx_out else None
    return out_x, out_c


def axial_rope_angles(L):
    t = jnp.arange(L)
    pos = jnp.stack([t // GRID_W, t % GRID_W], axis=-1).astype(F32)
    inv = ROPE_BASE ** (-jnp.arange(ROPE_FREQS, dtype=F32) / ROPE_FREQS)
    return pos[:, :, None] * inv


def apply_axial_rope(x, ang):
    b, L, h, dh = x.shape
    xs = x.astype(F32).reshape(b, L, h, 2, 2, ROPE_FREQS)
    x1, x2 = xs[..., 0, :], xs[..., 1, :]
    cos = jnp.cos(ang)[None, :, None]
    sin = jnp.sin(ang)[None, :, None]
    out = jnp.stack([x1 * cos - x2 * sin, x2 * cos + x1 * sin], axis=-2)
    return out.reshape(b, L, h, dh).astype(x.dtype)


def attend(q, k, v):
    s = jnp.einsum('bqhgd,bkhd->bhgqk', q, k).astype(F32) * (HEAD_DIM ** -0.5)
    p = jax.nn.softmax(s, axis=-1).astype(v.dtype)
    return jnp.einsum('bhgqk,bkhd->bqhgd', p, v)


def gqa_mixer(q_x, k_x, v_x, q_c, k_c, v_c, q_g, k_g, ang, ctx_out):
    b, L = q_x.shape[:2]
    lc = q_c.shape[1]
    grp = GQA_HEADS // GQA_KV_HEADS
    q_x = apply_axial_rope(rmsnorm(q_x.reshape(b, L, GQA_HEADS, HEAD_DIM), q_g), ang)
    k_x = apply_axial_rope(rmsnorm(k_x.reshape(b, L, GQA_KV_HEADS, HEAD_DIM), k_g), ang)
    v_x = v_x.reshape(b, L, GQA_KV_HEADS, HEAD_DIM)
    k_c = rmsnorm(k_c.reshape(b, lc, GQA_KV_HEADS, HEAD_DIM), k_g)
    v_c = v_c.reshape(b, lc, GQA_KV_HEADS, HEAD_DIM)
    k_all = jnp.concatenate([k_x, k_c], axis=1)
    v_all = jnp.concatenate([v_x, v_c], axis=1)
    nb = L // QUERY_BLOCK
    qb = q_x.reshape(b, nb, QUERY_BLOCK, GQA_KV_HEADS, grp, HEAD_DIM).transpose(1, 0, 2, 3, 4, 5)
    o = lax.map(lambda qblk: attend(qblk, k_all, v_all), qb)
    o_x = o.transpose(1, 0, 2, 3, 4, 5).reshape(b, L, GQA_HEADS * HEAD_DIM)
    o_c = None
    if ctx_out:
        qc = rmsnorm(q_c.reshape(b, lc, GQA_HEADS, HEAD_DIM), q_g).reshape(b, lc, GQA_KV_HEADS, grp, HEAD_DIM)
        o_c = attend(qc, k_c, v_c).reshape(b, lc, GQA_HEADS * HEAD_DIM)
    return o_x, o_c


def even_mixer(hx, hc, w_in, w_out, lam_re, lam_im, log_dt, b_re, b_im, c_re, c_im, d_skip, w_glu,
               q_g, k_g, ang, ctx_out):
    splits = [S5_WIDTH, S5_WIDTH + GQA_HEADS * HEAD_DIM, S5_WIDTH + (GQA_HEADS + GQA_KV_HEADS) * HEAD_DIM]
    u_x, q_x, k_x, v_x = jnp.split(hx @ w_in, splits, axis=-1)
    u_c, q_c, k_c, v_c = jnp.split(hc @ w_in, splits, axis=-1)
    a_x, a_c = s5_mixer(u_x, u_c, lam_re, lam_im, log_dt, b_re, b_im, c_re, c_im, d_skip, w_glu, ctx_out)
    g_x, g_c = gqa_mixer(q_x, k_x, v_x, q_c, k_c, v_c, q_g, k_g, ang, ctx_out)
    ox = jnp.concatenate([a_x.astype(g_x.dtype), g_x], axis=-1) @ w_out
    oc = jnp.concatenate([a_c.astype(g_c.dtype), g_c], axis=-1) @ w_out if ctx_out else None
    return ox, oc


def short_conv(x, w, b):
    y = lax.conv_general_dilated(x, w[:, None, :].astype(x.dtype), window_strides=(1,),
                                 padding=((SHORT_CONV // 2, SHORT_CONV // 2),),
                                 dimension_numbers=('NWC', 'WIO', 'NWC'),
                                 feature_group_count=x.shape[-1])
    return y + b


def hyena_filter_spectrum(L, f_w1, f_b1, f_w2, f_b2, f_w3, f_freq, f_decay):
    k = jnp.arange(L, dtype=F32)
    t = k / max(L - 1, 1)
    bands = jnp.linspace(1e-4, HYENA_BANDS - 1, HYENA_BANDS, dtype=F32)
    ang = (2.0 * math.pi / L) * k[:, None] * bands[None, :]
    z = jnp.concatenate([t[:, None], jnp.cos(ang), -jnp.sin(ang)], axis=-1)
    freq = f_freq.astype(F32)
    h = jnp.sin(freq * (z @ f_w1.astype(F32) + f_b1.astype(F32)))
    h = jnp.sin(freq * (h @ f_w2.astype(F32) + f_b2.astype(F32)))
    h = h @ f_w3.astype(F32)
    h = h * jnp.exp(-t[:, None] * jnp.abs(f_decay.astype(F32)))
    h = h / (jnp.sum(jnp.abs(h), axis=0, keepdims=True) + EPS)
    h = h.reshape(L, 2, HYENA_ORDER, HYENA_WIDTH)
    spec_fwd = jnp.fft.rfft(h[:, 0], n=2 * L, axis=0)
    spec_bwd = jnp.fft.rfft(h[:, 1], n=2 * L, axis=0)
    return spec_fwd + jnp.conj(spec_bwd)


def hyena_mixer(proj, conv_w, conv_b, spec, hy_d):
    L = proj.shape[1]
    u = short_conv(proj, conv_w, conv_b).astype(F32)
    v, g1, g2 = jnp.split(u, 3, axis=-1)
    z = v
    for o, gate in enumerate((g1, g2)):
        zf = jnp.fft.rfft(z, n=2 * L, axis=1)
        y = jnp.fft.irfft(zf * spec[None, :, o], n=2 * L, axis=1)[:, :L]
        z = gate * (y + hy_d[o].astype(F32) * z)
    return z


def na_mixer(q_x, k_x, v_x, k_c, v_c, rpb):
    b, L = q_x.shape[:2]
    rows = L // GRID_W
    wr = min(NA_WIN_ROWS, rows)
    shp = (b, rows, GRID_W, NA_HEADS, HEAD_DIM)
    q_rows = q_x.reshape(shp).transpose(1, 0, 2, 3, 4)
    k_grid = k_x.reshape(shp)
    v_grid = v_x.reshape(shp)
    col = jnp.arange(GRID_W)
    col_start = jnp.clip(col - NA_WIN_COLS // 2, 0, GRID_W - NA_WIN_COLS)
    col_mask = (col[None, :] >= col_start[:, None]) & (col[None, :] < col_start[:, None] + NA_WIN_COLS)
    dc_idx = jnp.clip(col[None, :] - col[:, None] + NA_WIN_COLS - 1, 0, 2 * NA_WIN_COLS - 2)
    scale = HEAD_DIM ** -0.5

    def row_step(args):
        r, q = args
        r0 = jnp.clip(r - wr // 2, 0, rows - wr)
        k_band = lax.dynamic_slice_in_dim(k_grid, r0, wr, axis=1)
        v_band = lax.dynamic_slice_in_dim(v_grid, r0, wr, axis=1)
        dr_idx = r0 + jnp.arange(wr) - r + NA_WIN_ROWS - 1
        bias = rpb[:, dr_idx[None, :, None], dc_idx[:, None, :]].astype(F32)
        s_loc = jnp.einsum('bqhd,bwkhd->bhqwk', q, k_band).astype(F32) * scale + bias
        s_loc = jnp.where(col_mask[:, None, :], s_loc, -jnp.inf)
        s_ctx = jnp.einsum('bqhd,bchd->bhqc', q, k_c).astype(F32) * scale
        s = jnp.concatenate([s_loc.reshape(b, NA_HEADS, GRID_W, wr * GRID_W), s_ctx], axis=-1)
        p = jax.nn.softmax(s, axis=-1).astype(v_band.dtype)
        p_loc = p[..., :wr * GRID_W].reshape(b, NA_HEADS, GRID_W, wr, GRID_W)
        p_ctx = p[..., wr * GRID_W:]
        return (jnp.einsum('bhqwk,bwkhd->bqhd', p_loc, v_band)
                + jnp.einsum('bhqc,bchd->bqhd', p_ctx, v_c))

    o = lax.map(row_step, (jnp.arange(rows), q_rows))
    return o.transpose(1, 0, 2, 3, 4).reshape(b, L, NA_HEADS * HEAD_DIM)


def odd_mixer(hx, hc, w_in, w_out, conv_w, conv_b, f_w1, f_b1, f_w2, f_b2, f_w3, f_freq, f_decay,
              hy_d, rpb, ctx_out):
    b, L, _ = hx.shape
    lc = hc.shape[1]
    hy_w = 3 * HYENA_WIDTH
    na_w = NA_HEADS * HEAD_DIM

    def heads(t):
        return t.reshape(t.shape[0], t.shape[1], NA_HEADS, HEAD_DIM)

    px = hx @ w_in
    hy_x, q_x, k_x, v_x = jnp.split(px, [hy_w, hy_w + na_w, hy_w + 2 * na_w], axis=-1)
    if ctx_out:
        hy_c, q_c, k_c, v_c = jnp.split(hc @ w_in, [hy_w, hy_w + na_w, hy_w + 2 * na_w], axis=-1)
    else:
        k_c, v_c = jnp.split(hc @ w_in[:, hy_w + na_w:], 2, axis=-1)
    filt = (f_w1, f_b1, f_w2, f_b2, f_w3, f_freq, f_decay)
    o_hy = hyena_mixer(hy_x, conv_w, conv_b, hyena_filter_spectrum(L, *filt), hy_d)
    o_na = na_mixer(heads(q_x), heads(k_x), heads(v_x), heads(k_c), heads(v_c), rpb)
    ox = jnp.concatenate([o_hy.astype(o_na.dtype), o_na], axis=-1) @ w_out
    oc = None
    if ctx_out:
        oc_hy = hyena_mixer(hy_c, conv_w, conv_b, hyena_filter_spectrum(lc, *filt), hy_d)
        oc_na = attend(heads(q_c)[:, :, :, None, :], heads(k_c), heads(v_c)).reshape(b, lc, na_w)
        oc = jnp.concatenate([oc_hy.astype(oc_na.dtype), oc_na], axis=-1) @ w_out
    return ox, oc


def moe_swiglu(h, router, w1, w3, w2):
    b, L, d = h.shape
    T = b * L
    n_assign = T * MOE_TOP_K
    xf = h.reshape(T, d)
    logits = (xf @ router).astype(F32)
    top_val, top_idx = lax.top_k(logits, MOE_TOP_K)
    top_w = jax.nn.softmax(top_val, axis=-1)
    e_flat = top_idx.reshape(-1)
    tok_flat = jnp.arange(n_assign) // MOE_TOP_K
    w_flat = top_w.reshape(-1)
    order = jnp.argsort(e_flat)
    e_s, tok_s, w_s = e_flat[order], tok_flat[order], w_flat[order]
    counts = jnp.bincount(e_flat, length=MOE_EXPERTS)
    start = jnp.cumsum(counts) - counts
    padded = (counts + MOE_BLOCK - 1) // MOE_BLOCK * MOE_BLOCK
    pend = jnp.cumsum(padded)
    pstart = pend - padded
    slot = pstart[e_s] + jnp.arange(n_assign) - start[e_s]
    n_slots = n_assign + MOE_EXPERTS * MOE_BLOCK
    slot_tok = jnp.full((n_slots,), T, jnp.int32).at[slot].set(tok_s.astype(jnp.int32))
    block_e = jnp.clip(jnp.searchsorted(pend, jnp.arange(n_slots // MOE_BLOCK) * MOE_BLOCK, side='right'),
                       0, MOE_EXPERTS - 1)
    x_pad = jnp.concatenate([xf, jnp.zeros((1, d), xf.dtype)], axis=0)

    def run_block(args):
        toks, e = args
        xb = x_pad[toks]
        return (jax.nn.silu(xb @ w1[e]) * (xb @ w3[e])) @ w2[e]

    y_slots = lax.map(run_block, (slot_tok.reshape(-1, MOE_BLOCK), block_e)).reshape(n_slots, d)
    y = jax.ops.segment_sum(y_slots[slot] * w_s[:, None].astype(y_slots.dtype), tok_s, num_segments=T)
    return y.reshape(b, L, d)


def setup_inputs(seed: int = 0) -> dict:
    key = jax.random.key(seed)
    keys = iter(jax.random.split(key, 64))
    ne, no = (DEPTH + 1) // 2, DEPTH // 2
    G, N, P = S5_GROUPS, S5_STATE, S5_GROUP_CH

    def normal(shape, scale):
        return scale * jax.random.normal(next(keys), shape, F32)

    def gain(shape):
        return 1.0 + normal(shape, 0.05)

    D = D_MODEL
    return {
        'x': normal((BATCH, SEQ, D), 1.0),
        'c': normal((BATCH, D), 1.0),
        'ctx': normal((BATCH, CTX_LEN, D), 1.0),
        'c_ctx': normal((D,), 1.0),
        'mod_w': normal((DEPTH, D, 6 * D), D ** -0.5),
        'mod_b': normal((DEPTH, 6 * D), 0.02),
        'norm1_g': gain((DEPTH, D)),
        'norm2_g': gain((DEPTH, D)),
        'ev_w_in': normal((ne, D, EVEN_IN), D ** -0.5),
        'ev_w_out': normal((ne, EVEN_MIX, D), EVEN_MIX ** -0.5),
        's5_lam_re': -0.5 + normal((ne, 2, G, N), 0.01),
        's5_lam_im': math.pi * jnp.arange(N, dtype=F32) + normal((ne, 2, G, N), 0.01),
        's5_log_dt': jax.random.uniform(next(keys), (ne, 2, G), F32, math.log(S5_DT_MIN), math.log(S5_DT_MAX)),
        's5_b_re': normal((ne, 2, G, N, P), (2 * P) ** -0.5),
        's5_b_im': normal((ne, 2, G, N, P), (2 * P) ** -0.5),
        's5_c_re': normal((ne, 2, G, P, N), (2 * N) ** -0.5),
        's5_c_im': normal((ne, 2, G, P, N), (2 * N) ** -0.5),
        's5_d': normal((ne, S5_WIDTH), 1.0),
        's5_w_glu': normal((ne, S5_WIDTH, S5_WIDTH), S5_WIDTH ** -0.5),
        'gqa_q_g': gain((ne, HEAD_DIM)),
        'gqa_k_g': gain((ne, HEAD_DIM)),
        'ffn_w1': normal((ne, D, FFN_DIM), D ** -0.5),
        'ffn_w3': normal((ne, D, FFN_DIM), D ** -0.5),
        'ffn_w2': normal((ne, FFN_DIM, D), FFN_DIM ** -0.5),
        'od_w_in': normal((no, D, ODD_IN), D ** -0.5),
        'od_w_out': normal((no, ODD_MIX, D), ODD_MIX ** -0.5),
        'hy_conv_w': normal((no, SHORT_CONV, 3 * HYENA_WIDTH), SHORT_CONV ** -0.5),
        'hy_conv_b': normal((no, 3 * HYENA_WIDTH), 0.02),
        'hy_w1': normal((no, HYENA_POS_DIM, HYENA_HIDDEN), HYENA_POS_DIM ** -0.5),
        'hy_b1': normal((no, HYENA_HIDDEN), 0.1),
        'hy_w2': normal((no, HYENA_HIDDEN, HYENA_HIDDEN), HYENA_HIDDEN ** -0.5),
        'hy_b2': normal((no, HYENA_HIDDEN), 0.1),
        'hy_w3': normal((no, HYENA_HIDDEN, 2 * HYENA_ORDER * HYENA_WIDTH), HYENA_HIDDEN ** -0.5),
        'hy_freq': 1.0 + normal((no, HYENA_HIDDEN), 0.05),
        'hy_decay': jax.random.uniform(next(keys), (no, 2 * HYENA_ORDER * HYENA_WIDTH), F32,
                                       HYENA_DECAY_MIN, HYENA_DECAY_MAX),
        'hy_d': normal((no, HYENA_ORDER, HYENA_WIDTH), 1.0),
        'na_rpb': normal((no, NA_HEADS, 2 * NA_WIN_ROWS - 1, 2 * NA_WIN_COLS - 1), 0.02),
        'moe_router': normal((no, D, MOE_EXPERTS), D ** -0.5),
        'moe_w1': normal((no, MOE_EXPERTS, D, MOE_FFN_DIM), D ** -0.5),
        'moe_w3': normal((no, MOE_EXPERTS, D, MOE_FFN_DIM), D ** -0.5),
        'moe_w2': normal((no, MOE_EXPERTS, MOE_FFN_DIM, D), MOE_FFN_DIM ** -0.5),
        'final_g': gain((D,)),
    }


def reference(x, c, ctx, c_ctx, mod_w, mod_b, norm1_g, norm2_g,
              ev_w_in, ev_w_out, s5_lam_re, s5_lam_im, s5_log_dt, s5_b_re, s5_b_im, s5_c_re, s5_c_im,
              s5_d, s5_w_glu, gqa_q_g, gqa_k_g, ffn_w1, ffn_w3, ffn_w2,
              od_w_in, od_w_out, hy_conv_w, hy_conv_b, hy_w1, hy_b1, hy_w2, hy_b2, hy_w3, hy_freq,
              hy_decay, hy_d, na_rpb, moe_router, moe_w1, moe_w3, moe_w2, final_g):
    L = x.shape[1]
    ang = axial_rope_angles(L)
    cx = ctx

    def channel_mix(h, layer):
        i = layer // 2
        if layer % 2 == 0:
            return swiglu(h, ffn_w1[i], ffn_w3[i], ffn_w2[i])
        return moe_swiglu(h, moe_router[i], moe_w1[i], moe_w3[i], moe_w2[i])

    for layer in range(DEPTH):
        last = layer == DEPTH - 1
        i = layer // 2
        sh1, sc1, g1, sh2, sc2, g2 = adaln(c, mod_w[layer], mod_b[layer])
        csh1, csc1, cg1, csh2, csc2, cg2 = adaln(c_ctx[None], mod_w[layer], mod_b[layer])
        hx = modulate(rmsnorm(x, norm1_g[layer]), sh1, sc1)
        hc = modulate(rmsnorm(cx, norm1_g[layer]), csh1, csc1)
        if layer % 2 == 0:
            ox, oc = even_mixer(hx, hc, ev_w_in[i], ev_w_out[i], s5_lam_re[i], s5_lam_im[i], s5_log_dt[i],
                                s5_b_re[i], s5_b_im[i], s5_c_re[i], s5_c_im[i], s5_d[i], s5_w_glu[i],
                                gqa_q_g[i], gqa_k_g[i], ang, not last)
        else:
            ox, oc = odd_mixer(hx, hc, od_w_in[i], od_w_out[i], hy_conv_w[i], hy_conv_b[i], hy_w1[i], hy_b1[i],
                               hy_w2[i], hy_b2[i], hy_w3[i], hy_freq[i], hy_decay[i], hy_d[i], na_rpb[i],
                               not last)
        x = x + g1 * ox
        x = x + g2 * channel_mix(modulate(rmsnorm(x, norm2_g[layer]), sh2, sc2), layer)
        if not last:
            cx = cx + cg1 * oc
            cx = cx + cg2 * channel_mix(modulate(rmsnorm(cx, norm2_g[layer]), csh2, csc2), layer)
    return rmsnorm(x, final_g)
```

```python
import functools
import math

import jax
import jax.numpy as jnp
import numpy as np
from jax import lax
from jax.experimental import pallas as pl
from jax.experimental.pallas import tpu as pltpu

F32 = jnp.float32
BF16 = jnp.bfloat16
EPS = 1e-6
HEAD_DIM = 64
GRID_W = 64
ROPE_FREQS = HEAD_DIM // 4
ROPE_BASE = 10000.0
ROW_TILE = 256
S5_GROUP_CH = 16
S5_STATE = 64
S5_CHUNK = 16
VMEM_LIMIT = 56 * 1024 * 1024
HIGHEST = lax.Precision.HIGHEST


def _cparams(*sem):
    return pltpu.CompilerParams(dimension_semantics=sem, vmem_limit_bytes=VMEM_LIMIT)


def _bdot(a, b):
    return jnp.dot(a.astype(BF16), b.astype(BF16), preferred_element_type=F32)


def _silu(a):
    return a * jax.nn.sigmoid(a)


def _norm_mod(x, g, shift, scale):
    y = x * lax.rsqrt(jnp.mean(x * x, axis=-1, keepdims=True) + EPS)
    return (y * g) * (1.0 + scale) + shift


def _pick_mod(modx_ref, modc_ref, is_ctx):
    return jnp.where(is_ctx, modc_ref[...], modx_ref[...])


def _mod_kernel(c_ref, w_ref, b_ref, o_ref):
    s = _silu(c_ref[...])
    o_ref[0] = jnp.dot(s, w_ref[0], preferred_element_type=F32, precision=HIGHEST) + b_ref[0]


def _modulation(cvec, mod_w, mod_b):
    depth, d, n = mod_w.shape
    rows = cvec.shape[0]
    tn = 1024
    return pl.pallas_call(
        _mod_kernel,
        out_shape=jax.ShapeDtypeStruct((depth, rows, n), F32),
        grid=(depth, n // tn),
        in_specs=[pl.BlockSpec((rows, d), lambda l, j: (0, 0)),
                  pl.BlockSpec((1, d, tn), lambda l, j: (l, 0, j)),
                  pl.BlockSpec((1, 1, tn), lambda l, j: (l, 0, j))],
        out_specs=pl.BlockSpec((1, rows, tn), lambda l, j: (l, 0, j)),
        compiler_params=_cparams("arbitrary", "arbitrary"),
        name="adaln_mod",
    )(cvec, mod_w, mod_b.reshape(depth, 1, n))


def _in_kernel(x_ref, modx_ref, modc_ref, g_ref, w_ref, o_ref, *, ctx_rule):
    bb, tm, d = x_ref.shape
    is_ctx = ctx_rule(pl.program_id(1))
    mod = _pick_mod(modx_ref, modc_ref, is_ctx)
    h = _norm_mod(x_ref[...], g_ref[...], mod[:, 0:1, :], mod[:, 1:2, :])
    o = _bdot(h.reshape(bb * tm, d), w_ref[...])
    o_ref[...] = o.reshape(bb, tm, -1).astype(o_ref.dtype)


def _in_proj(stream, modx, modc, g, w, *, t_off, n_t, ctx_rule, bb=2):
    b, _, d = stream.shape
    n = w.shape[1]
    return pl.pallas_call(
        functools.partial(_in_kernel, ctx_rule=ctx_rule),
        out_shape=jax.ShapeDtypeStruct((b, n_t * ROW_TILE, n), F32),
        grid=(b // bb, n_t),
        in_specs=[pl.BlockSpec((bb, ROW_TILE, d), lambda i, t: (i, t + t_off, 0)),
                  pl.BlockSpec((bb, 6, d), lambda i, t: (i, 0, 0)),
                  pl.BlockSpec((1, 6, d), lambda i, t: (0, 0, 0)),
                  pl.BlockSpec((1, d), lambda i, t: (0, 0)),
                  pl.BlockSpec((d, n), lambda i, t: (0, 0))],
        out_specs=pl.BlockSpec((bb, ROW_TILE, n), lambda i, t: (i, t, 0)),
        compiler_params=_cparams("parallel", "arbitrary"),
        name="in_proj",
    )(stream, modx, modc, g.reshape(1, d), w)


def _tile0_is_ctx(t):
    return t == 0


def _never_ctx(t):
    return t < 0


def _always_ctx(t):
    return t >= 0


def _s5_tables(lam_re, lam_im, log_dt, b_re, b_im, c_re, c_im, d_skip):
    t_len, n_st, p_ch = S5_CHUNK, S5_STATE, S5_GROUP_CH
    groups = lam_re.shape[1]
    dt = jnp.exp(log_dt.astype(F32))[..., None]
    lr = lam_re.astype(F32)
    li = lam_im.astype(F32)
    mag = jnp.exp(lr * dt)
    ar = mag * jnp.cos(li * dt)
    ai = mag * jnp.sin(li * dt)
    nr = ar - 1.0
    den = lr * lr + li * li
    kr = (nr * lr + ai * li) / den
    ki = (ai * lr - nr * li) / den
    br = b_re.astype(F32)
    bi = b_im.astype(F32)
    bbr = kr[..., None] * br - ki[..., None] * bi
    bbi = kr[..., None] * bi + ki[..., None] * br
    cr = c_re.astype(F32)
    ci = c_im.astype(F32)
    pr = [jnp.ones_like(ar)]
    pi = [jnp.zeros_like(ai)]
    for _ in range(t_len):
        pr.append(pr[-1] * ar - pi[-1] * ai)
        pi.append(pr[-2] * ai + pi[-1] * ar)
    pr = jnp.stack(pr)
    pi = jnp.stack(pi)
    er = cr[None] * pr[:, :, :, None, :] - ci[None] * pi[:, :, :, None, :]
    ei = cr[None] * pi[:, :, :, None, :] + ci[None] * pr[:, :, :, None, :]
    kern = (jnp.einsum('jdgqn,dgnp->jdgqp', er, bbr, precision=HIGHEST)
            - jnp.einsum('jdgqn,dgnp->jdgqp', ei, bbi, precision=HIGHEST))
    s_idx = np.arange(t_len)[:, None]
    t_idx = np.arange(t_len)[None, :]
    lag_f = np.clip(t_idx - s_idx, 0, t_len - 1)
    lag_b = np.clip(s_idx - t_idx, 0, t_len - 1)
    kf = kern[:, 0][lag_f]
    kb = kern[:, 1][lag_b]
    mask_f = jnp.asarray(s_idx <= t_idx, F32)[:, :, None, None, None]
    mask_b = jnp.asarray(s_idx >= t_idx, F32)[:, :, None, None, None]
    dmat = jnp.eye(p_ch, dtype=F32)[None] * d_skip.astype(F32).reshape(groups, 1, p_ch)
    eye_t = jnp.asarray(s_idx == t_idx, F32)[:, :, None, None, None]
    full = kf * mask_f + kb * mask_b + eye_t * dmat[None, None]
    toep = full.transpose(2, 0, 4, 1, 3).reshape(groups, t_len * p_ch, t_len * p_ch)
    def drive(pw_r, pw_i, d):
        re = pw_r[..., None] * bbr[d][None] - pw_i[..., None] * bbi[d][None]
        im = pw_r[..., None] * bbi[d][None] + pw_i[..., None] * bbr[d][None]
        return re.transpose(1, 0, 3, 2), im.transpose(1, 0, 3, 2)
    f_re, f_im = drive(pr[:t_len, 0][::-1], pi[:t_len, 0][::-1], 0)
    b_re2, b_im2 = drive(pr[:t_len, 1], pi[:t_len, 1], 1)
    wst = jnp.concatenate([f_re, f_im, b_re2, b_im2], axis=-1).reshape(groups, t_len * p_ch, 4 * n_st)
    def read(e_r, e_i):
        return e_r.transpose(1, 3, 0, 2), (-e_i).transpose(1, 3, 0, 2)
    of_re, of_im = read(er[1:, 0], ei[1:, 0])
    ob_re, ob_im = read(er[1:, 1][::-1], ei[1:, 1][::-1])
    wout = jnp.concatenate([of_re, of_im, ob_re, ob_im], axis=1).reshape(groups, 4 * n_st, t_len * p_ch)
    a_r = pr[t_len]
    a_i = pi[t_len]
    adec = jnp.stack([jnp.concatenate([a_r[0], a_r[0]], -1), jnp.concatenate([-a_i[0], a_i[0]], -1),
                      jnp.concatenate([a_r[1], a_r[1]], -1), jnp.concatenate([-a_i[1], a_i[1]], -1)], axis=1)
    return toep.astype(BF16), wst.astype(BF16), wout.astype(BF16), adec


def _s5_kernel(u_ref, toep_ref, wst_ref, wout_ref, a_ref, y_ref, s_scr, h_scr, *, nb, n_ctx, n_chunks, rows_blk):
    rows = u_ref.shape[1]
    n2 = 2 * S5_STATE
    toep = toep_ref[0]
    wst = wst_ref[0]
    for r0 in range(0, rows, rows_blk):
        u = u_ref[0, r0:r0 + rows_blk, :].astype(BF16)
        y_ref[0, r0:r0 + rows_blk, :] = jnp.dot(u, toep, preferred_element_type=F32)
        s_scr[r0:r0 + rows_blk, :] = jnp.dot(u, wst, preferred_element_type=F32)
    a = a_ref[0]
    af1, af2, ab1, ab2 = a[0:1], a[1:2], a[2:3], a[3:4]

    def step(i, carry):
        hf, hb = carry
        cb = jnp.where(i < n_ctx, n_ctx - 1 - i, n_chunks - 1 - (i - n_ctx))
        rf = pl.multiple_of(i * nb, nb)
        rb = pl.multiple_of(cb * nb, nb)
        h_scr[pl.ds(rf, nb), 0:n2] = hf
        h_scr[pl.ds(rb, nb), n2:2 * n2] = hb
        sf = s_scr[pl.ds(rf, nb), 0:n2]
        sb = s_scr[pl.ds(rb, nb), n2:2 * n2]
        hf = af1 * hf + af2 * pltpu.roll(hf, S5_STATE, 1) + sf
        hb = ab1 * hb + ab2 * pltpu.roll(hb, S5_STATE, 1) + sb
        return hf, hb

    zero = jnp.zeros((nb, n2), F32)
    lax.fori_loop(0, n_chunks, step, (zero, zero))
    wout = wout_ref[0]
    for r0 in range(0, rows, rows_blk):
        h = h_scr[r0:r0 + rows_blk, :].astype(BF16)
        y_ref[0, r0:r0 + rows_blk, :] += jnp.dot(h, wout, preferred_element_type=F32)


def _s5_scan(u, tables, n_ctx_tok):
    toep, wst, wout, adec = tables
    b, s, width = u.shape
    groups = width // S5_GROUP_CH
    n_chunks = s // S5_CHUNK
    cw = S5_CHUNK * S5_GROUP_CH
    rows = n_chunks * b
    ug = u.reshape(b, n_chunks, S5_CHUNK, groups, S5_GROUP_CH).transpose(3, 1, 0, 2, 4).reshape(groups, rows, cw)
    rows_blk = math.gcd(rows, 512)
    y = pl.pallas_call(
        functools.partial(_s5_kernel, nb=b, n_ctx=n_ctx_tok // S5_CHUNK, n_chunks=n_chunks, rows_blk=rows_blk),
        out_shape=jax.ShapeDtypeStruct((groups, rows, cw), F32),
        grid=(groups,),
        in_specs=[pl.BlockSpec((1, rows, cw), lambda g: (g, 0, 0)),
                  pl.BlockSpec((1, cw, cw), lambda g: (g, 0, 0)),
                  pl.BlockSpec((1, cw, 4 * S5_STATE), lambda g: (g, 0, 0)),
                  pl.BlockSpec((1, 4 * S5_STATE, cw), lambda g: (g, 0, 0)),
                  pl.BlockSpec((1, 4, 2 * S5_STATE), lambda g: (g, 0, 0))],
        out_specs=pl.BlockSpec((1, rows, cw), lambda g: (g, 0, 0)),
        scratch_shapes=[pltpu.VMEM((rows, 4 * S5_STATE), F32), pltpu.VMEM((rows, 4 * S5_STATE), F32)],
        compiler_params=_cparams("parallel"),
        name="s5_scan",
    )(ug, toep, wst, wout, adec)
    return y.reshape(groups, n_chunks, b, S5_CHUNK, S5_GROUP_CH).transpose(2, 1, 3, 0, 4).reshape(b, s, width)


def _qk_prep_kernel(p_ref, cos_ref, sin_ref, qg_ref, kg_ref, avg_ref, q_ref, k_ref, v_ref, *, s5_w, n_q, n_kv):
    dq = n_q * HEAD_DIM
    dk = n_kv * HEAD_DIM
    avg = avg_ref[...]

    def head_norm(z, gain):
        sq = z * z
        hi = sq.astype(BF16)
        lo = (sq - hi.astype(F32)).astype(BF16)
        w = avg[:z.shape[1], :z.shape[1]]
        ms = jnp.dot(hi, w, preferred_element_type=F32) + jnp.dot(lo, w, preferred_element_type=F32)
        return z * lax.rsqrt(ms + EPS) * gain

    def rope(z, cos, sin):
        lane = lax.broadcasted_iota(jnp.int32, z.shape, 1)
        first = (lane % (2 * ROPE_FREQS)) < ROPE_FREQS
        width = z.shape[1]
        partner = jnp.where(first, pltpu.roll(z, width - ROPE_FREQS, 1), pltpu.roll(z, ROPE_FREQS, 1))
        return z * cos + partner * sin

    p = p_ref[0]
    q = rope(head_norm(p[:, s5_w:s5_w + dq], qg_ref[...]), cos_ref[...], sin_ref[...])
    k = rope(head_norm(p[:, s5_w + dq:s5_w + dq + dk], kg_ref[:, :dk]), cos_ref[:, :dk], sin_ref[:, :dk])
    v = p[:, s5_w + dq + dk:s5_w + dq + 2 * dk]
    q = q * (HEAD_DIM ** -0.5)
    for h in range(n_q):
        q_ref[0, h] = q[:, h * HEAD_DIM:(h + 1) * HEAD_DIM].astype(BF16)
    for h in range(n_kv):
        k_ref[0, h] = k[:, h * HEAD_DIM:(h + 1) * HEAD_DIM].astype(BF16)
        v_ref[0, h] = v[:, h * HEAD_DIM:(h + 1) * HEAD_DIM].astype(BF16)


def _rope_tables(seq, n_ctx_tok, width):
    t = np.arange(seq)
    pos = np.stack([t // GRID_W, t % GRID_W], axis=-1).astype(np.float32)
    inv = (ROPE_BASE ** (-np.arange(ROPE_FREQS, dtype=np.float32) / ROPE_FREQS)).astype(np.float32)
    ang = jnp.asarray(pos)[:, :, None] * jnp.asarray(inv)
    cos = jnp.cos(ang)
    sin = jnp.sin(ang)
    cos_h = jnp.concatenate([cos, cos], axis=-1).reshape(seq, HEAD_DIM)
    sin_h = jnp.concatenate([-sin, sin], axis=-1).reshape(seq, HEAD_DIM)
    cos_h = jnp.concatenate([jnp.ones((n_ctx_tok, HEAD_DIM), F32), cos_h], axis=0)
    sin_h = jnp.concatenate([jnp.zeros((n_ctx_tok, HEAD_DIM), F32), sin_h], axis=0)
    reps = width // HEAD_DIM
    return jnp.tile(cos_h, (1, reps)), jnp.tile(sin_h, (1, reps))


def _qk_prep(proj, q_g, k_g, n_ctx_tok, *, s5_w, n_q, n_kv):
    b, s, n = proj.shape
    dq = n_q * HEAD_DIM
    cos, sin = _rope_tables(s - n_ctx_tok, n_ctx_tok, dq)
    avg = jnp.asarray(np.kron(np.eye(n_q, dtype=np.float32), np.full((HEAD_DIM, HEAD_DIM), 1.0 / HEAD_DIM, np.float32)), BF16)
    qg = jnp.tile(q_g.astype(F32), n_q).reshape(1, dq)
    kg = jnp.tile(k_g.astype(F32), n_q).reshape(1, dq)
    tm = ROW_TILE
    return pl.pallas_call(
        functools.partial(_qk_prep_kernel, s5_w=s5_w, n_q=n_q, n_kv=n_kv),
        out_shape=(jax.ShapeDtypeStruct((b, n_q, s, HEAD_DIM), BF16),
                   jax.ShapeDtypeStruct((b, n_kv, s, HEAD_DIM), BF16),
                   jax.ShapeDtypeStruct((b, n_kv, s, HEAD_DIM), BF16)),
        grid=(b, s // tm),
        in_specs=[pl.BlockSpec((1, tm, n), lambda i, t: (i, t, 0)),
                  pl.BlockSpec((tm, dq), lambda i, t: (t, 0)),
                  pl.BlockSpec((tm, dq), lambda i, t: (t, 0)),
                  pl.BlockSpec((1, dq), lambda i, t: (0, 0)),
                  pl.BlockSpec((1, dq), lambda i, t: (0, 0)),
                  pl.BlockSpec((dq, dq), lambda i, t: (0, 0))],
        out_specs=(pl.BlockSpec((1, n_q, tm, HEAD_DIM), lambda i, t: (i, 0, t, 0)),
                   pl.BlockSpec((1, n_kv, tm, HEAD_DIM), lambda i, t: (i, 0, t, 0)),
                   pl.BlockSpec((1, n_kv, tm, HEAD_DIM), lambda i, t: (i, 0, t, 0))),
        compiler_params=_cparams("parallel", "arbitrary"),
        name="qk_prep",
    )(proj, cos, sin, qg, kg, avg)


def _softmax_pv(s, v):
    m = jnp.max(s, axis=-1, keepdims=True)
    p = jnp.exp(s - m)
    l = jnp.sum(p, axis=-1, keepdims=True)
    return jnp.dot(p.astype(BF16), v, preferred_element_type=F32) / l


def _gqa_kernel(q_ref, k_ref, v_ref, o_ref, *, n_kv, grp, n_ctx_tok):
    tq = q_ref.shape[2]
    nt = (((1,), (1,)), ((), ()))

    def attend(n_keys):
        for h in range(n_kv):
            q2 = q_ref[0, h * grp:(h + 1) * grp].reshape(grp * tq, HEAD_DIM)
            s = lax.dot_general(q2, k_ref[0, h, 0:n_keys, :], nt, preferred_element_type=F32)
            o = _softmax_pv(s, v_ref[0, h, 0:n_keys, :])
            for j in range(grp):
                c0 = (h * grp + j) * HEAD_DIM
                o_ref[0, :, c0:c0 + HEAD_DIM] = o[j * tq:(j + 1) * tq].astype(o_ref.dtype)

    @pl.when(pl.program_id(1) == 0)
    def _():
        attend(n_ctx_tok)

    @pl.when(pl.program_id(1) != 0)
    def _():
        attend(k_ref.shape[2])


def _gqa_attention(q, k, v, n_ctx_tok):
    b, n_q, s, _ = q.shape
    n_kv = k.shape[1]
    tq = ROW_TILE
    return pl.pallas_call(
        functools.partial(_gqa_kernel, n_kv=n_kv, grp=n_q // n_kv, n_ctx_tok=n_ctx_tok),
        out_shape=jax.ShapeDtypeStruct((b, s, n_q * HEAD_DIM), BF16),
        grid=(b, s // tq),
        in_specs=[pl.BlockSpec((1, n_q, tq, HEAD_DIM), lambda i, t: (i, 0, t, 0)),
                  pl.BlockSpec((1, n_kv, s, HEAD_DIM), lambda i, t: (i, 0, 0, 0)),
                  pl.BlockSpec((1, n_kv, s, HEAD_DIM), lambda i, t: (i, 0, 0, 0))],
        out_specs=pl.BlockSpec((1, tq, n_q * HEAD_DIM), lambda i, t: (i, t, 0)),
        compiler_params=_cparams("parallel", "arbitrary"),
        name="gqa_attention",
    )(q, k, v)


def _gelu_tanh(y):
    return 0.5 * y * (1.0 + jnp.tanh(math.sqrt(2.0 / math.pi) * (y + 0.044715 * (y * y * y))))


def _mix_even_kernel(x_ref, y_ref, att_ref, modx_ref, modc_ref, wglu_ref, wa_ref, wb_ref, o_ref):
    bb, tm, d = x_ref.shape
    mod = _pick_mod(modx_ref, modc_ref, pl.program_id(1) == 0)
    y = y_ref[...].reshape(bb * tm, -1)
    g = _gelu_tanh(y)
    a = g * jax.nn.sigmoid(_bdot(g, wglu_ref[...]))
    ox = _bdot(a, wa_ref[...]) + jnp.dot(att_ref[...].reshape(bb * tm, -1), wb_ref[...], preferred_element_type=F32)
    o_ref[...] = x_ref[...] + mod[:, 2:3, :] * ox.reshape(bb, tm, d)


def _mix_even(stream, y_s5, att, modx, modc, w_glu, w_out, bb=2):
    b, s, d = stream.shape
    w5 = y_s5.shape[2]
    wa = att.shape[2]
    tm = ROW_TILE
    return pl.pallas_call(
        _mix_even_kernel,
        out_shape=jax.ShapeDtypeStruct((b, s, d), F32),
        grid=(b // bb, s // tm),
        in_specs=[pl.BlockSpec((bb, tm, d), lambda i, t: (i, t, 0)),
                  pl.BlockSpec((bb, tm, w5), lambda i, t: (i, t, 0)),
                  pl.BlockSpec((bb, tm, wa), lambda i, t: (i, t, 0)),
                  pl.BlockSpec((bb, 6, d), lambda i, t: (i, 0, 0)),
                  pl.BlockSpec((1, 6, d), lambda i, t: (0, 0, 0)),
                  pl.BlockSpec((w5, w5), lambda i, t: (0, 0)),
                  pl.BlockSpec((w5, d), lambda i, t: (0, 0)),
                  pl.BlockSpec((wa, d), lambda i, t: (1, 0))],
        out_specs=pl.BlockSpec((bb, tm, d), lambda i, t: (i, t, 0)),
        compiler_params=_cparams("parallel", "arbitrary"),
        name="mix_even",
    )(stream, y_s5, att, modx, modc, w_glu, w_out, w_out)


def _ffn_kernel(x_ref, modx_ref, modc_ref, g_ref, w1_ref, w3_ref, w2_ref, o_ref, h_scr, acc_scr):
    bb, tm, d = x_ref.shape
    j = pl.program_id(2)
    mod = _pick_mod(modx_ref, modc_ref, pl.program_id(1) == 0)

    @pl.when(j == 0)
    def _():
        h = _norm_mod(x_ref[...], g_ref[...], mod[:, 3:4, :], mod[:, 4:5, :])
        h_scr[...] = h.reshape(bb * tm, d).astype(BF16)
        acc_scr[...] = jnp.zeros_like(acc_scr)

    h = h_scr[...]
    a = jnp.dot(h, w1_ref[...], preferred_element_type=F32)
    g = jnp.dot(h, w3_ref[...], preferred_element_type=F32)
    acc_scr[...] += _bdot(_silu(a) * g, w2_ref[...])

    @pl.when(j == pl.num_programs(2) - 1)
    def _():
        o_ref[...] = x_ref[...] + mod[:, 5:6, :] * acc_scr[...].reshape(bb, tm, d)


def _ffn(stream, modx, modc, g, w1, w3, w2, bb=4, tf=512):
    b, s, d = stream.shape
    f = w1.shape[1]
    tm = ROW_TILE
    return pl.pallas_call(
        _ffn_kernel,
        out_shape=jax.ShapeDtypeStruct((b, s, d), F32),
        grid=(b // bb, s // tm, f // tf),
        in_specs=[pl.BlockSpec((bb, tm, d), lambda i, t, j: (i, t, 0)),
                  pl.BlockSpec((bb, 6, d), lambda i, t, j: (i, 0, 0)),
                  pl.BlockSpec((1, 6, d), lambda i, t, j: (0, 0, 0)),
                  pl.BlockSpec((1, d), lambda i, t, j: (0, 0)),
                  pl.BlockSpec((d, tf), lambda i, t, j: (0, j)),
                  pl.BlockSpec((d, tf), lambda i, t, j: (0, j)),
                  pl.BlockSpec((tf, d), lambda i, t, j: (j, 0))],
        out_specs=pl.BlockSpec((bb, tm, d), lambda i, t, j: (i, t, 0)),
        scratch_shapes=[pltpu.VMEM((bb * tm, d), BF16), pltpu.VMEM((bb * tm, d), F32)],
        compiler_params=_cparams("parallel", "arbitrary", "arbitrary"),
        name="ffn_dense",
    )(stream, modx, modc, g.reshape(1, d), w1, w3, w2)


def _even_layer(stream, modx, modc, n_ctx_tok, norm1_g, norm2_g, w_in, w_out, s5_params, w_glu, q_g, k_g,
                ffn_w1, ffn_w3, ffn_w2):
    s5_w = w_glu.shape[0]
    n_q = s5_w // HEAD_DIM
    n_kv = n_q // 2
    n_t = stream.shape[1] // ROW_TILE
    proj = _in_proj(stream, modx, modc, norm1_g, w_in.astype(BF16), t_off=0, n_t=n_t, ctx_rule=_tile0_is_ctx)
    y_s5 = _s5_scan(proj[:, :, :s5_w], _s5_tables(*s5_params), n_ctx_tok)
    q, k, v = _qk_prep(proj, q_g, k_g, n_ctx_tok, s5_w=s5_w, n_q=n_q, n_kv=n_kv)
    att = _gqa_attention(q, k, v, n_ctx_tok)
    stream = _mix_even(stream, y_s5, att, modx, modc, w_glu.astype(BF16), w_out.astype(BF16))
    return _ffn(stream, modx, modc, norm2_g, ffn_w1.astype(BF16), ffn_w3.astype(BF16), ffn_w2.astype(BF16))


HYENA_BANDS = 16
CONV_TILE = 256


def _hy_filter_kernel(z_ref, w1_ref, b1_ref, w2_ref, b2_ref, w3_ref, freq_ref, decay_ref, t_ref, h_ref):
    f = freq_ref[...]
    h = jnp.sin(f * (jnp.dot(z_ref[...], w1_ref[...], preferred_element_type=F32, precision=HIGHEST) + b1_ref[...]))
    h = jnp.sin(f * (jnp.dot(h, w2_ref[...], preferred_element_type=F32, precision=HIGHEST) + b2_ref[...]))
    h = jnp.dot(h, w3_ref[...], preferred_element_type=F32, precision=HIGHEST)
    h = h * jnp.exp(-t_ref[...] * jnp.abs(decay_ref[...]))
    h_ref[...] = h / (jnp.sum(jnp.abs(h), axis=0, keepdims=True) + EPS)


def _hy_filters(seq, f_w1, f_b1, f_w2, f_b2, f_w3, f_freq, f_decay):
    k = np.arange(seq, dtype=np.float32)
    t = k / max(seq - 1, 1)
    bands = np.linspace(1e-4, HYENA_BANDS - 1, HYENA_BANDS, dtype=np.float32)
    ang = jnp.asarray(np.float32(2.0 * math.pi / seq) * k[:, None] * bands[None, :])
    pos_dim, hidden = f_w1.shape
    zpad = 128
    z = jnp.concatenate([jnp.asarray(t)[:, None], jnp.cos(ang), -jnp.sin(ang),
                         jnp.zeros((seq, zpad - pos_dim), F32)], axis=-1)
    w1 = jnp.concatenate([f_w1.astype(F32), jnp.zeros((zpad - pos_dim, hidden), F32)], axis=0)
    n = f_w3.shape[1]
    tc = 512
    return pl.pallas_call(
        _hy_filter_kernel,
        out_shape=jax.ShapeDtypeStruct((seq, n), F32),
        grid=(n // tc,),
        in_specs=[pl.BlockSpec((seq, zpad), lambda j: (0, 0)),
                  pl.BlockSpec((zpad, hidden), lambda j: (0, 0)),
                  pl.BlockSpec((1, hidden), lambda j: (0, 0)),
                  pl.BlockSpec((hidden, hidden), lambda j: (0, 0)),
                  pl.BlockSpec((1, hidden), lambda j: (0, 0)),
                  pl.BlockSpec((hidden, tc), lambda j: (0, j)),
                  pl.BlockSpec((1, hidden), lambda j: (0, 0)),
                  pl.BlockSpec((1, tc), lambda j: (0, j)),
                  pl.BlockSpec((seq, 1), lambda j: (0, 0))],
        out_specs=pl.BlockSpec((seq, tc), lambda j: (0, j)),
        compiler_params=_cparams("arbitrary"),
        name="hyena_filter",
    )(z, w1, f_b1.reshape(1, hidden).astype(F32), f_w2.astype(F32), f_b2.reshape(1, hidden).astype(F32),
      f_w3.astype(F32), f_freq.reshape(1, hidden).astype(F32), f_decay.reshape(1, n).astype(F32),
      jnp.asarray(t)[:, None])


def _dft_tables(seq):
    idx = np.arange(seq, dtype=np.int64)
    m = jnp.asarray(((idx[:, None] * idx[None, :]) % (2 * seq)).astype(np.int32))
    ang = m.astype(F32) * np.float32(math.pi / seq)
    alt = jnp.asarray((1.0 - 2.0 * (idx % 2)).astype(np.float32))[:, None]
    return jnp.cos(ang).astype(BF16), jnp.sin(ang).astype(BF16), alt


def _split_bf16(a):
    hi = a.astype(BF16)
    return hi, (a - hi.astype(F32)).astype(BF16)


def _hy_spec_kernel(hf_ref, hb_ref, cos_ref, sin_ref, alt_ref, wk_ref, sre_ref, sim_ref, sny_ref, *, seq):
    hs = hf_ref[...] + hb_ref[...]
    hd = hb_ref[...] - hf_ref[...]
    s_hi, s_lo = _split_bf16(hs)
    d_hi, d_lo = _split_bf16(hd)
    c = cos_ref[...]
    s = sin_ref[...]
    wk = wk_ref[...]
    sre_ref[...] = wk * (jnp.dot(c, s_hi, preferred_element_type=F32) + jnp.dot(c, s_lo, preferred_element_type=F32))
    sim_ref[...] = wk * (jnp.dot(s, d_hi, preferred_element_type=F32) + jnp.dot(s, d_lo, preferred_element_type=F32))
    sny_ref[...] = jnp.sum(alt_ref[...] * hs, axis=0, keepdims=True) * (0.5 / seq)


def _hy_spectrum(h, cos, sin, alt):
    seq, n = h.shape
    half = n // 2
    tc = 128
    nb = half // tc
    wk = jnp.full((seq, 1), 1.0 / seq, F32).at[0, 0].set(0.5 / seq)
    return pl.pallas_call(
        functools.partial(_hy_spec_kernel, seq=seq),
        out_shape=(jax.ShapeDtypeStruct((seq, half), F32), jax.ShapeDtypeStruct((seq, half), F32),
                   jax.ShapeDtypeStruct((1, half), F32)),
        grid=(nb,),
        in_specs=[pl.BlockSpec((seq, tc), lambda j: (0, j)),
                  pl.BlockSpec((seq, tc), lambda j: (0, j + nb)),
                  pl.BlockSpec((seq, seq), lambda j: (0, 0), pipeline_mode=pl.Buffered(1)),
                  pl.BlockSpec((seq, seq), lambda j: (0, 0), pipeline_mode=pl.Buffered(1)),
                  pl.BlockSpec((seq, 1), lambda j: (0, 0)),
                  pl.BlockSpec((seq, 1), lambda j: (0, 0))],
        out_specs=(pl.BlockSpec((seq, tc), lambda j: (0, j)), pl.BlockSpec((seq, tc), lambda j: (0, j)),
                   pl.BlockSpec((1, tc), lambda j: (0, j))),
        compiler_params=_cparams("arbitrary"),
        name="hyena_spectrum",
    )(h, h, cos, sin, alt, wk)


def _hy_conv_kernel(z_ref, g_ref, cwz_ref, cbz_ref, cwg_ref, cbg_ref, cos_ref, sin_ref, alt_ref,
                    sre_ref, sim_ref, sny_ref, d_ref, o_ref, z_scr, zb_scr, yre_scr, yim_scr, *, conv_z, blk):
    seq = z_ref.shape[1]
    tc = z_ref.shape[2]
    row = lax.broadcasted_iota(jnp.int32, (blk, 1), 0)

    def short_conv(x_ref, w_ref, b_ref, l0):
        x = x_ref[0, l0:l0 + blk, :]
        w = w_ref[...]
        top = x_ref[0, l0 - 1:l0, :] if l0 > 0 else jnp.zeros((1, tc), F32)
        bot = x_ref[0, l0 + blk:l0 + blk + 1, :] if l0 + blk < seq else jnp.zeros((1, tc), F32)
        prev = jnp.where(row == 0, top, pltpu.roll(x, 1, 0))
        nxt = jnp.where(row == blk - 1, bot, pltpu.roll(x, blk - 1, 0))
        return prev * w[0:1] + x * w[1:2] + nxt * w[2:3] + b_ref[...]

    zny = jnp.zeros((1, tc), F32)
    for l0 in range(0, seq, blk):
        z = short_conv(z_ref, cwz_ref, cbz_ref, l0) if conv_z else z_ref[0, l0:l0 + blk, :]
        z_scr[l0:l0 + blk, :] = z
        zb_scr[l0:l0 + blk, :] = z.astype(BF16)
        zny = zny + jnp.sum(alt_ref[l0:l0 + blk, :] * z, axis=0, keepdims=True)
    zny = zny * sny_ref[...]
    for k0 in range(0, seq, blk):
        zc = jnp.dot(cos_ref[k0:k0 + blk, :], zb_scr[...], preferred_element_type=F32)
        zs = jnp.dot(sin_ref[k0:k0 + blk, :], zb_scr[...], preferred_element_type=F32)
        sre = sre_ref[k0:k0 + blk, :]
        sim = sim_ref[k0:k0 + blk, :]
        yre_scr[k0:k0 + blk, :] = (zc * sre + zs * sim).astype(BF16)
        yim_scr[k0:k0 + blk, :] = (zc * sim - zs * sre).astype(BF16)
    d = d_ref[0]
    for l0 in range(0, seq, blk):
        y = (jnp.dot(cos_ref[l0:l0 + blk, :], yre_scr[...], preferred_element_type=F32)
             - jnp.dot(sin_ref[l0:l0 + blk, :], yim_scr[...], preferred_element_type=F32)
             + alt_ref[l0:l0 + blk, :] * zny)
        gate = short_conv(g_ref, cwg_ref, cbg_ref, l0)
        o_ref[0, l0:l0 + blk, :] = gate * (y + d * z_scr[l0:l0 + blk, :])


def _hy_order(z_src, z_off, g_src, g_off, conv_w, conv_b, cos, sin, alt, sre, sim, sny, hy_d, order, width, conv_z):
    b, seq, _ = z_src.shape
    tc = CONV_TILE
    nb = width // tc
    zo, go = z_off // tc, g_off // tc

    def col(off):
        return lambda j, i: (0, j + off)

    return pl.pallas_call(
        functools.partial(_hy_conv_kernel, conv_z=conv_z, blk=512),
        out_shape=jax.ShapeDtypeStruct((b, seq, width), F32),
        grid=(nb, b),
        in_specs=[pl.BlockSpec((1, seq, tc), lambda j, i: (i, 0, j + zo)),
                  pl.BlockSpec((1, seq, tc), lambda j, i: (i, 0, j + go)),
                  pl.BlockSpec((3, tc), col(zo if conv_z else 0)), pl.BlockSpec((1, tc), col(zo if conv_z else 0)),
                  pl.BlockSpec((3, tc), col(go)), pl.BlockSpec((1, tc), col(go)),
                  pl.BlockSpec((seq, seq), lambda j, i: (0, 0)),
                  pl.BlockSpec((seq, seq), lambda j, i: (0, 0)),
                  pl.BlockSpec((seq, 1), lambda j, i: (0, 0)),
                  pl.BlockSpec((seq, tc), col(order * nb)),
                  pl.BlockSpec((seq, tc), col(order * nb)),
                  pl.BlockSpec((1, tc), col(order * nb)),
                  pl.BlockSpec((1, 1, tc), lambda j, i: (order, 0, j))],
        out_specs=pl.BlockSpec((1, seq, tc), lambda j, i: (i, 0, j)),
        scratch_shapes=[pltpu.VMEM((seq, tc), F32), pltpu.VMEM((seq, tc), BF16),
                        pltpu.VMEM((seq, tc), BF16), pltpu.VMEM((seq, tc), BF16)],
        compiler_params=_cparams("arbitrary", "arbitrary"),
        name="hyena_conv",
    )(z_src, g_src, conv_w, conv_b, conv_w, conv_b, cos, sin, alt, sre, sim, sny, hy_d.reshape(2, 1, width))


def _hy_conv(proj, conv_w, conv_b, cos, sin, alt, sre, sim, sny, hy_d, width):
    args = (conv_w, conv_b, cos, sin, alt, sre, sim, sny, hy_d)
    z1 = _hy_order(proj, 0, proj, width, *args, order=0, width=width, conv_z=True)
    return _hy_order(z1, 0, proj, 2 * width, *args, order=1, width=width, conv_z=False)


NA_WIN_ROWS = 8
NA_WIN_COLS = 16
NA_QROWS = ROW_TILE // GRID_W
NA_KTILES = 3


def _na_bias_tiles(rpb, rows):
    nqb = rows // NA_QROWS
    assert nqb >= 3 and rows >= NA_WIN_ROWS + NA_QROWS
    col = np.arange(GRID_W)
    col_start = np.clip(col - NA_WIN_COLS // 2, 0, GRID_W - NA_WIN_COLS)
    col_ok = (col[None, :] >= col_start[:, None]) & (col[None, :] < col_start[:, None] + NA_WIN_COLS)
    dc_idx = np.clip(col[None, :] - col[:, None] + NA_WIN_COLS - 1, 0, 2 * NA_WIN_COLS - 2)
    tiles = []
    for j in (0, 1, nqb - 1):
        kb0 = min(max(j - 1, 0), nqb - NA_KTILES)
        q_r = j * NA_QROWS + np.arange(NA_QROWS)
        k_r = kb0 * NA_QROWS + np.arange(NA_KTILES * NA_QROWS)
        r0 = np.clip(q_r - NA_WIN_ROWS // 2, 0, rows - NA_WIN_ROWS)
        row_ok = (k_r[None, :] >= r0[:, None]) & (k_r[None, :] < r0[:, None] + NA_WIN_ROWS)
        dr_idx = np.clip(k_r[None, :] - q_r[:, None] + NA_WIN_ROWS - 1, 0, 2 * NA_WIN_ROWS - 2)
        ok = row_ok[:, None, :, None] & col_ok[None, :, None, :]
        bias = rpb[:, dr_idx[:, None, :, None], dc_idx[None, :, None, :]].astype(F32)
        tile = jnp.where(jnp.asarray(ok)[None], bias, -jnp.inf)
        tiles.append(tile.reshape(rpb.shape[0], ROW_TILE, NA_KTILES * ROW_TILE))
    return jnp.stack(tiles, axis=1)


def _na_kernel(q_ref, k0_ref, k1_ref, k2_ref, v0_ref, v1_ref, v2_ref, kc_ref, vc_ref, bias_ref, o_ref, *, n_heads):
    nt = (((1,), (1,)), ((), ()))
    q = (q_ref[0] * (HEAD_DIM ** -0.5)).astype(BF16)
    k = jnp.concatenate([k0_ref[0], k1_ref[0], k2_ref[0]], axis=0).astype(BF16)
    v = jnp.concatenate([v0_ref[0], v1_ref[0], v2_ref[0]], axis=0).astype(BF16)
    kc = kc_ref[0].astype(BF16)
    vc = vc_ref[0].astype(BF16)
    for h in range(n_heads):
        sl = slice(h * HEAD_DIM, (h + 1) * HEAD_DIM)
        s_loc = lax.dot_general(q[:, sl], k[:, sl], nt, preferred_element_type=F32) + bias_ref[h, 0]
        s_ctx = lax.dot_general(q[:, sl], kc[:, sl], nt, preferred_element_type=F32)
        m = jnp.maximum(jnp.max(s_loc, axis=-1, keepdims=True), jnp.max(s_ctx, axis=-1, keepdims=True))
        p_loc = jnp.exp(s_loc - m)
        p_ctx = jnp.exp(s_ctx - m)
        l = jnp.sum(p_loc, axis=-1, keepdims=True) + jnp.sum(p_ctx, axis=-1, keepdims=True)
        o = (jnp.dot(p_loc.astype(BF16), v[:, sl], preferred_element_type=F32)
             + jnp.dot(p_ctx.astype(BF16), vc[:, sl], preferred_element_type=F32)) / l
        o_ref[0, :, sl] = o.astype(o_ref.dtype)


def _na_attention(proj, proj_c, rpb, *, q_off, n_heads):
    b, seq, _ = proj.shape
    w = n_heads * HEAD_DIM
    tm = ROW_TILE
    nqb = seq // tm
    bias = _na_bias_tiles(rpb, seq // GRID_W)
    qc = q_off // w

    def kb0(j):
        return jnp.clip(j - 1, 0, nqb - NA_KTILES)

    def kv_spec(cb, off):
        return pl.BlockSpec((1, tm, w), lambda i, j: (i, kb0(j) + off, cb))

    def bias_type(j):
        return jnp.where(j == 0, 0, jnp.where(j == nqb - 1, 2, 1))

    return pl.pallas_call(
        functools.partial(_na_kernel, n_heads=n_heads),
        out_shape=jax.ShapeDtypeStruct((b, seq, w), BF16),
        grid=(b, nqb),
        in_specs=[pl.BlockSpec((1, tm, w), lambda i, j: (i, j, qc)),
                  kv_spec(qc + 1, 0), kv_spec(qc + 1, 1), kv_spec(qc + 1, 2),
                  kv_spec(qc + 2, 0), kv_spec(qc + 2, 1), kv_spec(qc + 2, 2),
                  pl.BlockSpec((1, proj_c.shape[1], w), lambda i, j: (i, 0, 0)),
                  pl.BlockSpec((1, proj_c.shape[1], w), lambda i, j: (i, 0, 1)),
                  pl.BlockSpec((n_heads, 1, tm, NA_KTILES * tm), lambda i, j: (0, bias_type(j), 0, 0))],
        out_specs=pl.BlockSpec((1, tm, w), lambda i, j: (i, j, 0)),
        compiler_params=_cparams("parallel", "arbitrary"),
        name="na_attention",
    )(proj, proj, proj, proj, proj, proj, proj, proj_c, proj_c, bias)


MOE_TOP_K = 2


def _mix_odd_kernel(x_ref, hy_ref, na_ref, modx_ref, g_ref, wa_ref, wb_ref, rt_ref, x1_ref, h2_ref, idx_ref, wt_ref):
    bb, tm, d = x_ref.shape
    mod = modx_ref[...]
    ox = (_bdot(hy_ref[...].reshape(bb * tm, -1), wa_ref[...])
          + jnp.dot(na_ref[...].reshape(bb * tm, -1), wb_ref[...], preferred_element_type=F32))
    x1 = x_ref[...] + mod[:, 2:3, :] * ox.reshape(bb, tm, d)
    x1_ref[...] = x1
    h2 = _norm_mod(x1, g_ref[...], mod[:, 3:4, :], mod[:, 4:5, :])
    h2_ref[...] = h2
    lg = lax.dot_general(rt_ref[...], h2.reshape(bb * tm, d), (((1,), (1,)), ((), ())),
                         preferred_element_type=F32, precision=HIGHEST)
    n_e = lg.shape[0]
    eid = lax.broadcasted_iota(jnp.int32, lg.shape, 0)
    m1 = jnp.max(lg, axis=0, keepdims=True)
    i1 = jnp.min(jnp.where(lg == m1, eid, n_e), axis=0, keepdims=True)
    lg2 = jnp.where(eid == i1, -jnp.inf, lg)
    m2 = jnp.max(lg2, axis=0, keepdims=True)
    i2 = jnp.min(jnp.where(lg2 == m2, eid, n_e), axis=0, keepdims=True)
    e2 = jnp.exp(m2 - m1)
    den = 1.0 + e2
    idx_ref[...] = jnp.concatenate([i1, i2], axis=0)
    wt_ref[...] = jnp.concatenate([1.0 / den, e2 / den], axis=0)


def _mix_odd(stream, t_off, o_hy, o_na, modx, g, w_out, router):
    bb = 1
    b, _, d = stream.shape
    seq = o_hy.shape[1]
    wh = o_hy.shape[2]
    wn = o_na.shape[2]
    n_e = router.shape[1]
    tm = ROW_TILE
    n_t = seq // tm
    rows = bb * tm
    return pl.pallas_call(
        _mix_odd_kernel,
        out_shape=(jax.ShapeDtypeStruct((b, seq, d), F32), jax.ShapeDtypeStruct((b, seq, d), F32),
                   jax.ShapeDtypeStruct((MOE_TOP_K, b * seq), jnp.int32), jax.ShapeDtypeStruct((MOE_TOP_K, b * seq), F32)),
        grid=(b // bb, n_t),
        in_specs=[pl.BlockSpec((bb, tm, d), lambda i, t: (i, t + t_off, 0)),
                  pl.BlockSpec((bb, tm, wh), lambda i, t: (i, t, 0)),
                  pl.BlockSpec((bb, tm, wn), lambda i, t: (i, t, 0)),
                  pl.BlockSpec((bb, 6, d), lambda i, t: (i, 0, 0)),
                  pl.BlockSpec((1, d), lambda i, t: (0, 0)),
                  pl.BlockSpec((wh, d), lambda i, t: (0, 0)),
                  pl.BlockSpec((wn, d), lambda i, t: (1, 0)),
                  pl.BlockSpec((n_e, d), lambda i, t: (0, 0))],
        out_specs=(pl.BlockSpec((bb, tm, d), lambda i, t: (i, t, 0)),
                   pl.BlockSpec((bb, tm, d), lambda i, t: (i, t, 0)),
                   pl.BlockSpec((MOE_TOP_K, rows), lambda i, t: (0, i * n_t + t)),
                   pl.BlockSpec((MOE_TOP_K, rows), lambda i, t: (0, i * n_t + t))),
        compiler_params=_cparams("arbitrary", "arbitrary"),
        name="mix_odd_router",
    )(stream, o_hy, o_na, modx, g.reshape(1, d), w_out, w_out, router.T.astype(F32))


MOE_BLOCK = 512


def _moe_plan(idx_t, n_experts):
    k, t = idx_t.shape
    e_flat = idx_t.T.reshape(-1)
    onehot = (e_flat[:, None] == jnp.arange(n_experts)[None, :]).astype(jnp.int32)
    rank = jnp.sum((jnp.cumsum(onehot, axis=0) - onehot) * onehot, axis=1)
    counts = jnp.sum(onehot, axis=0)
    padded = (counts + MOE_BLOCK - 1) // MOE_BLOCK * MOE_BLOCK
    pend = jnp.cumsum(padded)
    pstart = pend - padded
    slot = pstart[e_flat] + rank
    n_blocks = (t * k) // MOE_BLOCK + n_experts
    tok = jnp.arange(t * k, dtype=jnp.int32) // k
    slot_tok = jnp.zeros((n_blocks * MOE_BLOCK,), jnp.int32).at[slot].set(tok)
    block_e = jnp.clip(jnp.searchsorted(pend, jnp.arange(n_blocks) * MOE_BLOCK, side='right'), 0, n_experts - 1)
    n_used = (pend[-1] // MOE_BLOCK).astype(jnp.int32).reshape(1)
    return (slot_tok.reshape(n_blocks, 1, MOE_BLOCK), block_e.astype(jnp.int32), n_used,
            slot.reshape(t, k).astype(jnp.int32))


def _moe_ffn_kernel(be_ref, nu_ref, tok_ref, h_hbm, w1_ref, w3_ref, w2_ref, y_ref, xg_scr, xb_scr, acc_scr, sem):
    i = pl.program_id(0)
    j = pl.program_id(1)
    used = i < nu_ref[0]
    rows = xg_scr.shape[0]

    def row_copy(r):
        return pltpu.make_async_copy(h_hbm.at[pl.ds(tok_ref[0, 0, r], 1)], xg_scr.at[pl.ds(r, 1)], sem)

    @pl.when(jnp.logical_and(used, j == 0))
    def _():
        def start(r, carry):
            row_copy(r).start()
            return carry
        lax.fori_loop(0, rows, start, 0)

        def wait(r, carry):
            row_copy(r).wait()
            return carry
        lax.fori_loop(0, rows, wait, 0)
        xb_scr[...] = xg_scr[...].astype(BF16)
        acc_scr[...] = jnp.zeros_like(acc_scr)

    @pl.when(used)
    def _():
        xb = xb_scr[...]
        a = jnp.dot(xb, w1_ref[0], preferred_element_type=F32)
        g = jnp.dot(xb, w3_ref[0], preferred_element_type=F32)
        acc_scr[...] += _bdot(_silu(a) * g, w2_ref[0])

    last = j == pl.num_programs(1) - 1

    @pl.when(jnp.logical_and(used, last))
    def _():
        y_ref[...] = acc_scr[...]

    @pl.when(jnp.logical_and(jnp.logical_not(used), last))
    def _():
        y_ref[...] = jnp.zeros_like(y_ref)


def _moe_ffn(h2, slot_tok, block_e, n_used, w1, w3, w2, tf=512):
    t, d = h2.shape
    n_blocks = slot_tok.shape[0]
    f = w1.shape[2]
    mb = MOE_BLOCK
    grid_spec = pltpu.PrefetchScalarGridSpec(
        num_scalar_prefetch=2,
        grid=(n_blocks, f // tf),
        in_specs=[pl.BlockSpec((1, 1, mb), lambda i, j, be, nu: (i, 0, 0), memory_space=pltpu.SMEM),
                  pl.BlockSpec(memory_space=pl.ANY),
                  pl.BlockSpec((1, d, tf), lambda i, j, be, nu: (be[i], 0, j)),
                  pl.BlockSpec((1, d, tf), lambda i, j, be, nu: (be[i], 0, j)),
                  pl.BlockSpec((1, tf, d), lambda i, j, be, nu: (be[i], j, 0))],
        out_specs=pl.BlockSpec((mb, d), lambda i, j, be, nu: (i, 0)),
        scratch_shapes=[pltpu.VMEM((mb, d), F32), pltpu.VMEM((mb, d), BF16), pltpu.VMEM((mb, d), F32),
                        pltpu.SemaphoreType.DMA(())],
    )
    return pl.pallas_call(
        _moe_ffn_kernel,
        out_shape=jax.ShapeDtypeStruct((n_blocks * mb, d), F32),
        grid_spec=grid_spec,
        compiler_params=_cparams("arbitrary", "arbitrary"),
        name="moe_expert_ffn",
    )(block_e, n_used, slot_tok, h2, w1, w3, w2)


def _moe_combine_kernel(slot_ref, y_hbm, x1_ref, wt_ref, modx_ref, g_ref, o_ref, ya_scr, yb_scr, sem):
    tm = x1_ref.shape[1]

    def copies(r):
        return (pltpu.make_async_copy(y_hbm.at[pl.ds(slot_ref[0, 0, r], 1)], ya_scr.at[pl.ds(r, 1)], sem),
                pltpu.make_async_copy(y_hbm.at[pl.ds(slot_ref[0, 0, tm + r], 1)], yb_scr.at[pl.ds(r, 1)], sem))

    def start(r, carry):
        for cp in copies(r):
            cp.start()
        return carry
    lax.fori_loop(0, tm, start, 0)

    def wait(r, carry):
        for cp in copies(r):
            cp.wait()
        return carry
    lax.fori_loop(0, tm, wait, 0)
    wt = wt_ref[...]
    y = wt[:, 0:1] * ya_scr[...] + wt[:, 1:2] * yb_scr[...]
    x2 = x1_ref[0] + modx_ref[0, 5:6, :] * y
    o_ref[0] = (x2 * lax.rsqrt(jnp.mean(x2 * x2, axis=-1, keepdims=True) + EPS)) * g_ref[...]


def _moe_combine(y_slots, slot_of, wt, x1, modx, final_g):
    b, seq, d = x1.shape
    tm = ROW_TILE
    n_t = seq // tm
    slots = slot_of.reshape(b * n_t, tm, MOE_TOP_K).transpose(0, 2, 1).reshape(b * n_t, 1, MOE_TOP_K * tm)
    return pl.pallas_call(
        _moe_combine_kernel,
        out_shape=jax.ShapeDtypeStruct((b, seq, d), F32),
        grid=(b, n_t),
        in_specs=[pl.BlockSpec((1, 1, MOE_TOP_K * tm), lambda i, t: (i * n_t + t, 0, 0), memory_space=pltpu.SMEM),
                  pl.BlockSpec(memory_space=pl.ANY),
                  pl.BlockSpec((1, tm, d), lambda i, t: (i, t, 0)),
                  pl.BlockSpec((tm, MOE_TOP_K), lambda i, t: (i * n_t + t, 0)),
                  pl.BlockSpec((1, 6, d), lambda i, t: (i, 0, 0)),
                  pl.BlockSpec((1, d), lambda i, t: (0, 0))],
        out_specs=pl.BlockSpec((1, tm, d), lambda i, t: (i, t, 0)),
        scratch_shapes=[pltpu.VMEM((tm, d), F32), pltpu.VMEM((tm, d), F32), pltpu.SemaphoreType.DMA(())],
        compiler_params=_cparams("arbitrary", "arbitrary"),
        name="moe_combine_norm",
    )(slots, y_slots, x1, wt, modx, final_g.reshape(1, d))


def _odd_layer(stream, modx, modc, n_ctx_tok, norm1_g, norm2_g, w_in, w_out, conv_w, conv_b, filt, hy_d, rpb,
               router, moe_w1, moe_w3, moe_w2, final_g):
    b, s, d = stream.shape
    seq = s - n_ctx_tok
    hy_w = hy_d.shape[1]
    n_heads = rpb.shape[0]
    na_w = n_heads * HEAD_DIM
    t_off = n_ctx_tok // ROW_TILE
    w_in_b = w_in.astype(BF16)
    proj = _in_proj(stream, modx, modc, norm1_g, w_in_b, t_off=t_off, n_t=seq // ROW_TILE, ctx_rule=_never_ctx)
    proj_c = _in_proj(stream, modx, modc, norm1_g, w_in_b[:, 3 * hy_w + na_w:], t_off=0, n_t=t_off,
                      ctx_rule=_always_ctx)
    cos, sin, alt = _dft_tables(seq)
    sre, sim, sny = _hy_spectrum(_hy_filters(seq, *filt), cos, sin, alt)
    o_hy = _hy_conv(proj, conv_w.astype(F32), conv_b.reshape(1, -1).astype(F32), cos, sin, alt, sre, sim, sny,
                    hy_d.astype(F32), hy_w)
    o_na = _na_attention(proj, proj_c, rpb, q_off=3 * hy_w, n_heads=n_heads)
    x1, h2, idx_t, wt_t = _mix_odd(stream, t_off, o_hy, o_na, modx, norm2_g, w_out.astype(BF16), router)
    slot_tok, block_e, n_used, slot_of = _moe_plan(idx_t, router.shape[1])
    y_slots = _moe_ffn(h2.reshape(b * seq, d), slot_tok, block_e, n_used,
                       moe_w1.astype(BF16), moe_w3.astype(BF16), moe_w2.astype(BF16))
    return _moe_combine(y_slots, slot_of, wt_t.T, x1, modx, final_g)


def kernel(x, c, ctx, c_ctx, mod_w, mod_b, norm1_g, norm2_g, ev_w_in, ev_w_out, s5_lam_re, s5_lam_im, s5_log_dt, s5_b_re, s5_b_im, s5_c_re, s5_c_im, s5_d, s5_w_glu, gqa_q_g, gqa_k_g, ffn_w1, ffn_w3, ffn_w2, od_w_in, od_w_out, hy_conv_w, hy_conv_b, hy_w1, hy_b1, hy_w2, hy_b2, hy_w3, hy_freq, hy_decay, hy_d, na_rpb, moe_router, moe_w1, moe_w3, moe_w2, final_g):
    b, seq, d = x.shape
    n_ctx_tok = ctx.shape[1]
    assert n_ctx_tok == ROW_TILE and seq % ROW_TILE == 0
    stream = jnp.concatenate([ctx, x], axis=1)
    rows = 8 * ((b + 1 + 7) // 8)
    cvec = jnp.zeros((rows, d), F32).at[:b].set(c).at[b].set(c_ctx)
    m = _modulation(cvec, mod_w, mod_b)
    modx = [m[l, :b].reshape(b, 6, d) for l in range(2)]
    modc = [m[l, b].reshape(1, 6, d) for l in range(2)]
    s5p = (s5_lam_re[0], s5_lam_im[0], s5_log_dt[0], s5_b_re[0], s5_b_im[0], s5_c_re[0], s5_c_im[0], s5_d[0])
    stream = _even_layer(stream, modx[0], modc[0], n_ctx_tok, norm1_g[0], norm2_g[0], ev_w_in[0], ev_w_out[0], s5p,
                         s5_w_glu[0], gqa_q_g[0], gqa_k_g[0], ffn_w1[0], ffn_w3[0], ffn_w2[0])
    filt = (hy_w1[0], hy_b1[0], hy_w2[0], hy_b2[0], hy_w3[0], hy_freq[0], hy_decay[0])
    return _odd_layer(stream, modx[1], modc[1], n_ctx_tok, norm1_g[1], norm2_g[1], od_w_in[0], od_w_out[0],
                      hy_conv_w[0], hy_conv_b[0], filt, hy_d[0], na_rpb[0], moe_router[0],
                      moe_w1[0], moe_w3[0], moe_w2[0], final_g)
```

```python
import functools
import math

import jax
import jax.numpy as jnp
import numpy as np
from jax import lax
from jax.experimental import pallas as pl
from jax.experimental.pallas import tpu as pltpu

F32 = jnp.float32
BF16 = jnp.bfloat16
EPS = 1e-6
HEAD_DIM = 64
GRID_W = 64
ROPE_FREQS = HEAD_DIM // 4
ROPE_BASE = 10000.0
ROW_TILE = 256
S5_GROUP_CH = 16
S5_STATE = 64
S5_CHUNK = 16
VMEM_LIMIT = 56 * 1024 * 1024
HIGHEST = lax.Precision.HIGHEST


def _cparams(*sem):
    return pltpu.CompilerParams(dimension_semantics=sem, vmem_limit_bytes=VMEM_LIMIT)


def _bdot(a, b):
    return jnp.dot(a.astype(BF16), b.astype(BF16), preferred_element_type=F32)


def _silu(a):
    return a * jax.nn.sigmoid(a)


def _norm_mod(x, g, shift, scale):
    y = x * lax.rsqrt(jnp.mean(x * x, axis=-1, keepdims=True) + EPS)
    return (y * g) * (1.0 + scale) + shift


def _pick_mod(modx_ref, modc_ref, is_ctx):
    return jnp.where(is_ctx, modc_ref[...], modx_ref[...])


def _mod_kernel(c_ref, w_ref, b_ref, o_ref):
    s = _silu(c_ref[...])
    o_ref[0] = jnp.dot(s, w_ref[0], preferred_element_type=F32, precision=HIGHEST) + b_ref[0]


def _modulation(cvec, mod_w, mod_b):
    depth, d, n = mod_w.shape
    rows = cvec.shape[0]
    tn = 1024
    return pl.pallas_call(
        _mod_kernel,
        out_shape=jax.ShapeDtypeStruct((depth, rows, n), F32),
        grid=(depth, n // tn),
        in_specs=[pl.BlockSpec((rows, d), lambda l, j: (0, 0)),
                  pl.BlockSpec((1, d, tn), lambda l, j: (l, 0, j)),
                  pl.BlockSpec((1, 1, tn), lambda l, j: (l, 0, j))],
        out_specs=pl.BlockSpec((1, rows, tn), lambda l, j: (l, 0, j)),
        compiler_params=_cparams("arbitrary", "arbitrary"),
        name="adaln_mod",
    )(cvec, mod_w, mod_b.reshape(depth, 1, n))


def _stream_specs(srcs, bb, t_off=0):
    d = srcs[0].shape[2]
    if len(srcs) == 1:
        return [pl.BlockSpec((bb, ROW_TILE, d), lambda i, t, *_: (i, t + t_off, 0))]
    return [pl.BlockSpec((bb, ROW_TILE, d), lambda i, t, *_: (i, 0, 0)),
            pl.BlockSpec((bb, ROW_TILE, d), lambda i, t, *_: (i, jnp.maximum(t - 1, 0), 0))]


def _stream_tile(refs, is_ctx):
    if len(refs) == 1:
        return refs[0][...]
    return jnp.where(is_ctx, refs[0][...], refs[1][...])


def _in_kernel(*refs, ctx_rule, n_src):
    src, (modx_ref, modc_ref, g_ref, w_ref, o_ref) = refs[:n_src], refs[n_src:]
    bb, tm, d = src[0].shape
    is_ctx = ctx_rule(pl.program_id(1))
    mod = _pick_mod(modx_ref, modc_ref, is_ctx)
    h = _norm_mod(_stream_tile(src, is_ctx), g_ref[...], mod[:, 0:1, :], mod[:, 1:2, :])
    o = _bdot(h.reshape(bb * tm, d), w_ref[...])
    o_ref[...] = o.reshape(bb, tm, -1).astype(o_ref.dtype)


def _in_proj(srcs, modx, modc, g, w, *, t_off, n_t, ctx_rule, bb=2):
    b, _, d = srcs[0].shape
    n = w.shape[1]
    return pl.pallas_call(
        functools.partial(_in_kernel, ctx_rule=ctx_rule, n_src=len(srcs)),
        out_shape=jax.ShapeDtypeStruct((b, n_t * ROW_TILE, n), F32),
        grid=(b // bb, n_t),
        in_specs=_stream_specs(srcs, bb, t_off) + [
            pl.BlockSpec((bb, 6, d), lambda i, t: (i, 0, 0)),
            pl.BlockSpec((1, 6, d), lambda i, t: (0, 0, 0)),
            pl.BlockSpec((1, d), lambda i, t: (0, 0)),
            pl.BlockSpec((d, n), lambda i, t: (0, 0))],
        out_specs=pl.BlockSpec((bb, ROW_TILE, n), lambda i, t: (i, t, 0)),
        compiler_params=_cparams("parallel", "arbitrary"),
        name="in_proj",
    )(*srcs, modx, modc, g.reshape(1, d), w)


def _tile0_is_ctx(t):
    return t == 0


def _never_ctx(t):
    return t < 0


def _always_ctx(t):
    return t >= 0


def _s5_tables(lam_re, lam_im, log_dt, b_re, b_im, c_re, c_im, d_skip):
    t_len, n_st, p_ch = S5_CHUNK, S5_STATE, S5_GROUP_CH
    groups = lam_re.shape[1]
    dt = jnp.exp(log_dt.astype(F32))[..., None]
    lr = lam_re.astype(F32)
    li = lam_im.astype(F32)
    mag = jnp.exp(lr * dt)
    ar = mag * jnp.cos(li * dt)
    ai = mag * jnp.sin(li * dt)
    nr = ar - 1.0
    den = lr * lr + li * li
    kr = (nr * lr + ai * li) / den
    ki = (ai * lr - nr * li) / den
    br = b_re.astype(F32)
    bi = b_im.astype(F32)
    bbr = kr[..., None] * br - ki[..., None] * bi
    bbi = kr[..., None] * bi + ki[..., None] * br
    cr = c_re.astype(F32)
    ci = c_im.astype(F32)
    pr = [jnp.ones_like(ar)]
    pi = [jnp.zeros_like(ai)]
    for _ in range(t_len):
        pr.append(pr[-1] * ar - pi[-1] * ai)
        pi.append(pr[-2] * ai + pi[-1] * ar)
    pr = jnp.stack(pr)
    pi = jnp.stack(pi)
    er = cr[None] * pr[:, :, :, None, :] - ci[None] * pi[:, :, :, None, :]
    ei = cr[None] * pi[:, :, :, None, :] + ci[None] * pr[:, :, :, None, :]
    kern = (jnp.einsum('jdgqn,dgnp->jdgqp', er, bbr, precision=HIGHEST)
            - jnp.einsum('jdgqn,dgnp->jdgqp', ei, bbi, precision=HIGHEST))
    s_idx = np.arange(t_len)[:, None]
    t_idx = np.arange(t_len)[None, :]
    lag_f = np.clip(t_idx - s_idx, 0, t_len - 1)
    lag_b = np.clip(s_idx - t_idx, 0, t_len - 1)
    kf = kern[:, 0][lag_f]
    kb = kern[:, 1][lag_b]
    mask_f = jnp.asarray(s_idx <= t_idx, F32)[:, :, None, None, None]
    mask_b = jnp.asarray(s_idx >= t_idx, F32)[:, :, None, None, None]
    dmat = jnp.eye(p_ch, dtype=F32)[None] * d_skip.astype(F32).reshape(groups, 1, p_ch)
    eye_t = jnp.asarray(s_idx == t_idx, F32)[:, :, None, None, None]
    full = kf * mask_f + kb * mask_b + eye_t * dmat[None, None]
    toep = full.transpose(2, 0, 4, 1, 3).reshape(groups, t_len * p_ch, t_len * p_ch)
    def drive(pw_r, pw_i, d):
        re = pw_r[..., None] * bbr[d][None] - pw_i[..., None] * bbi[d][None]
        im = pw_r[..., None] * bbi[d][None] + pw_i[..., None] * bbr[d][None]
        return re.transpose(1, 0, 3, 2), im.transpose(1, 0, 3, 2)
    f_re, f_im = drive(pr[:t_len, 0][::-1], pi[:t_len, 0][::-1], 0)
    b_re2, b_im2 = drive(pr[:t_len, 1], pi[:t_len, 1], 1)
    wst = jnp.concatenate([f_re, f_im, b_re2, b_im2], axis=-1).reshape(groups, t_len * p_ch, 4 * n_st)
    def read(e_r, e_i):
        return e_r.transpose(1, 3, 0, 2), (-e_i).transpose(1, 3, 0, 2)
    of_re, of_im = read(er[1:, 0], ei[1:, 0])
    ob_re, ob_im = read(er[1:, 1][::-1], ei[1:, 1][::-1])
    wout = jnp.concatenate([of_re, of_im, ob_re, ob_im], axis=1).reshape(groups, 4 * n_st, t_len * p_ch)
    a_r = pr[t_len]
    a_i = pi[t_len]
    adec = jnp.stack([jnp.concatenate([a_r[0], a_r[0]], -1), jnp.concatenate([-a_i[0], a_i[0]], -1),
                      jnp.concatenate([a_r[1], a_r[1]], -1), jnp.concatenate([-a_i[1], a_i[1]], -1)], axis=1)
    return toep.astype(BF16), wst.astype(BF16), wout.astype(BF16), adec


def _s5_kernel(u_ref, toep_ref, wst_ref, wout_ref, a_ref, y_ref, s_scr, h_scr, *, nb, n_ctx, n_chunks, rows_blk):
    rows = u_ref.shape[1]
    n2 = 2 * S5_STATE
    toep = toep_ref[0]
    wst = wst_ref[0]
    for r0 in range(0, rows, rows_blk):
        u = u_ref[0, r0:r0 + rows_blk, :].astype(BF16)
        y_ref[0, r0:r0 + rows_blk, :] = jnp.dot(u, toep, preferred_element_type=F32)
        s_scr[r0:r0 + rows_blk, :] = jnp.dot(u, wst, preferred_element_type=F32)
    a = a_ref[0]
    af1, af2, ab1, ab2 = a[0:1], a[1:2], a[2:3], a[3:4]

    def step(i, carry):
        hf, hb = carry
        cb = jnp.where(i < n_ctx, n_ctx - 1 - i, n_chunks - 1 - (i - n_ctx))
        rf = pl.multiple_of(i * nb, nb)
        rb = pl.multiple_of(cb * nb, nb)
        h_scr[pl.ds(rf, nb), 0:n2] = hf
        h_scr[pl.ds(rb, nb), n2:2 * n2] = hb
        sf = s_scr[pl.ds(rf, nb), 0:n2]
        sb = s_scr[pl.ds(rb, nb), n2:2 * n2]
        hf = af1 * hf + af2 * pltpu.roll(hf, S5_STATE, 1) + sf
        hb = ab1 * hb + ab2 * pltpu.roll(hb, S5_STATE, 1) + sb
        return hf, hb

    zero = jnp.zeros((nb, n2), F32)
    lax.fori_loop(0, n_chunks, step, (zero, zero))
    wout = wout_ref[0]
    for r0 in range(0, rows, rows_blk):
        h = h_scr[r0:r0 + rows_blk, :].astype(BF16)
        y_ref[0, r0:r0 + rows_blk, :] += jnp.dot(h, wout, preferred_element_type=F32)


def _s5_scan(u, tables, n_ctx_tok):
    toep, wst, wout, adec = tables
    b, s, width = u.shape
    groups = width // S5_GROUP_CH
    n_chunks = s // S5_CHUNK
    cw = S5_CHUNK * S5_GROUP_CH
    rows = n_chunks * b
    ug = u.reshape(b, n_chunks, S5_CHUNK, groups, S5_GROUP_CH).transpose(3, 1, 0, 2, 4).reshape(groups, rows, cw)
    rows_blk = math.gcd(rows, 512)
    y = pl.pallas_call(
        functools.partial(_s5_kernel, nb=b, n_ctx=n_ctx_tok // S5_CHUNK, n_chunks=n_chunks, rows_blk=rows_blk),
        out_shape=jax.ShapeDtypeStruct((groups, rows, cw), F32),
        grid=(groups,),
        in_specs=[pl.BlockSpec((1, rows, cw), lambda g: (g, 0, 0)),
                  pl.BlockSpec((1, cw, cw), lambda g: (g, 0, 0)),
                  pl.BlockSpec((1, cw, 4 * S5_STATE), lambda g: (g, 0, 0)),
                  pl.BlockSpec((1, 4 * S5_STATE, cw), lambda g: (g, 0, 0)),
                  pl.BlockSpec((1, 4, 2 * S5_STATE), lambda g: (g, 0, 0))],
        out_specs=pl.BlockSpec((1, rows, cw), lambda g: (g, 0, 0)),
        scratch_shapes=[pltpu.VMEM((rows, 4 * S5_STATE), F32), pltpu.VMEM((rows, 4 * S5_STATE), F32)],
        compiler_params=_cparams("parallel"),
        name="s5_scan",
    )(ug, toep, wst, wout, adec)
    return y.reshape(groups, n_chunks, b, S5_CHUNK, S5_GROUP_CH).transpose(2, 1, 3, 0, 4).reshape(b, s, width)


def _qk_prep_kernel(p_ref, cos_ref, sin_ref, qg_ref, kg_ref, avg_ref, q_ref, k_ref, v_ref, *, s5_w, n_q, n_kv):
    dq = n_q * HEAD_DIM
    dk = n_kv * HEAD_DIM
    avg = avg_ref[...]

    def head_norm(z, gain):
        sq = z * z
        hi = sq.astype(BF16)
        lo = (sq - hi.astype(F32)).astype(BF16)
        w = avg[:z.shape[1], :z.shape[1]]
        ms = jnp.dot(hi, w, preferred_element_type=F32) + jnp.dot(lo, w, preferred_element_type=F32)
        return z * lax.rsqrt(ms + EPS) * gain

    def rope(z, cos, sin):
        lane = lax.broadcasted_iota(jnp.int32, z.shape, 1)
        first = (lane % (2 * ROPE_FREQS)) < ROPE_FREQS
        width = z.shape[1]
        partner = jnp.where(first, pltpu.roll(z, width - ROPE_FREQS, 1), pltpu.roll(z, ROPE_FREQS, 1))
        return z * cos + partner * sin

    p = p_ref[0]
    q = rope(head_norm(p[:, s5_w:s5_w + dq], qg_ref[...]), cos_ref[...], sin_ref[...])
    k = rope(head_norm(p[:, s5_w + dq:s5_w + dq + dk], kg_ref[:, :dk]), cos_ref[:, :dk], sin_ref[:, :dk])
    v = p[:, s5_w + dq + dk:s5_w + dq + 2 * dk]
    q = q * (HEAD_DIM ** -0.5)
    for h in range(n_q):
        q_ref[0, h] = q[:, h * HEAD_DIM:(h + 1) * HEAD_DIM].astype(BF16)
    for h in range(n_kv):
        k_ref[0, h] = k[:, h * HEAD_DIM:(h + 1) * HEAD_DIM].astype(BF16)
        v_ref[0, h] = v[:, h * HEAD_DIM:(h + 1) * HEAD_DIM].astype(BF16)


def _rope_tables(seq, n_ctx_tok, width):
    t = np.arange(seq)
    pos = np.stack([t // GRID_W, t % GRID_W], axis=-1).astype(np.float32)
    inv = (ROPE_BASE ** (-np.arange(ROPE_FREQS, dtype=np.float32) / ROPE_FREQS)).astype(np.float32)
    ang = jnp.asarray(pos)[:, :, None] * jnp.asarray(inv)
    cos = jnp.cos(ang)
    sin = jnp.sin(ang)
    cos_h = jnp.concatenate([cos, cos], axis=-1).reshape(seq, HEAD_DIM)
    sin_h = jnp.concatenate([-sin, sin], axis=-1).reshape(seq, HEAD_DIM)
    cos_h = jnp.concatenate([jnp.ones((n_ctx_tok, HEAD_DIM), F32), cos_h], axis=0)
    sin_h = jnp.concatenate([jnp.zeros((n_ctx_tok, HEAD_DIM), F32), sin_h], axis=0)
    reps = width // HEAD_DIM
    return jnp.tile(cos_h, (1, reps)), jnp.tile(sin_h, (1, reps))


def _qk_prep(proj, q_g, k_g, n_ctx_tok, *, s5_w, n_q, n_kv):
    b, s, n = proj.shape
    dq = n_q * HEAD_DIM
    cos, sin = _rope_tables(s - n_ctx_tok, n_ctx_tok, dq)
    avg = jnp.asarray(np.kron(np.eye(n_q, dtype=np.float32), np.full((HEAD_DIM, HEAD_DIM), 1.0 / HEAD_DIM, np.float32)), BF16)
    qg = jnp.tile(q_g.astype(F32), n_q).reshape(1, dq)
    kg = jnp.tile(k_g.astype(F32), n_q).reshape(1, dq)
    tm = ROW_TILE
    return pl.pallas_call(
        functools.partial(_qk_prep_kernel, s5_w=s5_w, n_q=n_q, n_kv=n_kv),
        out_shape=(jax.ShapeDtypeStruct((b, n_q, s, HEAD_DIM), BF16),
                   jax.ShapeDtypeStruct((b, n_kv, s, HEAD_DIM), BF16),
                   jax.ShapeDtypeStruct((b, n_kv, s, HEAD_DIM), BF16)),
        grid=(b, s // tm),
        in_specs=[pl.BlockSpec((1, tm, n), lambda i, t: (i, t, 0)),
                  pl.BlockSpec((tm, dq), lambda i, t: (t, 0)),
                  pl.BlockSpec((tm, dq), lambda i, t: (t, 0)),
                  pl.BlockSpec((1, dq), lambda i, t: (0, 0)),
                  pl.BlockSpec((1, dq), lambda i, t: (0, 0)),
                  pl.BlockSpec((dq, dq), lambda i, t: (0, 0))],
        out_specs=(pl.BlockSpec((1, n_q, tm, HEAD_DIM), lambda i, t: (i, 0, t, 0)),
                   pl.BlockSpec((1, n_kv, tm, HEAD_DIM), lambda i, t: (i, 0, t, 0)),
                   pl.BlockSpec((1, n_kv, tm, HEAD_DIM), lambda i, t: (i, 0, t, 0))),
        compiler_params=_cparams("parallel", "arbitrary"),
        name="qk_prep",
    )(proj, cos, sin, qg, kg, avg)


def _softmax_pv(s, v):
    m = jnp.max(s, axis=-1, keepdims=True)
    p = jnp.exp(s - m)
    l = jnp.sum(p, axis=-1, keepdims=True)
    return jnp.dot(p.astype(BF16), v, preferred_element_type=F32) / l


def _gqa_kernel(q_ref, k_ref, v_ref, o_ref, *, n_kv, grp, n_ctx_tok):
    tq = q_ref.shape[2]
    nt = (((1,), (1,)), ((), ()))

    def attend(n_keys):
        for h in range(n_kv):
            q2 = q_ref[0, h * grp:(h + 1) * grp].reshape(grp * tq, HEAD_DIM)
            s = lax.dot_general(q2, k_ref[0, h, 0:n_keys, :], nt, preferred_element_type=F32)
            o = _softmax_pv(s, v_ref[0, h, 0:n_keys, :])
            for j in range(grp):
                c0 = (h * grp + j) * HEAD_DIM
                o_ref[0, :, c0:c0 + HEAD_DIM] = o[j * tq:(j + 1) * tq].astype(o_ref.dtype)

    @pl.when(pl.program_id(1) == 0)
    def _():
        attend(n_ctx_tok)

    @pl.when(pl.program_id(1) != 0)
    def _():
        attend(k_ref.shape[2])


def _gqa_attention(q, k, v, n_ctx_tok):
    b, n_q, s, _ = q.shape
    n_kv = k.shape[1]
    tq = ROW_TILE
    return pl.pallas_call(
        functools.partial(_gqa_kernel, n_kv=n_kv, grp=n_q // n_kv, n_ctx_tok=n_ctx_tok),
        out_shape=jax.ShapeDtypeStruct((b, s, n_q * HEAD_DIM), BF16),
        grid=(b, s // tq),
        in_specs=[pl.BlockSpec((1, n_q, tq, HEAD_DIM), lambda i, t: (i, 0, t, 0)),
                  pl.BlockSpec((1, n_kv, s, HEAD_DIM), lambda i, t: (i, 0, 0, 0)),
                  pl.BlockSpec((1, n_kv, s, HEAD_DIM), lambda i, t: (i, 0, 0, 0))],
        out_specs=pl.BlockSpec((1, tq, n_q * HEAD_DIM), lambda i, t: (i, t, 0)),
        compiler_params=_cparams("parallel", "arbitrary"),
        name="gqa_attention",
    )(q, k, v)


def _gelu_tanh(y):
    return 0.5 * y * (1.0 + jnp.tanh(math.sqrt(2.0 / math.pi) * (y + 0.044715 * (y * y * y))))


def _mix_even_kernel(*refs, n_src):
    src, (y_ref, att_ref, modx_ref, modc_ref, wglu_ref, wa_ref, wb_ref, o_ref) = refs[:n_src], refs[n_src:]
    bb, tm, d = src[0].shape
    is_ctx = pl.program_id(1) == 0
    mod = _pick_mod(modx_ref, modc_ref, is_ctx)
    y = y_ref[...].reshape(bb * tm, -1)
    g = _gelu_tanh(y)
    a = g * jax.nn.sigmoid(_bdot(g, wglu_ref[...]))
    ox = _bdot(a, wa_ref[...]) + jnp.dot(att_ref[...].reshape(bb * tm, -1), wb_ref[...], preferred_element_type=F32)
    o_ref[...] = _stream_tile(src, is_ctx) + mod[:, 2:3, :] * ox.reshape(bb, tm, d)


def _mix_even(srcs, y_s5, att, modx, modc, w_glu, w_out, bb=2):
    b, s, w5 = y_s5.shape
    d = srcs[0].shape[2]
    wa = att.shape[2]
    tm = ROW_TILE
    return pl.pallas_call(
        functools.partial(_mix_even_kernel, n_src=len(srcs)),
        out_shape=jax.ShapeDtypeStruct((b, s, d), F32),
        grid=(b // bb, s // tm),
        in_specs=_stream_specs(srcs, bb) + [
                  pl.BlockSpec((bb, tm, w5), lambda i, t: (i, t, 0)),
                  pl.BlockSpec((bb, tm, wa), lambda i, t: (i, t, 0)),
                  pl.BlockSpec((bb, 6, d), lambda i, t: (i, 0, 0)),
                  pl.BlockSpec((1, 6, d), lambda i, t: (0, 0, 0)),
                  pl.BlockSpec((w5, w5), lambda i, t: (0, 0)),
                  pl.BlockSpec((w5, d), lambda i, t: (0, 0)),
                  pl.BlockSpec((wa, d), lambda i, t: (1, 0))],
        out_specs=pl.BlockSpec((bb, tm, d), lambda i, t: (i, t, 0)),
        compiler_params=_cparams("parallel", "arbitrary"),
        name="mix_even",
    )(*srcs, y_s5, att, modx, modc, w_glu, w_out, w_out)


def _ffn_kernel(x_ref, modx_ref, modc_ref, g_ref, w1_ref, w3_ref, w2_ref, o_ref, h_scr, acc_scr):
    bb, tm, d = x_ref.shape
    j = pl.program_id(2)
    mod = _pick_mod(modx_ref, modc_ref, pl.program_id(1) == 0)

    @pl.when(j == 0)
    def _():
        h = _norm_mod(x_ref[...], g_ref[...], mod[:, 3:4, :], mod[:, 4:5, :])
        h_scr[...] = h.reshape(bb * tm, d).astype(BF16)
        acc_scr[...] = jnp.zeros_like(acc_scr)

    h = h_scr[...]
    a = jnp.dot(h, w1_ref[...], preferred_element_type=F32)
    g = jnp.dot(h, w3_ref[...], preferred_element_type=F32)
    acc_scr[...] += _bdot(_silu(a) * g, w2_ref[...])

    @pl.when(j == pl.num_programs(2) - 1)
    def _():
        o_ref[...] = x_ref[...] + mod[:, 5:6, :] * acc_scr[...].reshape(bb, tm, d)


def _ffn(stream, modx, modc, g, w1, w3, w2, bb=4, tf=512):
    b, s, d = stream.shape
    f = w1.shape[1]
    tm = ROW_TILE
    return pl.pallas_call(
        _ffn_kernel,
        out_shape=jax.ShapeDtypeStruct((b, s, d), F32),
        grid=(b // bb, s // tm, f // tf),
        in_specs=[pl.BlockSpec((bb, tm, d), lambda i, t, j: (i, t, 0)),
                  pl.BlockSpec((bb, 6, d), lambda i, t, j: (i, 0, 0)),
                  pl.BlockSpec((1, 6, d), lambda i, t, j: (0, 0, 0)),
                  pl.BlockSpec((1, d), lambda i, t, j: (0, 0)),
                  pl.BlockSpec((d, tf), lambda i, t, j: (0, j)),
                  pl.BlockSpec((d, tf), lambda i, t, j: (0, j)),
                  pl.BlockSpec((tf, d), lambda i, t, j: (j, 0))],
        out_specs=pl.BlockSpec((bb, tm, d), lambda i, t, j: (i, t, 0)),
        scratch_shapes=[pltpu.VMEM((bb * tm, d), BF16), pltpu.VMEM((bb * tm, d), F32)],
        compiler_params=_cparams("parallel", "arbitrary", "arbitrary"),
        name="ffn_dense",
    )(stream, modx, modc, g.reshape(1, d), w1, w3, w2)


def _even_layer(srcs, modx, modc, n_ctx_tok, norm1_g, norm2_g, w_in, w_out, s5_params, w_glu, q_g, k_g,
                ffn_w1, ffn_w3, ffn_w2):
    s5_w = w_glu.shape[0]
    n_q = s5_w // HEAD_DIM
    n_kv = n_q // 2
    n_t = sum(a.shape[1] for a in srcs) // ROW_TILE
    proj = _in_proj(srcs, modx, modc, norm1_g, w_in.astype(BF16), t_off=0, n_t=n_t, ctx_rule=_tile0_is_ctx)
    y_s5 = _s5_scan(proj[:, :, :s5_w], _s5_tables(*s5_params), n_ctx_tok)
    q, k, v = _qk_prep(proj, q_g, k_g, n_ctx_tok, s5_w=s5_w, n_q=n_q, n_kv=n_kv)
    att = _gqa_attention(q, k, v, n_ctx_tok)
    stream = _mix_even(srcs, y_s5, att, modx, modc, w_glu.astype(BF16), w_out.astype(BF16))
    return _ffn(stream, modx, modc, norm2_g, ffn_w1.astype(BF16), ffn_w3.astype(BF16), ffn_w2.astype(BF16))


HYENA_BANDS = 16
CONV_TILE = 256


def _hy_filter_kernel(z_ref, w1_ref, b1_ref, w2_ref, b2_ref, w3_ref, freq_ref, decay_ref, t_ref, h_ref):
    f = freq_ref[...]
    h = jnp.sin(f * (jnp.dot(z_ref[...], w1_ref[...], preferred_element_type=F32, precision=HIGHEST) + b1_ref[...]))
    h = jnp.sin(f * (jnp.dot(h, w2_ref[...], preferred_element_type=F32, precision=HIGHEST) + b2_ref[...]))
    h = jnp.dot(h, w3_ref[...], preferred_element_type=F32, precision=HIGHEST)
    h = h * jnp.exp(-t_ref[...] * jnp.abs(decay_ref[...]))
    h_ref[...] = h / (jnp.sum(jnp.abs(h), axis=0, keepdims=True) + EPS)


def _hy_filters(seq, f_w1, f_b1, f_w2, f_b2, f_w3, f_freq, f_decay):
    k = np.arange(seq, dtype=np.float32)
    t = k / max(seq - 1, 1)
    bands = np.linspace(1e-4, HYENA_BANDS - 1, HYENA_BANDS, dtype=np.float32)
    ang = jnp.asarray(np.float32(2.0 * math.pi / seq) * k[:, None] * bands[None, :])
    pos_dim, hidden = f_w1.shape
    zpad = 128
    z = jnp.concatenate([jnp.asarray(t)[:, None], jnp.cos(ang), -jnp.sin(ang),
                         jnp.zeros((seq, zpad - pos_dim), F32)], axis=-1)
    w1 = jnp.concatenate([f_w1.astype(F32), jnp.zeros((zpad - pos_dim, hidden), F32)], axis=0)
    n = f_w3.shape[1]
    tc = 512
    return pl.pallas_call(
        _hy_filter_kernel,
        out_shape=jax.ShapeDtypeStruct((seq, n), F32),
        grid=(n // tc,),
        in_specs=[pl.BlockSpec((seq, zpad), lambda j: (0, 0)),
                  pl.BlockSpec((zpad, hidden), lambda j: (0, 0)),
                  pl.BlockSpec((1, hidden), lambda j: (0, 0)),
                  pl.BlockSpec((hidden, hidden), lambda j: (0, 0)),
                  pl.BlockSpec((1, hidden), lambda j: (0, 0)),
                  pl.BlockSpec((hidden, tc), lambda j: (0, j)),
                  pl.BlockSpec((1, hidden), lambda j: (0, 0)),
                  pl.BlockSpec((1, tc), lambda j: (0, j)),
                  pl.BlockSpec((seq, 1), lambda j: (0, 0))],
        out_specs=pl.BlockSpec((seq, tc), lambda j: (0, j)),
        compiler_params=_cparams("arbitrary"),
        name="hyena_filter",
    )(z, w1, f_b1.reshape(1, hidden).astype(F32), f_w2.astype(F32), f_b2.reshape(1, hidden).astype(F32),
      f_w3.astype(F32), f_freq.reshape(1, hidden).astype(F32), f_decay.reshape(1, n).astype(F32),
      jnp.asarray(t)[:, None])


def _dft_tables(seq):
    idx = np.arange(seq, dtype=np.int64)
    m = jnp.asarray(((idx[:, None] * idx[None, :]) % (2 * seq)).astype(np.int32))
    ang = m.astype(F32) * np.float32(math.pi / seq)
    alt = jnp.asarray((1.0 - 2.0 * (idx % 2)).astype(np.float32))[:, None]
    return jnp.cos(ang).astype(BF16), jnp.sin(ang).astype(BF16), alt


def _split_bf16(a):
    hi = a.astype(BF16)
    return hi, (a - hi.astype(F32)).astype(BF16)


def _hy_spec_kernel(hf_ref, hb_ref, cos_ref, sin_ref, alt_ref, wk_ref, sre_ref, sim_ref, sny_ref, *, seq):
    hs = hf_ref[...] + hb_ref[...]
    hd = hb_ref[...] - hf_ref[...]
    s_hi, s_lo = _split_bf16(hs)
    d_hi, d_lo = _split_bf16(hd)
    c = cos_ref[...]
    s = sin_ref[...]
    wk = wk_ref[...]
    sre_ref[...] = wk * (jnp.dot(c, s_hi, preferred_element_type=F32) + jnp.dot(c, s_lo, preferred_element_type=F32))
    sim_ref[...] = wk * (jnp.dot(s, d_hi, preferred_element_type=F32) + jnp.dot(s, d_lo, preferred_element_type=F32))
    sny_ref[...] = jnp.sum(alt_ref[...] * hs, axis=0, keepdims=True) * (0.5 / seq)


def _hy_spectrum(h, cos, sin, alt):
    seq, n = h.shape
    half = n // 2
    tc = 128
    nb = half // tc
    wk = jnp.full((seq, 1), 1.0 / seq, F32).at[0, 0].set(0.5 / seq)
    return pl.pallas_call(
        functools.partial(_hy_spec_kernel, seq=seq),
        out_shape=(jax.ShapeDtypeStruct((seq, half), F32), jax.ShapeDtypeStruct((seq, half), F32),
                   jax.ShapeDtypeStruct((1, half), F32)),
        grid=(nb,),
        in_specs=[pl.BlockSpec((seq, tc), lambda j: (0, j)),
                  pl.BlockSpec((seq, tc), lambda j: (0, j + nb)),
                  pl.BlockSpec((seq, seq), lambda j: (0, 0), pipeline_mode=pl.Buffered(1)),
                  pl.BlockSpec((seq, seq), lambda j: (0, 0), pipeline_mode=pl.Buffered(1)),
                  pl.BlockSpec((seq, 1), lambda j: (0, 0)),
                  pl.BlockSpec((seq, 1), lambda j: (0, 0))],
        out_specs=(pl.BlockSpec((seq, tc), lambda j: (0, j)), pl.BlockSpec((seq, tc), lambda j: (0, j)),
                   pl.BlockSpec((1, tc), lambda j: (0, j))),
        compiler_params=_cparams("arbitrary"),
        name="hyena_spectrum",
    )(h, h, cos, sin, alt, wk)


def _hy_conv_kernel(z_ref, g_ref, cwz_ref, cbz_ref, cwg_ref, cbg_ref, cos_ref, sin_ref, alt_ref,
                    sre_ref, sim_ref, sny_ref, d_ref, o_ref, z_scr, zb_scr, yre_scr, yim_scr, *, conv_z, blk):
    seq = z_ref.shape[1]
    tc = z_ref.shape[2]
    row = lax.broadcasted_iota(jnp.int32, (blk, 1), 0)

    def short_conv(x_ref, w_ref, b_ref, l0):
        x = x_ref[0, l0:l0 + blk, :]
        w = w_ref[...]
        top = x_ref[0, l0 - 1:l0, :] if l0 > 0 else jnp.zeros((1, tc), F32)
        bot = x_ref[0, l0 + blk:l0 + blk + 1, :] if l0 + blk < seq else jnp.zeros((1, tc), F32)
        prev = jnp.where(row == 0, top, pltpu.roll(x, 1, 0))
        nxt = jnp.where(row == blk - 1, bot, pltpu.roll(x, blk - 1, 0))
        return prev * w[0:1] + x * w[1:2] + nxt * w[2:3] + b_ref[...]

    zny = jnp.zeros((1, tc), F32)
    for l0 in range(0, seq, blk):
        z = short_conv(z_ref, cwz_ref, cbz_ref, l0) if conv_z else z_ref[0, l0:l0 + blk, :]
        z_scr[l0:l0 + blk, :] = z
        zb_scr[l0:l0 + blk, :] = z.astype(BF16)
        zny = zny + jnp.sum(alt_ref[l0:l0 + blk, :] * z, axis=0, keepdims=True)
    zny = zny * sny_ref[...]
    for k0 in range(0, seq, blk):
        zc = jnp.dot(cos_ref[k0:k0 + blk, :], zb_scr[...], preferred_element_type=F32)
        zs = jnp.dot(sin_ref[k0:k0 + blk, :], zb_scr[...], preferred_element_type=F32)
        sre = sre_ref[k0:k0 + blk, :]
        sim = sim_ref[k0:k0 + blk, :]
        yre_scr[k0:k0 + blk, :] = (zc * sre + zs * sim).astype(BF16)
        yim_scr[k0:k0 + blk, :] = (zc * sim - zs * sre).astype(BF16)
    d = d_ref[0]
    for l0 in range(0, seq, blk):
        y = (jnp.dot(cos_ref[l0:l0 + blk, :], yre_scr[...], preferred_element_type=F32)
             - jnp.dot(sin_ref[l0:l0 + blk, :], yim_scr[...], preferred_element_type=F32)
             + alt_ref[l0:l0 + blk, :] * zny)
        gate = short_conv(g_ref, cwg_ref, cbg_ref, l0)
        o_ref[0, l0:l0 + blk, :] = gate * (y + d * z_scr[l0:l0 + blk, :])


def _hy_order(z_src, z_off, g_src, g_off, conv_w, conv_b, cos, sin, alt, sre, sim, sny, hy_d, order, width, conv_z):
    b, seq, _ = z_src.shape
    tc = CONV_TILE
    nb = width // tc
    zo, go = z_off // tc, g_off // tc

    def col(off):
        return lambda j, i: (0, j + off)

    return pl.pallas_call(
        functools.partial(_hy_conv_kernel, conv_z=conv_z, blk=512),
        out_shape=jax.ShapeDtypeStruct((b, seq, width), F32),
        grid=(nb, b),
        in_specs=[pl.BlockSpec((1, seq, tc), lambda j, i: (i, 0, j + zo)),
                  pl.BlockSpec((1, seq, tc), lambda j, i: (i, 0, j + go)),
                  pl.BlockSpec((3, tc), col(zo if conv_z else 0)), pl.BlockSpec((1, tc), col(zo if conv_z else 0)),
                  pl.BlockSpec((3, tc), col(go)), pl.BlockSpec((1, tc), col(go)),
                  pl.BlockSpec((seq, seq), lambda j, i: (0, 0)),
                  pl.BlockSpec((seq, seq), lambda j, i: (0, 0)),
                  pl.BlockSpec((seq, 1), lambda j, i: (0, 0)),
                  pl.BlockSpec((seq, tc), col(order * nb)),
                  pl.BlockSpec((seq, tc), col(order * nb)),
                  pl.BlockSpec((1, tc), col(order * nb)),
                  pl.BlockSpec((1, 1, tc), lambda j, i: (order, 0, j))],
        out_specs=pl.BlockSpec((1, seq, tc), lambda j, i: (i, 0, j)),
        scratch_shapes=[pltpu.VMEM((seq, tc), F32), pltpu.VMEM((seq, tc), BF16),
                        pltpu.VMEM((seq, tc), BF16), pltpu.VMEM((seq, tc), BF16)],
        compiler_params=_cparams("arbitrary", "arbitrary"),
        name="hyena_conv",
    )(z_src, g_src, conv_w, conv_b, conv_w, conv_b, cos, sin, alt, sre, sim, sny, hy_d.reshape(2, 1, width))


def _hy_conv(proj, conv_w, conv_b, cos, sin, alt, sre, sim, sny, hy_d, width):
    args = (conv_w, conv_b, cos, sin, alt, sre, sim, sny, hy_d)
    z1 = _hy_order(proj, 0, proj, width, *args, order=0, width=width, conv_z=True)
    return _hy_order(z1, 0, proj, 2 * width, *args, order=1, width=width, conv_z=False)


NA_WIN_ROWS = 8
NA_WIN_COLS = 16
NA_QROWS = ROW_TILE // GRID_W
NA_KTILES = 3


def _na_bias_tiles(rpb, rows):
    nqb = rows // NA_QROWS
    assert nqb >= 3 and rows >= NA_WIN_ROWS + NA_QROWS
    col = np.arange(GRID_W)
    col_start = np.clip(col - NA_WIN_COLS // 2, 0, GRID_W - NA_WIN_COLS)
    col_ok = (col[None, :] >= col_start[:, None]) & (col[None, :] < col_start[:, None] + NA_WIN_COLS)
    dc_idx = np.clip(col[None, :] - col[:, None] + NA_WIN_COLS - 1, 0, 2 * NA_WIN_COLS - 2)
    tiles = []
    for j in (0, 1, nqb - 1):
        kb0 = min(max(j - 1, 0), nqb - NA_KTILES)
        q_r = j * NA_QROWS + np.arange(NA_QROWS)
        k_r = kb0 * NA_QROWS + np.arange(NA_KTILES * NA_QROWS)
        r0 = np.clip(q_r - NA_WIN_ROWS // 2, 0, rows - NA_WIN_ROWS)
        row_ok = (k_r[None, :] >= r0[:, None]) & (k_r[None, :] < r0[:, None] + NA_WIN_ROWS)
        dr_idx = np.clip(k_r[None, :] - q_r[:, None] + NA_WIN_ROWS - 1, 0, 2 * NA_WIN_ROWS - 2)
        ok = row_ok[:, None, :, None] & col_ok[None, :, None, :]
        pick_r = jnp.asarray(dr_idx[:, :, None] == np.arange(2 * NA_WIN_ROWS - 1), F32)
        pick_c = jnp.asarray(dc_idx[:, :, None] == np.arange(2 * NA_WIN_COLS - 1), F32)
        by_row = jnp.einsum('hij,abi->habj', rpb.astype(F32), pick_r, precision=HIGHEST)
        bias = jnp.einsum('habj,cdj->hacbd', by_row, pick_c, precision=HIGHEST)
        tile = jnp.where(jnp.asarray(ok)[None], bias, -jnp.inf)
        tiles.append(tile.reshape(rpb.shape[0], ROW_TILE, NA_KTILES * ROW_TILE))
    return jnp.stack(tiles, axis=1)


def _na_kernel(q_ref, k0_ref, k1_ref, k2_ref, v0_ref, v1_ref, v2_ref, kc_ref, vc_ref, bias_ref, o_ref, *, n_heads):
    nt = (((1,), (1,)), ((), ()))
    q = (q_ref[0] * (HEAD_DIM ** -0.5)).astype(BF16)
    k = jnp.concatenate([k0_ref[0], k1_ref[0], k2_ref[0]], axis=0).astype(BF16)
    v = jnp.concatenate([v0_ref[0], v1_ref[0], v2_ref[0]], axis=0).astype(BF16)
    kc = kc_ref[0].astype(BF16)
    vc = vc_ref[0].astype(BF16)
    for h in range(n_heads):
        sl = slice(h * HEAD_DIM, (h + 1) * HEAD_DIM)
        s_loc = lax.dot_general(q[:, sl], k[:, sl], nt, preferred_element_type=F32) + bias_ref[h, 0]
        s_ctx = lax.dot_general(q[:, sl], kc[:, sl], nt, preferred_element_type=F32)
        m = jnp.maximum(jnp.max(s_loc, axis=-1, keepdims=True), jnp.max(s_ctx, axis=-1, keepdims=True))
        p_loc = jnp.exp(s_loc - m)
        p_ctx = jnp.exp(s_ctx - m)
        l = jnp.sum(p_loc, axis=-1, keepdims=True) + jnp.sum(p_ctx, axis=-1, keepdims=True)
        o = (jnp.dot(p_loc.astype(BF16), v[:, sl], preferred_element_type=F32)
             + jnp.dot(p_ctx.astype(BF16), vc[:, sl], preferred_element_type=F32)) / l
        o_ref[0, :, sl] = o.astype(o_ref.dtype)


def _na_attention(proj, proj_c, rpb, *, q_off, n_heads):
    b, seq, _ = proj.shape
    w = n_heads * HEAD_DIM
    tm = ROW_TILE
    nqb = seq // tm
    bias = _na_bias_tiles(rpb, seq // GRID_W)
    qc = q_off // w

    def kb0(j):
        return jnp.clip(j - 1, 0, nqb - NA_KTILES)

    def kv_spec(cb, off):
        return pl.BlockSpec((1, tm, w), lambda i, j: (i, kb0(j) + off, cb))

    def bias_type(j):
        return jnp.where(j == 0, 0, jnp.where(j == nqb - 1, 2, 1))

    return pl.pallas_call(
        functools.partial(_na_kernel, n_heads=n_heads),
        out_shape=jax.ShapeDtypeStruct((b, seq, w), BF16),
        grid=(b, nqb),
        in_specs=[pl.BlockSpec((1, tm, w), lambda i, j: (i, j, qc)),
                  kv_spec(qc + 1, 0), kv_spec(qc + 1, 1), kv_spec(qc + 1, 2),
                  kv_spec(qc + 2, 0), kv_spec(qc + 2, 1), kv_spec(qc + 2, 2),
                  pl.BlockSpec((1, proj_c.shape[1], w), lambda i, j: (i, 0, 0)),
                  pl.BlockSpec((1, proj_c.shape[1], w), lambda i, j: (i, 0, 1)),
                  pl.BlockSpec((n_heads, 1, tm, NA_KTILES * tm), lambda i, j: (0, bias_type(j), 0, 0))],
        out_specs=pl.BlockSpec((1, tm, w), lambda i, j: (i, j, 0)),
        compiler_params=_cparams("parallel", "arbitrary"),
        name="na_attention",
    )(proj, proj, proj, proj, proj, proj, proj, proj_c, proj_c, bias)


MOE_TOP_K = 2


def _mix_odd_kernel(x_ref, hy_ref, na_ref, modx_ref, g_ref, wa_ref, wb_ref, rt_ref, x1_ref, h2_ref, idx_ref, wt_ref):
    bb, tm, d = x_ref.shape
    mod = modx_ref[...]
    ox = (_bdot(hy_ref[...].reshape(bb * tm, -1), wa_ref[...])
          + jnp.dot(na_ref[...].reshape(bb * tm, -1), wb_ref[...], preferred_element_type=F32))
    x1 = x_ref[...] + mod[:, 2:3, :] * ox.reshape(bb, tm, d)
    x1_ref[...] = x1
    h2 = _norm_mod(x1, g_ref[...], mod[:, 3:4, :], mod[:, 4:5, :])
    h2_ref[...] = h2
    lg = lax.dot_general(rt_ref[...], h2.reshape(bb * tm, d), (((1,), (1,)), ((), ())),
                         preferred_element_type=F32, precision=HIGHEST)
    n_e = lg.shape[0]
    eid = lax.broadcasted_iota(jnp.int32, lg.shape, 0)
    m1 = jnp.max(lg, axis=0, keepdims=True)
    i1 = jnp.min(jnp.where(lg == m1, eid, n_e), axis=0, keepdims=True)
    lg2 = jnp.where(eid == i1, -jnp.inf, lg)
    m2 = jnp.max(lg2, axis=0, keepdims=True)
    i2 = jnp.min(jnp.where(lg2 == m2, eid, n_e), axis=0, keepdims=True)
    e2 = jnp.exp(m2 - m1)
    den = 1.0 + e2
    idx_ref[...] = jnp.concatenate([i1, i2], axis=0)
    wt_ref[...] = jnp.concatenate([1.0 / den, e2 / den], axis=0)


def _mix_odd(stream, t_off, o_hy, o_na, modx, g, w_out, router):
    bb = 1
    b, _, d = stream.shape
    seq = o_hy.shape[1]
    wh = o_hy.shape[2]
    wn = o_na.shape[2]
    n_e = router.shape[1]
    tm = ROW_TILE
    n_t = seq // tm
    rows = bb * tm
    return pl.pallas_call(
        _mix_odd_kernel,
        out_shape=(jax.ShapeDtypeStruct((b, seq, d), F32), jax.ShapeDtypeStruct((b, seq, d), F32),
                   jax.ShapeDtypeStruct((MOE_TOP_K, b * seq), jnp.int32), jax.ShapeDtypeStruct((MOE_TOP_K, b * seq), F32)),
        grid=(b // bb, n_t),
        in_specs=[pl.BlockSpec((bb, tm, d), lambda i, t: (i, t + t_off, 0)),
                  pl.BlockSpec((bb, tm, wh), lambda i, t: (i, t, 0)),
                  pl.BlockSpec((bb, tm, wn), lambda i, t: (i, t, 0)),
                  pl.BlockSpec((bb, 6, d), lambda i, t: (i, 0, 0)),
                  pl.BlockSpec((1, d), lambda i, t: (0, 0)),
                  pl.BlockSpec((wh, d), lambda i, t: (0, 0)),
                  pl.BlockSpec((wn, d), lambda i, t: (1, 0)),
                  pl.BlockSpec((n_e, d), lambda i, t: (0, 0))],
        out_specs=(pl.BlockSpec((bb, tm, d), lambda i, t: (i, t, 0)),
                   pl.BlockSpec((bb, tm, d), lambda i, t: (i, t, 0)),
                   pl.BlockSpec((MOE_TOP_K, rows), lambda i, t: (0, i * n_t + t)),
                   pl.BlockSpec((MOE_TOP_K, rows), lambda i, t: (0, i * n_t + t))),
        compiler_params=_cparams("arbitrary", "arbitrary"),
        name="mix_odd_router",
    )(stream, o_hy, o_na, modx, g.reshape(1, d), w_out, w_out, router.T.astype(F32))


MOE_BLOCK = 1024
MOE_FF_TILE = 896


def _moe_plan(idx_t, n_experts):
    k, t = idx_t.shape
    e_flat = idx_t.T.reshape(-1)
    onehot = (e_flat[:, None] == jnp.arange(n_experts)[None, :]).astype(jnp.int32)
    rank = jnp.sum((jnp.cumsum(onehot, axis=0) - onehot) * onehot, axis=1)
    counts = jnp.sum(onehot, axis=0)
    padded = (counts + MOE_BLOCK - 1) // MOE_BLOCK * MOE_BLOCK
    pend = jnp.cumsum(padded)
    pstart = pend - padded
    slot = pstart[e_flat] + rank
    n_blocks = (t * k) // MOE_BLOCK + n_experts
    tok = jnp.arange(t * k, dtype=jnp.int32) // k
    slot_tok = jnp.zeros((n_blocks * MOE_BLOCK,), jnp.int32).at[slot].set(tok)
    block_e = jnp.clip(jnp.searchsorted(pend, jnp.arange(n_blocks) * MOE_BLOCK, side='right'), 0, n_experts - 1)
    n_used = (pend[-1] // MOE_BLOCK).astype(jnp.int32).reshape(1)
    return (slot_tok.reshape(n_blocks, 1, MOE_BLOCK), block_e.astype(jnp.int32), n_used,
            slot.reshape(t, k).astype(jnp.int32))


def _moe_ffn_kernel(be_ref, nu_ref, tok_ref, nxt_ref, h_hbm, w1_ref, w3_ref, w2_ref, y_ref, xg_scr, xb_scr, sem,
                    *, n_j):
    i = pl.program_id(0)
    j = pl.program_id(1)
    used = i < nu_ref[0]
    rows = xb_scr.shape[0]
    per_step = rows // n_j
    cur = i % 2
    nxt = 1 - cur

    def row_copy(idx_ref, r, buf):
        return pltpu.make_async_copy(h_hbm.at[pl.ds(idx_ref[0, 0, r], 1)], xg_scr.at[buf, pl.ds(r, 1)], sem.at[buf])

    def wait_rows(idx_ref, buf):
        def wait(r, carry):
            row_copy(idx_ref, r, buf).wait()
            return carry
        lax.fori_loop(0, rows, wait, 0)

    @pl.when(jnp.logical_and(i == 0, j == 0))
    def _():
        def start(r, carry):
            row_copy(tok_ref, r, 0).start()
            return carry
        lax.fori_loop(0, rows, start, 0)

    @pl.when(j == 0)
    def _():
        wait_rows(tok_ref, cur)
        xb_scr[...] = xg_scr[cur].astype(BF16)
        y_ref[...] = jnp.zeros_like(y_ref)

    def prefetch_rows():
        base = j * per_step
        for k in range(per_step):
            row_copy(nxt_ref, base + k, nxt).start(priority=k % 2)

    @pl.when(used)
    def _():
        prefetch_rows()
        xb = xb_scr[...]
        a = jnp.dot(xb, w1_ref[0], preferred_element_type=F32)
        g = jnp.dot(xb, w3_ref[0], preferred_element_type=F32)
        y_ref[...] += _bdot(_silu(a) * g, w2_ref[0])

    @pl.when(jnp.logical_not(used))
    def _():
        prefetch_rows()

    @pl.when(jnp.logical_and(i == pl.num_programs(0) - 1, j == n_j - 1))
    def _():
        wait_rows(nxt_ref, nxt)


def _moe_ffn(h2, slot_tok, block_e, n_used, w1, w3, w2):
    t, d = h2.shape
    n_blocks = slot_tok.shape[0]
    f = w1.shape[2]
    mb = MOE_BLOCK
    tf = MOE_FF_TILE
    n_j = f // tf
    grid_spec = pltpu.PrefetchScalarGridSpec(
        num_scalar_prefetch=2,
        grid=(n_blocks, n_j),
        in_specs=[pl.BlockSpec((1, 1, mb), lambda i, j, be, nu: (i, 0, 0), memory_space=pltpu.SMEM),
                  pl.BlockSpec((1, 1, mb), lambda i, j, be, nu: (jnp.minimum(i + 1, n_blocks - 1), 0, 0),
                               memory_space=pltpu.SMEM),
                  pl.BlockSpec(memory_space=pl.ANY),
                  pl.BlockSpec((1, d, tf), lambda i, j, be, nu: (be[i], 0, j)),
                  pl.BlockSpec((1, d, tf), lambda i, j, be, nu: (be[i], 0, j)),
                  pl.BlockSpec((1, tf, d), lambda i, j, be, nu: (be[i], j, 0))],
        out_specs=pl.BlockSpec((mb, d), lambda i, j, be, nu: (i, 0)),
        scratch_shapes=[pltpu.VMEM((2, mb, d), F32), pltpu.VMEM((mb, d), BF16), pltpu.SemaphoreType.DMA((2,))],
    )
    return pl.pallas_call(
        functools.partial(_moe_ffn_kernel, n_j=n_j),
        out_shape=jax.ShapeDtypeStruct((n_blocks * mb, d), F32),
        grid_spec=grid_spec,
        compiler_params=_cparams("arbitrary", "arbitrary"),
        name="moe_expert_ffn",
    )(block_e, n_used, slot_tok, slot_tok, h2, w1, w3, w2)


def _moe_combine_kernel(slot_ref, nxt_ref, y_hbm, x1_ref, wt_ref, modx_ref, g_ref, o_ref, y_scr, sem):
    tm = x1_ref.shape[1]
    n_t = pl.num_programs(1)
    step = pl.program_id(0) * n_t + pl.program_id(1)
    n_steps = pl.num_programs(0) * n_t
    cur = step % 2
    nxt = 1 - cur

    def row_copy(idx_ref, r, buf):
        return pltpu.make_async_copy(y_hbm.at[pl.ds(idx_ref[0, 0, r], 1)], y_scr.at[buf, pl.ds(r, 1)], sem.at[buf])

    def wait_rows(idx_ref, buf):
        def wait(r, carry):
            row_copy(idx_ref, r, buf).wait()
            return carry
        lax.fori_loop(0, MOE_TOP_K * tm, wait, 0)

    @pl.when(step == 0)
    def _():
        def start(r, carry):
            row_copy(slot_ref, r, 0).start()
            return carry
        lax.fori_loop(0, MOE_TOP_K * tm, start, 0)

    wait_rows(slot_ref, cur)
    for r in range(MOE_TOP_K * tm):
        row_copy(nxt_ref, r, nxt).start(priority=r % 2)
    wt = wt_ref[...]
    y = wt[:, 0:1] * y_scr[cur, 0:tm, :] + wt[:, 1:2] * y_scr[cur, tm:2 * tm, :]
    x2 = x1_ref[0] + modx_ref[0, 5:6, :] * y
    o_ref[0] = (x2 * lax.rsqrt(jnp.mean(x2 * x2, axis=-1, keepdims=True) + EPS)) * g_ref[...]

    @pl.when(step == n_steps - 1)
    def _():
        wait_rows(nxt_ref, nxt)


def _moe_combine(y_slots, slot_of, wt, x1, modx, final_g):
    b, seq, d = x1.shape
    tm = ROW_TILE
    n_t = seq // tm
    n_steps = b * n_t
    slots = slot_of.reshape(n_steps, tm, MOE_TOP_K).transpose(0, 2, 1).reshape(n_steps, 1, MOE_TOP_K * tm)
    return pl.pallas_call(
        _moe_combine_kernel,
        out_shape=jax.ShapeDtypeStruct((b, seq, d), F32),
        grid=(b, n_t),
        in_specs=[pl.BlockSpec((1, 1, MOE_TOP_K * tm), lambda i, t: (i * n_t + t, 0, 0), memory_space=pltpu.SMEM),
                  pl.BlockSpec((1, 1, MOE_TOP_K * tm), lambda i, t: (jnp.minimum(i * n_t + t + 1, n_steps - 1), 0, 0),
                               memory_space=pltpu.SMEM),
                  pl.BlockSpec(memory_space=pl.ANY),
                  pl.BlockSpec((1, tm, d), lambda i, t: (i, t, 0)),
                  pl.BlockSpec((tm, MOE_TOP_K), lambda i, t: (i * n_t + t, 0)),
                  pl.BlockSpec((1, 6, d), lambda i, t: (i, 0, 0)),
                  pl.BlockSpec((1, d), lambda i, t: (0, 0))],
        out_specs=pl.BlockSpec((1, tm, d), lambda i, t: (i, t, 0)),
        scratch_shapes=[pltpu.VMEM((2, MOE_TOP_K * tm, d), F32), pltpu.SemaphoreType.DMA((2,))],
        compiler_params=_cparams("arbitrary", "arbitrary"),
        name="moe_combine_norm",
    )(slots, slots, y_slots, x1, wt, modx, final_g.reshape(1, d))


def _odd_layer(stream, modx, modc, n_ctx_tok, norm1_g, norm2_g, w_in, w_out, conv_w, conv_b, filt, hy_d, rpb,
               router, moe_w1, moe_w3, moe_w2, final_g):
    b, s, d = stream.shape
    seq = s - n_ctx_tok
    hy_w = hy_d.shape[1]
    n_heads = rpb.shape[0]
    na_w = n_heads * HEAD_DIM
    t_off = n_ctx_tok // ROW_TILE
    w_in_b = w_in.astype(BF16)
    proj = _in_proj((stream,), modx, modc, norm1_g, w_in_b, t_off=t_off, n_t=seq // ROW_TILE, ctx_rule=_never_ctx)
    proj_c = _in_proj((stream,), modx, modc, norm1_g, w_in_b[:, 3 * hy_w + na_w:], t_off=0, n_t=t_off,
                      ctx_rule=_always_ctx)
    cos, sin, alt = _dft_tables(seq)
    sre, sim, sny = _hy_spectrum(_hy_filters(seq, *filt), cos, sin, alt)
    o_hy = _hy_conv(proj, conv_w.astype(F32), conv_b.reshape(1, -1).astype(F32), cos, sin, alt, sre, sim, sny,
                    hy_d.astype(F32), hy_w)
    o_na = _na_attention(proj, proj_c, rpb, q_off=3 * hy_w, n_heads=n_heads)
    x1, h2, idx_t, wt_t = _mix_odd(stream, t_off, o_hy, o_na, modx, norm2_g, w_out.astype(BF16), router)
    slot_tok, block_e, n_used, slot_of = _moe_plan(idx_t, router.shape[1])
    y_slots = _moe_ffn(h2.reshape(b * seq, d), slot_tok, block_e, n_used,
                       moe_w1.astype(BF16), moe_w3.astype(BF16), moe_w2.astype(BF16))
    return _moe_combine(y_slots, slot_of, wt_t.T, x1, modx, final_g)


def kernel(x, c, ctx, c_ctx, mod_w, mod_b, norm1_g, norm2_g, ev_w_in, ev_w_out, s5_lam_re, s5_lam_im, s5_log_dt, s5_b_re, s5_b_im, s5_c_re, s5_c_im, s5_d, s5_w_glu, gqa_q_g, gqa_k_g, ffn_w1, ffn_w3, ffn_w2, od_w_in, od_w_out, hy_conv_w, hy_conv_b, hy_w1, hy_b1, hy_w2, hy_b2, hy_w3, hy_freq, hy_decay, hy_d, na_rpb, moe_router, moe_w1, moe_w3, moe_w2, final_g):
    b, seq, d = x.shape
    n_ctx_tok = ctx.shape[1]
    assert n_ctx_tok == ROW_TILE and seq % ROW_TILE == 0
    rows = 8 * ((b + 1 + 7) // 8)
    cvec = jnp.zeros((rows, d), F32).at[:b].set(c).at[b].set(c_ctx)
    m = _modulation(cvec, mod_w, mod_b)
    modx = [m[l, :b].reshape(b, 6, d) for l in range(2)]
    modc = [m[l, b].reshape(1, 6, d) for l in range(2)]
    s5p = (s5_lam_re[0], s5_lam_im[0], s5_log_dt[0], s5_b_re[0], s5_b_im[0], s5_c_re[0], s5_c_im[0], s5_d[0])
    stream = _even_layer((ctx, x), modx[0], modc[0], n_ctx_tok, norm1_g[0], norm2_g[0], ev_w_in[0], ev_w_out[0], s5p,
                         s5_w_glu[0], gqa_q_g[0], gqa_k_g[0], ffn_w1[0], ffn_w3[0], ffn_w2[0])
    filt = (hy_w1[0], hy_b1[0], hy_w2[0], hy_b2[0], hy_w3[0], hy_freq[0], hy_decay[0])
    return _odd_layer(stream, modx[1], modc[1], n_ctx_tok, norm1_g[1], norm2_g[1], od_w_in[0], od_w_out[0],
                      hy_conv_w[0], hy_conv_b[0], filt, hy_d[0], na_rpb[0], moe_router[0],
                      moe_w1[0], moe_w3[0], moe_w2[0], final_g)
```

```python
import functools
import math

import jax
import jax.numpy as jnp
import numpy as np
from jax import lax
from jax.experimental import pallas as pl
from jax.experimental.pallas import tpu as pltpu

F32 = jnp.float32
BF16 = jnp.bfloat16
EPS = 1e-6
HEAD_DIM = 64
GRID_W = 64
ROPE_FREQS = HEAD_DIM // 4
ROPE_BASE = 10000.0
ROW_TILE = 256
S5_GROUP_CH = 16
S5_STATE = 64
S5_CHUNK = 16
VMEM_LIMIT = 56 * 1024 * 1024
HIGHEST = lax.Precision.HIGHEST


def _cparams(*sem):
    return pltpu.CompilerParams(dimension_semantics=sem, vmem_limit_bytes=VMEM_LIMIT)


def _bdot(a, b):
    return jnp.dot(a.astype(BF16), b.astype(BF16), preferred_element_type=F32)


def _silu(a):
    return a * jax.nn.sigmoid(a)


def _norm_mod(x, g, shift, scale):
    y = x * lax.rsqrt(jnp.mean(x * x, axis=-1, keepdims=True) + EPS)
    return (y * g) * (1.0 + scale) + shift


SUBLANES = 8
LANES = 128


def _tile_shape(d):
    assert d % (SUBLANES * LANES) == 0
    return (d // LANES, LANES)


def _rows_to_tiles(a):
    return a.reshape(a.shape[0], a.shape[1] // LANES, LANES)


def _tiles_to_rows(a):
    return a.reshape(a.shape[0], a.shape[1] * a.shape[2])


def _pick_mod(modx_ref, modc_ref, is_ctx):
    return jnp.where(is_ctx, modc_ref[...], modx_ref[...])


def _mod_kernel(c_ref, w_ref, b_ref, o_ref):
    s = _silu(c_ref[...])
    o_ref[0] = jnp.dot(s, w_ref[0], preferred_element_type=F32, precision=HIGHEST) + b_ref[0]


def _modulation(cvec, mod_w, mod_b):
    depth, d, n = mod_w.shape
    rows = cvec.shape[0]
    tn = 1024
    return pl.pallas_call(
        _mod_kernel,
        out_shape=jax.ShapeDtypeStruct((depth, rows, n), F32),
        grid=(depth, n // tn),
        in_specs=[pl.BlockSpec((rows, d), lambda l, j: (0, 0)),
                  pl.BlockSpec((1, d, tn), lambda l, j: (l, 0, j)),
                  pl.BlockSpec((1, 1, tn), lambda l, j: (l, 0, j))],
        out_specs=pl.BlockSpec((1, rows, tn), lambda l, j: (l, 0, j)),
        compiler_params=_cparams("arbitrary", "arbitrary"),
        name="adaln_mod",
    )(cvec, mod_w, mod_b.reshape(depth, 1, n))


def _stream_specs(srcs, bb, t_off=0):
    d = srcs[0].shape[2]
    if len(srcs) == 1:
        return [pl.BlockSpec((bb, ROW_TILE, d), lambda i, t, *_: (i, t + t_off, 0))]
    return [pl.BlockSpec((bb, ROW_TILE, d), lambda i, t, *_: (i, 0, 0)),
            pl.BlockSpec((bb, ROW_TILE, d), lambda i, t, *_: (i, jnp.maximum(t - 1, 0), 0))]


def _stream_tile(refs, is_ctx):
    if len(refs) == 1:
        return refs[0][...]
    return jnp.where(is_ctx, refs[0][...], refs[1][...])


def _in_kernel(*refs, ctx_rule, n_src, cuts):
    src, (modx_ref, modc_ref, g_ref, w_ref), o_refs = refs[:n_src], refs[n_src:n_src + 4], refs[n_src + 4:]
    bb, tm, d = src[0].shape
    is_ctx = ctx_rule(pl.program_id(1))
    mod = _pick_mod(modx_ref, modc_ref, is_ctx)
    h = _norm_mod(_stream_tile(src, is_ctx), g_ref[...], mod[:, 0:1, :], mod[:, 1:2, :])
    o = _bdot(h.reshape(bb * tm, d), w_ref[...])
    for o_ref, (c0, c1) in zip(o_refs, cuts):
        o_ref[...] = o[:, c0:c1].reshape(bb, tm, c1 - c0).astype(o_ref.dtype)


def _in_proj(srcs, modx, modc, g, w, *, t_off, n_t, ctx_rule, outs=None, bb=2):
    b, _, d = srcs[0].shape
    n = w.shape[1]
    outs = outs or [(0, n, F32)]
    res = pl.pallas_call(
        functools.partial(_in_kernel, ctx_rule=ctx_rule, n_src=len(srcs), cuts=[(c0, c1) for c0, c1, _ in outs]),
        out_shape=[jax.ShapeDtypeStruct((b, n_t * ROW_TILE, c1 - c0), dt) for c0, c1, dt in outs],
        grid=(b // bb, n_t),
        in_specs=_stream_specs(srcs, bb, t_off) + [
            pl.BlockSpec((bb, 6, d), lambda i, t: (i, 0, 0)),
            pl.BlockSpec((1, 6, d), lambda i, t: (0, 0, 0)),
            pl.BlockSpec((1, d), lambda i, t: (0, 0)),
            pl.BlockSpec((d, n), lambda i, t: (0, 0))],
        out_specs=[pl.BlockSpec((bb, ROW_TILE, c1 - c0), lambda i, t: (i, t, 0)) for c0, c1, _ in outs],
        compiler_params=_cparams("parallel", "arbitrary"),
        name="in_proj",
    )(*srcs, modx, modc, g.reshape(1, d), w)
    return res if len(res) > 1 else res[0]


def _tile0_is_ctx(t):
    return t == 0


def _never_ctx(t):
    return t < 0


def _always_ctx(t):
    return t >= 0


def _s5_tables(lam_re, lam_im, log_dt, b_re, b_im, c_re, c_im, d_skip):
    t_len, n_st, p_ch = S5_CHUNK, S5_STATE, S5_GROUP_CH
    groups = lam_re.shape[1]
    dt = jnp.exp(log_dt.astype(F32))[..., None]
    lr = lam_re.astype(F32)
    li = lam_im.astype(F32)
    mag = jnp.exp(lr * dt)
    ar = mag * jnp.cos(li * dt)
    ai = mag * jnp.sin(li * dt)
    nr = ar - 1.0
    den = lr * lr + li * li
    kr = (nr * lr + ai * li) / den
    ki = (ai * lr - nr * li) / den
    br = b_re.astype(F32)
    bi = b_im.astype(F32)
    bbr = kr[..., None] * br - ki[..., None] * bi
    bbi = kr[..., None] * bi + ki[..., None] * br
    cr = c_re.astype(F32)
    ci = c_im.astype(F32)
    pr = [jnp.ones_like(ar)]
    pi = [jnp.zeros_like(ai)]
    for _ in range(t_len):
        pr.append(pr[-1] * ar - pi[-1] * ai)
        pi.append(pr[-2] * ai + pi[-1] * ar)
    pr = jnp.stack(pr)
    pi = jnp.stack(pi)
    er = cr[None] * pr[:, :, :, None, :] - ci[None] * pi[:, :, :, None, :]
    ei = cr[None] * pi[:, :, :, None, :] + ci[None] * pr[:, :, :, None, :]
    kern = (jnp.einsum('jdgqn,dgnp->jdgqp', er, bbr, precision=HIGHEST)
            - jnp.einsum('jdgqn,dgnp->jdgqp', ei, bbi, precision=HIGHEST))
    s_idx = np.arange(t_len)[:, None]
    t_idx = np.arange(t_len)[None, :]
    lag_f = np.clip(t_idx - s_idx, 0, t_len - 1)
    lag_b = np.clip(s_idx - t_idx, 0, t_len - 1)
    kf = kern[:, 0][lag_f]
    kb = kern[:, 1][lag_b]
    mask_f = jnp.asarray(s_idx <= t_idx, F32)[:, :, None, None, None]
    mask_b = jnp.asarray(s_idx >= t_idx, F32)[:, :, None, None, None]
    dmat = jnp.eye(p_ch, dtype=F32)[None] * d_skip.astype(F32).reshape(groups, 1, p_ch)
    eye_t = jnp.asarray(s_idx == t_idx, F32)[:, :, None, None, None]
    full = kf * mask_f + kb * mask_b + eye_t * dmat[None, None]
    toep = full.transpose(2, 0, 4, 1, 3).reshape(groups, t_len * p_ch, t_len * p_ch)
    def drive(pw_r, pw_i, d):
        re = pw_r[..., None] * bbr[d][None] - pw_i[..., None] * bbi[d][None]
        im = pw_r[..., None] * bbi[d][None] + pw_i[..., None] * bbr[d][None]
        return re.transpose(1, 0, 3, 2), im.transpose(1, 0, 3, 2)
    f_re, f_im = drive(pr[:t_len, 0][::-1], pi[:t_len, 0][::-1], 0)
    b_re2, b_im2 = drive(pr[:t_len, 1], pi[:t_len, 1], 1)
    wst = jnp.concatenate([f_re, f_im, b_re2, b_im2], axis=-1).reshape(groups, t_len * p_ch, 4 * n_st)
    def read(e_r, e_i):
        return e_r.transpose(1, 3, 0, 2), (-e_i).transpose(1, 3, 0, 2)
    of_re, of_im = read(er[1:, 0], ei[1:, 0])
    ob_re, ob_im = read(er[1:, 1][::-1], ei[1:, 1][::-1])
    wout = jnp.concatenate([of_re, of_im, ob_re, ob_im], axis=1).reshape(groups, 4 * n_st, t_len * p_ch)
    a_r = pr[t_len]
    a_i = pi[t_len]
    adec = jnp.stack([jnp.concatenate([a_r[0], a_r[0]], -1), jnp.concatenate([-a_i[0], a_i[0]], -1),
                      jnp.concatenate([a_r[1], a_r[1]], -1), jnp.concatenate([-a_i[1], a_i[1]], -1)], axis=1)
    return toep.astype(BF16), wst.astype(BF16), wout.astype(BF16), adec


def _s5_kernel(u_ref, toep_ref, wst_ref, wout_ref, a_ref, y_ref, s_scr, h_scr, y_scr, *, nb, n_ctx, n_chunks,
               rows_blk):
    rows = u_ref.shape[1]
    n2 = 2 * S5_STATE
    toep = toep_ref[0]
    wst = wst_ref[0]
    for r0 in range(0, rows, rows_blk):
        u = u_ref[0, r0:r0 + rows_blk, :].astype(BF16)
        y_scr[r0:r0 + rows_blk, :] = jnp.dot(u, toep, preferred_element_type=F32)
        s_scr[r0:r0 + rows_blk, :] = jnp.dot(u, wst, preferred_element_type=F32)
    a = a_ref[0]
    af1, af2, ab1, ab2 = a[0:1], a[1:2], a[2:3], a[3:4]

    def step(i, carry):
        hf, hb = carry
        cb = jnp.where(i < n_ctx, n_ctx - 1 - i, n_chunks - 1 - (i - n_ctx))
        rf = pl.multiple_of(i * nb, nb)
        rb = pl.multiple_of(cb * nb, nb)
        h_scr[pl.ds(rf, nb), 0:n2] = hf
        h_scr[pl.ds(rb, nb), n2:2 * n2] = hb
        sf = s_scr[pl.ds(rf, nb), 0:n2]
        sb = s_scr[pl.ds(rb, nb), n2:2 * n2]
        hf = af1 * hf + af2 * pltpu.roll(hf, S5_STATE, 1) + sf
        hb = ab1 * hb + ab2 * pltpu.roll(hb, S5_STATE, 1) + sb
        return hf, hb

    zero = jnp.zeros((nb, n2), F32)
    lax.fori_loop(0, n_chunks, step, (zero, zero))
    wout = wout_ref[0]
    for r0 in range(0, rows, rows_blk):
        h = h_scr[r0:r0 + rows_blk, :].astype(BF16)
        y = y_scr[r0:r0 + rows_blk, :] + jnp.dot(h, wout, preferred_element_type=F32)
        y_ref[0, r0:r0 + rows_blk, :] = y.astype(y_ref.dtype)


def _s5_scan(u, tables, n_ctx_tok):
    toep, wst, wout, adec = tables
    b, s, width = u.shape
    groups = width // S5_GROUP_CH
    n_chunks = s // S5_CHUNK
    cw = S5_CHUNK * S5_GROUP_CH
    rows = n_chunks * b
    ug = u.reshape(b, n_chunks, S5_CHUNK, groups, S5_GROUP_CH).transpose(3, 1, 0, 2, 4).reshape(groups, rows, cw)
    rows_blk = math.gcd(rows, 512)
    y = pl.pallas_call(
        functools.partial(_s5_kernel, nb=b, n_ctx=n_ctx_tok // S5_CHUNK, n_chunks=n_chunks, rows_blk=rows_blk),
        out_shape=jax.ShapeDtypeStruct((groups, rows, cw), BF16),
        grid=(groups,),
        in_specs=[pl.BlockSpec((1, rows, cw), lambda g: (g, 0, 0)),
                  pl.BlockSpec((1, cw, cw), lambda g: (g, 0, 0)),
                  pl.BlockSpec((1, cw, 4 * S5_STATE), lambda g: (g, 0, 0)),
                  pl.BlockSpec((1, 4 * S5_STATE, cw), lambda g: (g, 0, 0)),
                  pl.BlockSpec((1, 4, 2 * S5_STATE), lambda g: (g, 0, 0))],
        out_specs=pl.BlockSpec((1, rows, cw), lambda g: (g, 0, 0)),
        scratch_shapes=[pltpu.VMEM((rows, 4 * S5_STATE), F32), pltpu.VMEM((rows, 4 * S5_STATE), F32),
                        pltpu.VMEM((rows, cw), F32)],
        compiler_params=_cparams("parallel"),
        name="s5_scan",
    )(ug, toep, wst, wout, adec)
    return y.reshape(groups, n_chunks, b, S5_CHUNK, S5_GROUP_CH).transpose(2, 1, 3, 0, 4).reshape(b, s, width)


def _qk_prep_kernel(p_ref, cos_ref, sin_ref, qg_ref, kg_ref, avg_ref, q_ref, k_ref, v_ref, *, s5_w, n_q, n_kv):
    dq = n_q * HEAD_DIM
    dk = n_kv * HEAD_DIM
    avg = avg_ref[...]

    def head_norm(z, gain):
        sq = z * z
        hi = sq.astype(BF16)
        lo = (sq - hi.astype(F32)).astype(BF16)
        w = avg[:z.shape[1], :z.shape[1]]
        ms = jnp.dot(hi, w, preferred_element_type=F32) + jnp.dot(lo, w, preferred_element_type=F32)
        return z * lax.rsqrt(ms + EPS) * gain

    def rope(z, cos, sin):
        lane = lax.broadcasted_iota(jnp.int32, z.shape, 1)
        first = (lane % (2 * ROPE_FREQS)) < ROPE_FREQS
        width = z.shape[1]
        partner = jnp.where(first, pltpu.roll(z, width - ROPE_FREQS, 1), pltpu.roll(z, ROPE_FREQS, 1))
        return z * cos + partner * sin

    p = p_ref[0]
    q = rope(head_norm(p[:, s5_w:s5_w + dq], qg_ref[...]), cos_ref[...], sin_ref[...])
    k = rope(head_norm(p[:, s5_w + dq:s5_w + dq + dk], kg_ref[:, :dk]), cos_ref[:, :dk], sin_ref[:, :dk])
    v = p[:, s5_w + dq + dk:s5_w + dq + 2 * dk]
    q = q * (HEAD_DIM ** -0.5)
    for h in range(n_q):
        q_ref[0, h] = q[:, h * HEAD_DIM:(h + 1) * HEAD_DIM].astype(BF16)
    for h in range(n_kv):
        k_ref[0, h] = k[:, h * HEAD_DIM:(h + 1) * HEAD_DIM].astype(BF16)
        v_ref[0, h] = v[:, h * HEAD_DIM:(h + 1) * HEAD_DIM].astype(BF16)


def _rope_tables(seq, n_ctx_tok, width):
    t = np.arange(seq)
    pos = np.stack([t // GRID_W, t % GRID_W], axis=-1).astype(np.float32)
    inv = (ROPE_BASE ** (-np.arange(ROPE_FREQS, dtype=np.float32) / ROPE_FREQS)).astype(np.float32)
    ang = jnp.asarray(pos)[:, :, None] * jnp.asarray(inv)
    cos = jnp.cos(ang)
    sin = jnp.sin(ang)
    cos_h = jnp.concatenate([cos, cos], axis=-1).reshape(seq, HEAD_DIM)
    sin_h = jnp.concatenate([-sin, sin], axis=-1).reshape(seq, HEAD_DIM)
    cos_h = jnp.concatenate([jnp.ones((n_ctx_tok, HEAD_DIM), F32), cos_h], axis=0)
    sin_h = jnp.concatenate([jnp.zeros((n_ctx_tok, HEAD_DIM), F32), sin_h], axis=0)
    reps = width // HEAD_DIM
    return jnp.tile(cos_h, (1, reps)), jnp.tile(sin_h, (1, reps))


def _qk_prep(proj, q_g, k_g, n_ctx_tok, *, s5_w, n_q, n_kv):
    b, s, n = proj.shape
    dq = n_q * HEAD_DIM
    cos, sin = _rope_tables(s - n_ctx_tok, n_ctx_tok, dq)
    avg = jnp.asarray(np.kron(np.eye(n_q, dtype=np.float32), np.full((HEAD_DIM, HEAD_DIM), 1.0 / HEAD_DIM, np.float32)), BF16)
    qg = jnp.tile(q_g.astype(F32), n_q).reshape(1, dq)
    kg = jnp.tile(k_g.astype(F32), n_q).reshape(1, dq)
    tm = ROW_TILE
    return pl.pallas_call(
        functools.partial(_qk_prep_kernel, s5_w=s5_w, n_q=n_q, n_kv=n_kv),
        out_shape=(jax.ShapeDtypeStruct((b, n_q, s, HEAD_DIM), BF16),
                   jax.ShapeDtypeStruct((b, n_kv, s, HEAD_DIM), BF16),
                   jax.ShapeDtypeStruct((b, n_kv, s, HEAD_DIM), BF16)),
        grid=(b, s // tm),
        in_specs=[pl.BlockSpec((1, tm, n), lambda i, t: (i, t, 0)),
                  pl.BlockSpec((tm, dq), lambda i, t: (t, 0)),
                  pl.BlockSpec((tm, dq), lambda i, t: (t, 0)),
                  pl.BlockSpec((1, dq), lambda i, t: (0, 0)),
                  pl.BlockSpec((1, dq), lambda i, t: (0, 0)),
                  pl.BlockSpec((dq, dq), lambda i, t: (0, 0))],
        out_specs=(pl.BlockSpec((1, n_q, tm, HEAD_DIM), lambda i, t: (i, 0, t, 0)),
                   pl.BlockSpec((1, n_kv, tm, HEAD_DIM), lambda i, t: (i, 0, t, 0)),
                   pl.BlockSpec((1, n_kv, tm, HEAD_DIM), lambda i, t: (i, 0, t, 0))),
        compiler_params=_cparams("parallel", "arbitrary"),
        name="qk_prep",
    )(proj, cos, sin, qg, kg, avg)


def _softmax_pv(s, v):
    m = jnp.max(s, axis=-1, keepdims=True)
    p = jnp.exp(s - m)
    l = jnp.sum(p, axis=-1, keepdims=True)
    return jnp.dot(p.astype(BF16), v, preferred_element_type=F32) / l


def _gqa_kernel(q_ref, k_ref, v_ref, o_ref, *, n_kv, grp, n_ctx_tok):
    tq = q_ref.shape[2]
    nt = (((1,), (1,)), ((), ()))

    def attend(n_keys):
        for h in range(n_kv):
            q2 = q_ref[0, h * grp:(h + 1) * grp].reshape(grp * tq, HEAD_DIM)
            s = lax.dot_general(q2, k_ref[0, h, 0:n_keys, :], nt, preferred_element_type=F32)
            o = _softmax_pv(s, v_ref[0, h, 0:n_keys, :])
            for j in range(grp):
                c0 = (h * grp + j) * HEAD_DIM
                o_ref[0, :, c0:c0 + HEAD_DIM] = o[j * tq:(j + 1) * tq].astype(o_ref.dtype)

    @pl.when(pl.program_id(1) == 0)
    def _():
        attend(n_ctx_tok)

    @pl.when(pl.program_id(1) != 0)
    def _():
        attend(k_ref.shape[2])


def _gqa_attention(q, k, v, n_ctx_tok):
    b, n_q, s, _ = q.shape
    n_kv = k.shape[1]
    tq = ROW_TILE
    return pl.pallas_call(
        functools.partial(_gqa_kernel, n_kv=n_kv, grp=n_q // n_kv, n_ctx_tok=n_ctx_tok),
        out_shape=jax.ShapeDtypeStruct((b, s, n_q * HEAD_DIM), BF16),
        grid=(b, s // tq),
        in_specs=[pl.BlockSpec((1, n_q, tq, HEAD_DIM), lambda i, t: (i, 0, t, 0)),
                  pl.BlockSpec((1, n_kv, s, HEAD_DIM), lambda i, t: (i, 0, 0, 0)),
                  pl.BlockSpec((1, n_kv, s, HEAD_DIM), lambda i, t: (i, 0, 0, 0))],
        out_specs=pl.BlockSpec((1, tq, n_q * HEAD_DIM), lambda i, t: (i, t, 0)),
        compiler_params=_cparams("parallel", "arbitrary"),
        name="gqa_attention",
    )(q, k, v)


def _gelu_tanh(y):
    return 0.5 * y * (1.0 + jnp.tanh(math.sqrt(2.0 / math.pi) * (y + 0.044715 * (y * y * y))))


def _mix_even_kernel(*refs, n_src):
    src, (y_ref, att_ref, modx_ref, modc_ref, wglu_ref, wa_ref, wb_ref, o_ref) = refs[:n_src], refs[n_src:]
    bb, tm, d = src[0].shape
    is_ctx = pl.program_id(1) == 0
    mod = _pick_mod(modx_ref, modc_ref, is_ctx)
    y = y_ref[...].reshape(bb * tm, -1).astype(F32)
    g = _gelu_tanh(y)
    a = g * jax.nn.sigmoid(_bdot(g, wglu_ref[...]))
    ox = _bdot(a, wa_ref[...]) + jnp.dot(att_ref[...].reshape(bb * tm, -1), wb_ref[...], preferred_element_type=F32)
    o_ref[...] = _stream_tile(src, is_ctx) + mod[:, 2:3, :] * ox.reshape(bb, tm, d)


def _mix_even(srcs, y_s5, att, modx, modc, w_glu, w_out, bb=2):
    b, s, w5 = y_s5.shape
    d = srcs[0].shape[2]
    wa = att.shape[2]
    tm = ROW_TILE
    return pl.pallas_call(
        functools.partial(_mix_even_kernel, n_src=len(srcs)),
        out_shape=jax.ShapeDtypeStruct((b, s, d), F32),
        grid=(b // bb, s // tm),
        in_specs=_stream_specs(srcs, bb) + [
                  pl.BlockSpec((bb, tm, w5), lambda i, t: (i, t, 0)),
                  pl.BlockSpec((bb, tm, wa), lambda i, t: (i, t, 0)),
                  pl.BlockSpec((bb, 6, d), lambda i, t: (i, 0, 0)),
                  pl.BlockSpec((1, 6, d), lambda i, t: (0, 0, 0)),
                  pl.BlockSpec((w5, w5), lambda i, t: (0, 0)),
                  pl.BlockSpec((w5, d), lambda i, t: (0, 0)),
                  pl.BlockSpec((wa, d), lambda i, t: (1, 0))],
        out_specs=pl.BlockSpec((bb, tm, d), lambda i, t: (i, t, 0)),
        compiler_params=_cparams("parallel", "arbitrary"),
        name="mix_even",
    )(*srcs, y_s5, att, modx, modc, w_glu, w_out, w_out)


def _ffn_kernel(x_ref, modx_ref, modc_ref, g_ref, w1_ref, w3_ref, w2_ref, o_ref, h_scr, acc_scr):
    bb, tm, d = x_ref.shape
    j = pl.program_id(2)
    mod = _pick_mod(modx_ref, modc_ref, pl.program_id(1) == 0)

    @pl.when(j == 0)
    def _():
        h = _norm_mod(x_ref[...], g_ref[...], mod[:, 3:4, :], mod[:, 4:5, :])
        h_scr[...] = h.reshape(bb * tm, d).astype(BF16)
        acc_scr[...] = jnp.zeros_like(acc_scr)

    h = h_scr[...]
    a = jnp.dot(h, w1_ref[...], preferred_element_type=F32)
    g = jnp.dot(h, w3_ref[...], preferred_element_type=F32)
    acc_scr[...] += _bdot(_silu(a) * g, w2_ref[...])

    @pl.when(j == pl.num_programs(2) - 1)
    def _():
        o_ref[...] = x_ref[...] + mod[:, 5:6, :] * acc_scr[...].reshape(bb, tm, d)


def _ffn(stream, modx, modc, g, w1, w3, w2, bb=4, tf=512):
    b, s, d = stream.shape
    f = w1.shape[1]
    tm = ROW_TILE
    return pl.pallas_call(
        _ffn_kernel,
        out_shape=jax.ShapeDtypeStruct((b, s, d), F32),
        grid=(b // bb, s // tm, f // tf),
        in_specs=[pl.BlockSpec((bb, tm, d), lambda i, t, j: (i, t, 0)),
                  pl.BlockSpec((bb, 6, d), lambda i, t, j: (i, 0, 0)),
                  pl.BlockSpec((1, 6, d), lambda i, t, j: (0, 0, 0)),
                  pl.BlockSpec((1, d), lambda i, t, j: (0, 0)),
                  pl.BlockSpec((d, tf), lambda i, t, j: (0, j)),
                  pl.BlockSpec((d, tf), lambda i, t, j: (0, j)),
                  pl.BlockSpec((tf, d), lambda i, t, j: (j, 0))],
        out_specs=pl.BlockSpec((bb, tm, d), lambda i, t, j: (i, t, 0)),
        scratch_shapes=[pltpu.VMEM((bb * tm, d), BF16), pltpu.VMEM((bb * tm, d), F32)],
        compiler_params=_cparams("parallel", "arbitrary", "arbitrary"),
        name="ffn_dense",
    )(stream, modx, modc, g.reshape(1, d), w1, w3, w2)


def _even_layer(srcs, modx, modc, n_ctx_tok, norm1_g, norm2_g, w_in, w_out, s5_params, w_glu, q_g, k_g,
                ffn_w1, ffn_w3, ffn_w2):
    s5_w = w_glu.shape[0]
    n_q = s5_w // HEAD_DIM
    n_kv = n_q // 2
    n_t = sum(a.shape[1] for a in srcs) // ROW_TILE
    u, qkv = _in_proj(srcs, modx, modc, norm1_g, w_in.astype(BF16), t_off=0, n_t=n_t, ctx_rule=_tile0_is_ctx,
                      outs=[(0, s5_w, BF16), (s5_w, w_in.shape[1], F32)])
    y_s5 = _s5_scan(u, _s5_tables(*s5_params), n_ctx_tok)
    q, k, v = _qk_prep(qkv, q_g, k_g, n_ctx_tok, s5_w=0, n_q=n_q, n_kv=n_kv)
    att = _gqa_attention(q, k, v, n_ctx_tok)
    stream = _mix_even(srcs, y_s5, att, modx, modc, w_glu.astype(BF16), w_out.astype(BF16))
    return _ffn(stream, modx, modc, norm2_g, ffn_w1.astype(BF16), ffn_w3.astype(BF16), ffn_w2.astype(BF16))


HYENA_BANDS = 16
CONV_TILE = 256


def _hy_filter_kernel(z_ref, w1_ref, b1_ref, w2_ref, b2_ref, w3_ref, freq_ref, decay_ref, t_ref, h_ref):
    f = freq_ref[...]
    h = jnp.sin(f * (jnp.dot(z_ref[...], w1_ref[...], preferred_element_type=F32, precision=HIGHEST) + b1_ref[...]))
    h = jnp.sin(f * (jnp.dot(h, w2_ref[...], preferred_element_type=F32, precision=HIGHEST) + b2_ref[...]))
    h = jnp.dot(h, w3_ref[...], preferred_element_type=F32, precision=HIGHEST)
    h = h * jnp.exp(-t_ref[...] * jnp.abs(decay_ref[...]))
    h_ref[...] = h / (jnp.sum(jnp.abs(h), axis=0, keepdims=True) + EPS)


def _hy_filters(seq, f_w1, f_b1, f_w2, f_b2, f_w3, f_freq, f_decay):
    k = np.arange(seq, dtype=np.float32)
    t = k / max(seq - 1, 1)
    bands = np.linspace(1e-4, HYENA_BANDS - 1, HYENA_BANDS, dtype=np.float32)
    ang = jnp.asarray(np.float32(2.0 * math.pi / seq) * k[:, None] * bands[None, :])
    pos_dim, hidden = f_w1.shape
    zpad = 128
    z = jnp.concatenate([jnp.asarray(t)[:, None], jnp.cos(ang), -jnp.sin(ang),
                         jnp.zeros((seq, zpad - pos_dim), F32)], axis=-1)
    w1 = jnp.concatenate([f_w1.astype(F32), jnp.zeros((zpad - pos_dim, hidden), F32)], axis=0)
    n = f_w3.shape[1]
    tc = 512
    return pl.pallas_call(
        _hy_filter_kernel,
        out_shape=jax.ShapeDtypeStruct((seq, n), F32),
        grid=(n // tc,),
        in_specs=[pl.BlockSpec((seq, zpad), lambda j: (0, 0)),
                  pl.BlockSpec((zpad, hidden), lambda j: (0, 0)),
                  pl.BlockSpec((1, hidden), lambda j: (0, 0)),
                  pl.BlockSpec((hidden, hidden), lambda j: (0, 0)),
                  pl.BlockSpec((1, hidden), lambda j: (0, 0)),
                  pl.BlockSpec((hidden, tc), lambda j: (0, j)),
                  pl.BlockSpec((1, hidden), lambda j: (0, 0)),
                  pl.BlockSpec((1, tc), lambda j: (0, j)),
                  pl.BlockSpec((seq, 1), lambda j: (0, 0))],
        out_specs=pl.BlockSpec((seq, tc), lambda j: (0, j)),
        compiler_params=_cparams("arbitrary"),
        name="hyena_filter",
    )(z, w1, f_b1.reshape(1, hidden).astype(F32), f_w2.astype(F32), f_b2.reshape(1, hidden).astype(F32),
      f_w3.astype(F32), f_freq.reshape(1, hidden).astype(F32), f_decay.reshape(1, n).astype(F32),
      jnp.asarray(t)[:, None])


def _dft_tables(seq):
    idx = np.arange(seq, dtype=np.int64)
    m = jnp.asarray(((idx[:, None] * idx[None, :]) % (2 * seq)).astype(np.int32))
    ang = m.astype(F32) * np.float32(math.pi / seq)
    alt = jnp.asarray((1.0 - 2.0 * (idx % 2)).astype(np.float32))[:, None]
    return jnp.cos(ang).astype(BF16), jnp.sin(ang).astype(BF16), alt


def _split_bf16(a):
    hi = a.astype(BF16)
    return hi, (a - hi.astype(F32)).astype(BF16)


def _hy_spec_kernel(hf_ref, hb_ref, cos_ref, sin_ref, alt_ref, wk_ref, sre_ref, sim_ref, sny_ref, *, seq):
    hs = hf_ref[...] + hb_ref[...]
    hd = hb_ref[...] - hf_ref[...]
    s_hi, s_lo = _split_bf16(hs)
    d_hi, d_lo = _split_bf16(hd)
    c = cos_ref[...]
    s = sin_ref[...]
    wk = wk_ref[...]
    sre_ref[...] = wk * (jnp.dot(c, s_hi, preferred_element_type=F32) + jnp.dot(c, s_lo, preferred_element_type=F32))
    sim_ref[...] = wk * (jnp.dot(s, d_hi, preferred_element_type=F32) + jnp.dot(s, d_lo, preferred_element_type=F32))
    sny_ref[...] = jnp.sum(alt_ref[...] * hs, axis=0, keepdims=True) * (0.5 / seq)


def _hy_spectrum(h, cos, sin, alt):
    seq, n = h.shape
    half = n // 2
    tc = 128
    nb = half // tc
    wk = jnp.full((seq, 1), 1.0 / seq, F32).at[0, 0].set(0.5 / seq)
    return pl.pallas_call(
        functools.partial(_hy_spec_kernel, seq=seq),
        out_shape=(jax.ShapeDtypeStruct((seq, half), F32), jax.ShapeDtypeStruct((seq, half), F32),
                   jax.ShapeDtypeStruct((1, half), F32)),
        grid=(nb,),
        in_specs=[pl.BlockSpec((seq, tc), lambda j: (0, j)),
                  pl.BlockSpec((seq, tc), lambda j: (0, j + nb)),
                  pl.BlockSpec((seq, seq), lambda j: (0, 0), pipeline_mode=pl.Buffered(1)),
                  pl.BlockSpec((seq, seq), lambda j: (0, 0), pipeline_mode=pl.Buffered(1)),
                  pl.BlockSpec((seq, 1), lambda j: (0, 0)),
                  pl.BlockSpec((seq, 1), lambda j: (0, 0))],
        out_specs=(pl.BlockSpec((seq, tc), lambda j: (0, j)), pl.BlockSpec((seq, tc), lambda j: (0, j)),
                   pl.BlockSpec((1, tc), lambda j: (0, j))),
        compiler_params=_cparams("arbitrary"),
        name="hyena_spectrum",
    )(h, h, cos, sin, alt, wk)


def _hy_conv_kernel(z_ref, g_ref, cwz_ref, cbz_ref, cwg_ref, cbg_ref, cos_ref, sin_ref, alt_ref,
                    sre_ref, sim_ref, sny_ref, d_ref, o_ref, z_scr, zb_scr, yre_scr, yim_scr, *, conv_z, blk):
    seq = z_ref.shape[1]
    tc = z_ref.shape[2]
    row = lax.broadcasted_iota(jnp.int32, (blk, 1), 0)

    def short_conv(x_ref, w_ref, b_ref, l0):
        x = x_ref[0, l0:l0 + blk, :]
        w = w_ref[...]
        top = x_ref[0, l0 - 1:l0, :] if l0 > 0 else jnp.zeros((1, tc), F32)
        bot = x_ref[0, l0 + blk:l0 + blk + 1, :] if l0 + blk < seq else jnp.zeros((1, tc), F32)
        prev = jnp.where(row == 0, top, pltpu.roll(x, 1, 0))
        nxt = jnp.where(row == blk - 1, bot, pltpu.roll(x, blk - 1, 0))
        return prev * w[0:1] + x * w[1:2] + nxt * w[2:3] + b_ref[...]

    zny = jnp.zeros((1, tc), F32)
    for l0 in range(0, seq, blk):
        z = short_conv(z_ref, cwz_ref, cbz_ref, l0) if conv_z else z_ref[0, l0:l0 + blk, :]
        z_scr[l0:l0 + blk, :] = z
        zb_scr[l0:l0 + blk, :] = z.astype(BF16)
        zny = zny + jnp.sum(alt_ref[l0:l0 + blk, :] * z, axis=0, keepdims=True)
    zny = zny * sny_ref[...]
    for k0 in range(0, seq, blk):
        zc = jnp.dot(cos_ref[k0:k0 + blk, :], zb_scr[...], preferred_element_type=F32)
        zs = jnp.dot(sin_ref[k0:k0 + blk, :], zb_scr[...], preferred_element_type=F32)
        sre = sre_ref[k0:k0 + blk, :]
        sim = sim_ref[k0:k0 + blk, :]
        yre_scr[k0:k0 + blk, :] = (zc * sre + zs * sim).astype(BF16)
        yim_scr[k0:k0 + blk, :] = (zc * sim - zs * sre).astype(BF16)
    d = d_ref[0]
    for l0 in range(0, seq, blk):
        y = (jnp.dot(cos_ref[l0:l0 + blk, :], yre_scr[...], preferred_element_type=F32)
             - jnp.dot(sin_ref[l0:l0 + blk, :], yim_scr[...], preferred_element_type=F32)
             + alt_ref[l0:l0 + blk, :] * zny)
        gate = short_conv(g_ref, cwg_ref, cbg_ref, l0)
        o_ref[0, l0:l0 + blk, :] = gate * (y + d * z_scr[l0:l0 + blk, :])


def _hy_order(z_src, z_off, g_src, g_off, conv_w, conv_b, cos, sin, alt, sre, sim, sny, hy_d, order, width, conv_z):
    b, seq, _ = z_src.shape
    tc = CONV_TILE
    nb = width // tc
    zo, go = z_off // tc, g_off // tc

    def col(off):
        return lambda j, i: (0, j + off)

    return pl.pallas_call(
        functools.partial(_hy_conv_kernel, conv_z=conv_z, blk=512),
        out_shape=jax.ShapeDtypeStruct((b, seq, width), F32),
        grid=(nb, b),
        in_specs=[pl.BlockSpec((1, seq, tc), lambda j, i: (i, 0, j + zo)),
                  pl.BlockSpec((1, seq, tc), lambda j, i: (i, 0, j + go)),
                  pl.BlockSpec((3, tc), col(zo if conv_z else 0)), pl.BlockSpec((1, tc), col(zo if conv_z else 0)),
                  pl.BlockSpec((3, tc), col(go)), pl.BlockSpec((1, tc), col(go)),
                  pl.BlockSpec((seq, seq), lambda j, i: (0, 0)),
                  pl.BlockSpec((seq, seq), lambda j, i: (0, 0)),
                  pl.BlockSpec((seq, 1), lambda j, i: (0, 0)),
                  pl.BlockSpec((seq, tc), col(order * nb)),
                  pl.BlockSpec((seq, tc), col(order * nb)),
                  pl.BlockSpec((1, tc), col(order * nb)),
                  pl.BlockSpec((1, 1, tc), lambda j, i: (order, 0, j))],
        out_specs=pl.BlockSpec((1, seq, tc), lambda j, i: (i, 0, j)),
        scratch_shapes=[pltpu.VMEM((seq, tc), F32), pltpu.VMEM((seq, tc), BF16),
                        pltpu.VMEM((seq, tc), BF16), pltpu.VMEM((seq, tc), BF16)],
        compiler_params=_cparams("arbitrary", "arbitrary"),
        name="hyena_conv",
    )(z_src, g_src, conv_w, conv_b, conv_w, conv_b, cos, sin, alt, sre, sim, sny, hy_d.reshape(2, 1, width))


def _hy_conv(proj, conv_w, conv_b, cos, sin, alt, sre, sim, sny, hy_d, width):
    args = (conv_w, conv_b, cos, sin, alt, sre, sim, sny, hy_d)
    z1 = _hy_order(proj, 0, proj, width, *args, order=0, width=width, conv_z=True)
    return _hy_order(z1, 0, proj, 2 * width, *args, order=1, width=width, conv_z=False)


NA_WIN_ROWS = 8
NA_WIN_COLS = 16
NA_QROWS = ROW_TILE // GRID_W
NA_KTILES = 3


def _na_bias_tiles(rpb, rows):
    nqb = rows // NA_QROWS
    assert nqb >= 3 and rows >= NA_WIN_ROWS + NA_QROWS
    col = np.arange(GRID_W)
    col_start = np.clip(col - NA_WIN_COLS // 2, 0, GRID_W - NA_WIN_COLS)
    col_ok = (col[None, :] >= col_start[:, None]) & (col[None, :] < col_start[:, None] + NA_WIN_COLS)
    dc_idx = np.clip(col[None, :] - col[:, None] + NA_WIN_COLS - 1, 0, 2 * NA_WIN_COLS - 2)
    tiles = []
    for j in (0, 1, nqb - 1):
        kb0 = min(max(j - 1, 0), nqb - NA_KTILES)
        q_r = j * NA_QROWS + np.arange(NA_QROWS)
        k_r = kb0 * NA_QROWS + np.arange(NA_KTILES * NA_QROWS)
        r0 = np.clip(q_r - NA_WIN_ROWS // 2, 0, rows - NA_WIN_ROWS)
        row_ok = (k_r[None, :] >= r0[:, None]) & (k_r[None, :] < r0[:, None] + NA_WIN_ROWS)
        dr_idx = np.clip(k_r[None, :] - q_r[:, None] + NA_WIN_ROWS - 1, 0, 2 * NA_WIN_ROWS - 2)
        ok = row_ok[:, None, :, None] & col_ok[None, :, None, :]
        pick_r = jnp.asarray(dr_idx[:, :, None] == np.arange(2 * NA_WIN_ROWS - 1), F32)
        pick_c = jnp.asarray(dc_idx[:, :, None] == np.arange(2 * NA_WIN_COLS - 1), F32)
        by_row = jnp.einsum('hij,abi->habj', rpb.astype(F32), pick_r, precision=HIGHEST)
        bias = jnp.einsum('habj,cdj->hacbd', by_row, pick_c, precision=HIGHEST)
        tile = jnp.where(jnp.asarray(ok)[None], bias, -jnp.inf)
        tiles.append(tile.reshape(rpb.shape[0], ROW_TILE, NA_KTILES * ROW_TILE))
    return jnp.stack(tiles, axis=1)


def _na_kernel(q_ref, k0_ref, k1_ref, k2_ref, v0_ref, v1_ref, v2_ref, kc_ref, vc_ref, bias_ref, o_ref, *, n_heads):
    nt = (((1,), (1,)), ((), ()))
    q = (q_ref[0] * (HEAD_DIM ** -0.5)).astype(BF16)
    k = jnp.concatenate([k0_ref[0], k1_ref[0], k2_ref[0]], axis=0).astype(BF16)
    v = jnp.concatenate([v0_ref[0], v1_ref[0], v2_ref[0]], axis=0).astype(BF16)
    kc = kc_ref[0].astype(BF16)
    vc = vc_ref[0].astype(BF16)
    for h in range(n_heads):
        sl = slice(h * HEAD_DIM, (h + 1) * HEAD_DIM)
        s_loc = lax.dot_general(q[:, sl], k[:, sl], nt, preferred_element_type=F32) + bias_ref[h, 0]
        s_ctx = lax.dot_general(q[:, sl], kc[:, sl], nt, preferred_element_type=F32)
        m = jnp.maximum(jnp.max(s_loc, axis=-1, keepdims=True), jnp.max(s_ctx, axis=-1, keepdims=True))
        p_loc = jnp.exp(s_loc - m)
        p_ctx = jnp.exp(s_ctx - m)
        l = jnp.sum(p_loc, axis=-1, keepdims=True) + jnp.sum(p_ctx, axis=-1, keepdims=True)
        o = (jnp.dot(p_loc.astype(BF16), v[:, sl], preferred_element_type=F32)
             + jnp.dot(p_ctx.astype(BF16), vc[:, sl], preferred_element_type=F32)) / l
        o_ref[0, :, sl] = o.astype(o_ref.dtype)


def _na_attention(proj, proj_c, rpb, *, q_off, n_heads):
    b, seq, _ = proj.shape
    w = n_heads * HEAD_DIM
    tm = ROW_TILE
    nqb = seq // tm
    bias = _na_bias_tiles(rpb, seq // GRID_W)
    qc = q_off // w

    def kb0(j):
        return jnp.clip(j - 1, 0, nqb - NA_KTILES)

    def kv_spec(cb, off):
        return pl.BlockSpec((1, tm, w), lambda i, j: (i, kb0(j) + off, cb))

    def bias_type(j):
        return jnp.where(j == 0, 0, jnp.where(j == nqb - 1, 2, 1))

    return pl.pallas_call(
        functools.partial(_na_kernel, n_heads=n_heads),
        out_shape=jax.ShapeDtypeStruct((b, seq, w), BF16),
        grid=(b, nqb),
        in_specs=[pl.BlockSpec((1, tm, w), lambda i, j: (i, j, qc)),
                  kv_spec(qc + 1, 0), kv_spec(qc + 1, 1), kv_spec(qc + 1, 2),
                  kv_spec(qc + 2, 0), kv_spec(qc + 2, 1), kv_spec(qc + 2, 2),
                  pl.BlockSpec((1, proj_c.shape[1], w), lambda i, j: (i, 0, 0)),
                  pl.BlockSpec((1, proj_c.shape[1], w), lambda i, j: (i, 0, 1)),
                  pl.BlockSpec((n_heads, 1, tm, NA_KTILES * tm), lambda i, j: (0, bias_type(j), 0, 0))],
        out_specs=pl.BlockSpec((1, tm, w), lambda i, j: (i, j, 0)),
        compiler_params=_cparams("parallel", "arbitrary"),
        name="na_attention",
    )(proj, proj, proj, proj, proj, proj, proj, proj_c, proj_c, bias)


MOE_TOP_K = 2


def _mix_odd_kernel(x_ref, hy_ref, na_ref, modx_ref, g_ref, wa_ref, wb_ref, rt_ref, x1_ref, h2_ref, idx_ref, wt_ref,
                    *, n_e):
    bb, tm, d = x_ref.shape
    mod = modx_ref[...]
    ox = (_bdot(hy_ref[...].reshape(bb * tm, -1), wa_ref[...])
          + jnp.dot(na_ref[...].reshape(bb * tm, -1), wb_ref[...], preferred_element_type=F32))
    x1 = x_ref[...] + mod[:, 2:3, :] * ox.reshape(bb, tm, d)
    x1_ref[...] = x1
    h2 = _norm_mod(x1, g_ref[...], mod[:, 3:4, :], mod[:, 4:5, :])
    h2_ref[...] = _rows_to_tiles(h2.reshape(bb * tm, d)).reshape(h2_ref.shape)
    h_hi, h_lo = _split_bf16(h2.reshape(bb * tm, d))
    lg = (jnp.dot(h_hi, rt_ref[0], preferred_element_type=F32) + jnp.dot(h_lo, rt_ref[0], preferred_element_type=F32)
          + jnp.dot(h_hi, rt_ref[1], preferred_element_type=F32))
    eid = lax.broadcasted_iota(jnp.int32, lg.shape, 1)
    lg = jnp.where(eid < n_e, lg, -jnp.inf)
    m1 = jnp.max(lg, axis=1, keepdims=True)
    i1 = jnp.min(jnp.where(lg == m1, eid, n_e), axis=1, keepdims=True)
    lg2 = jnp.where(eid == i1, -jnp.inf, lg)
    m2 = jnp.max(lg2, axis=1, keepdims=True)
    i2 = jnp.min(jnp.where(lg2 == m2, eid, n_e), axis=1, keepdims=True)
    e2 = jnp.exp(m2 - m1)
    den = 1.0 + e2
    first = lax.broadcasted_iota(jnp.int32, (bb * tm, MOE_TOP_K), 1) == 0
    idx_ref[...] = jnp.where(first, i1, i2).reshape(bb, tm, MOE_TOP_K)
    wt_ref[...] = jnp.where(first, 1.0 / den, e2 / den).reshape(bb, tm, MOE_TOP_K)


ROUTER_LANES = 128


def _mix_odd(stream, t_off, o_hy, o_na, modx, g, w_out, router, bb=2):
    b, _, d = stream.shape
    seq = o_hy.shape[1]
    wh = o_hy.shape[2]
    wn = o_na.shape[2]
    n_e = router.shape[1]
    tm = ROW_TILE
    n_t = seq // tm
    router_pad = jnp.concatenate([router.astype(F32), jnp.zeros((d, ROUTER_LANES - n_e), F32)], axis=1)
    router_pad = jnp.stack(_split_bf16(router_pad))
    return pl.pallas_call(
        functools.partial(_mix_odd_kernel, n_e=n_e),
        out_shape=(jax.ShapeDtypeStruct((b, seq, d), F32), jax.ShapeDtypeStruct((b, seq) + _tile_shape(d), F32),
                   jax.ShapeDtypeStruct((b, seq, MOE_TOP_K), jnp.int32), jax.ShapeDtypeStruct((b, seq, MOE_TOP_K), F32)),
        grid=(b // bb, n_t),
        in_specs=[pl.BlockSpec((bb, tm, d), lambda i, t: (i, t + t_off, 0)),
                  pl.BlockSpec((bb, tm, wh), lambda i, t: (i, t, 0)),
                  pl.BlockSpec((bb, tm, wn), lambda i, t: (i, t, 0)),
                  pl.BlockSpec((bb, 6, d), lambda i, t: (i, 0, 0)),
                  pl.BlockSpec((1, d), lambda i, t: (0, 0)),
                  pl.BlockSpec((wh, d), lambda i, t: (0, 0)),
                  pl.BlockSpec((wn, d), lambda i, t: (1, 0)),
                  pl.BlockSpec((2, d, ROUTER_LANES), lambda i, t: (0, 0, 0))],
        out_specs=(pl.BlockSpec((bb, tm, d), lambda i, t: (i, t, 0)),
                   pl.BlockSpec((bb, tm) + _tile_shape(d), lambda i, t: (i, t, 0, 0)),
                   pl.BlockSpec((bb, tm, MOE_TOP_K), lambda i, t: (i, t, 0)),
                   pl.BlockSpec((bb, tm, MOE_TOP_K), lambda i, t: (i, t, 0))),
        compiler_params=_cparams("parallel", "arbitrary"),
        name="mix_odd_router",
    )(stream, o_hy, o_na, modx, g.reshape(1, d), w_out, w_out, router_pad)


MOE_BLOCK = 1024
MOE_FF_TILE = 896


def _moe_plan(idx, n_experts):
    t, k = idx.shape
    e_flat = idx.reshape(-1)
    onehot = (e_flat[:, None] == jnp.arange(n_experts)[None, :]).astype(jnp.int32)
    rank = jnp.sum((jnp.cumsum(onehot, axis=0) - onehot) * onehot, axis=1)
    counts = jnp.sum(onehot, axis=0)
    padded = (counts + MOE_BLOCK - 1) // MOE_BLOCK * MOE_BLOCK
    pend = jnp.cumsum(padded)
    pstart = pend - padded
    slot = pstart[e_flat] + rank
    n_blocks = (t * k) // MOE_BLOCK + n_experts
    tok = jnp.arange(t * k, dtype=jnp.int32) // k
    slot_tok = jnp.zeros((n_blocks * MOE_BLOCK,), jnp.int32).at[slot].set(tok, unique_indices=True)
    block_e = jnp.clip(jnp.searchsorted(pend, jnp.arange(n_blocks) * MOE_BLOCK, side='right'), 0, n_experts - 1)
    n_used = (pend[-1] // MOE_BLOCK).astype(jnp.int32).reshape(1)
    return (slot_tok.reshape(n_blocks, 1, MOE_BLOCK), block_e.astype(jnp.int32), n_used,
            slot.reshape(t, k).astype(jnp.int32))


def _moe_ffn_kernel(be_ref, nu_ref, tok_ref, nxt_ref, h_hbm, w1_ref, w3_ref, w2_ref, y_ref, xg_scr, xb_scr, acc_scr,
                    sem, *, n_j):
    i = pl.program_id(0)
    j = pl.program_id(1)
    used = i < nu_ref[0]
    rows = xb_scr.shape[0]
    per_step = rows // n_j
    cur = i % 2
    nxt = 1 - cur

    def row_copy(idx_ref, r, buf):
        return pltpu.make_async_copy(h_hbm.at[pl.ds(idx_ref[0, 0, r], 1)], xg_scr.at[buf, pl.ds(r, 1)], sem.at[buf])

    def wait_rows(buf):
        pltpu.make_async_copy(h_hbm.at[pl.ds(0, rows)], xg_scr.at[buf], sem.at[buf]).wait()

    @pl.when(jnp.logical_and(i == 0, j == 0))
    def _():
        def start(r, carry):
            row_copy(tok_ref, r, 0).start()
            return carry
        lax.fori_loop(0, rows, start, 0)

    @pl.when(j == 0)
    def _():
        wait_rows(cur)
        xb_scr[...] = _tiles_to_rows(xg_scr[cur]).astype(BF16)
        acc_scr[...] = jnp.zeros_like(acc_scr)

    def prefetch_rows():
        base = j * per_step
        for k in range(per_step):
            row_copy(nxt_ref, base + k, nxt).start(priority=k % 2)

    @pl.when(used)
    def _():
        xb = xb_scr[...]
        a = jnp.dot(xb, w1_ref[0], preferred_element_type=F32)
        g = jnp.dot(xb, w3_ref[0], preferred_element_type=F32)
        acc_scr[...] += _bdot(_silu(a) * g, w2_ref[0])
        prefetch_rows()

    @pl.when(jnp.logical_not(used))
    def _():
        prefetch_rows()

    @pl.when(j == n_j - 1)
    def _():
        y_ref[...] = _rows_to_tiles(acc_scr[...])

    @pl.when(jnp.logical_and(i == pl.num_programs(0) - 1, j == n_j - 1))
    def _():
        wait_rows(nxt)


def _moe_ffn(h2, slot_tok, block_e, n_used, w1, w3, w2):
    tile = h2.shape[1:]
    d = tile[0] * tile[1]
    n_blocks = slot_tok.shape[0]
    f = w1.shape[2]
    mb = MOE_BLOCK
    tf = MOE_FF_TILE
    n_j = f // tf
    grid_spec = pltpu.PrefetchScalarGridSpec(
        num_scalar_prefetch=2,
        grid=(n_blocks, n_j),
        in_specs=[pl.BlockSpec((1, 1, mb), lambda i, j, be, nu: (i, 0, 0), memory_space=pltpu.SMEM),
                  pl.BlockSpec((1, 1, mb), lambda i, j, be, nu: (jnp.minimum(i + 1, n_blocks - 1), 0, 0),
                               memory_space=pltpu.SMEM),
                  pl.BlockSpec(memory_space=pl.ANY),
                  pl.BlockSpec((1, d, tf), lambda i, j, be, nu: (be[i], 0, j)),
                  pl.BlockSpec((1, d, tf), lambda i, j, be, nu: (be[i], 0, j)),
                  pl.BlockSpec((1, tf, d), lambda i, j, be, nu: (be[i], j, 0))],
        out_specs=pl.BlockSpec((mb,) + tile, lambda i, j, be, nu: (i, 0, 0)),
        scratch_shapes=[pltpu.VMEM((2, mb) + tile, F32), pltpu.VMEM((mb, d), BF16), pltpu.VMEM((mb, d), F32),
                        pltpu.SemaphoreType.DMA((2,))],
    )
    return pl.pallas_call(
        functools.partial(_moe_ffn_kernel, n_j=n_j),
        out_shape=jax.ShapeDtypeStruct((n_blocks * mb,) + tile, F32),
        grid_spec=grid_spec,
        compiler_params=_cparams("arbitrary", "arbitrary"),
        name="moe_expert_ffn",
    )(block_e, n_used, slot_tok, slot_tok, h2, w1, w3, w2)


def _moe_combine_kernel(slot_ref, nxt_ref, y_hbm, x1_ref, wt_ref, modx_ref, g_ref, o_ref, y_scr, sem):
    tm = x1_ref.shape[1]
    n_t = pl.num_programs(1)
    step = pl.program_id(0) * n_t + pl.program_id(1)
    n_steps = pl.num_programs(0) * n_t
    cur = step % 2
    nxt = 1 - cur

    def row_copy(idx_ref, r, buf):
        return pltpu.make_async_copy(y_hbm.at[pl.ds(idx_ref[0, 0, r], 1)], y_scr.at[buf, pl.ds(r, 1)], sem.at[buf])

    def wait_rows(buf):
        pltpu.make_async_copy(y_hbm.at[pl.ds(0, MOE_TOP_K * tm)], y_scr.at[buf], sem.at[buf]).wait()

    @pl.when(step == 0)
    def _():
        def start(r, carry):
            row_copy(slot_ref, r, 0).start()
            return carry
        lax.fori_loop(0, MOE_TOP_K * tm, start, 0)

    wait_rows(cur)
    for r in range(MOE_TOP_K * tm):
        row_copy(nxt_ref, r, nxt).start(priority=r % 2)
    wt = wt_ref[...]
    y = wt[:, 0:1] * _tiles_to_rows(y_scr[cur, 0:tm]) + wt[:, 1:2] * _tiles_to_rows(y_scr[cur, tm:2 * tm])
    x2 = x1_ref[0] + modx_ref[0, 5:6, :] * y
    o_ref[0] = (x2 * lax.rsqrt(jnp.mean(x2 * x2, axis=-1, keepdims=True) + EPS)) * g_ref[...]

    @pl.when(step == n_steps - 1)
    def _():
        wait_rows(nxt)


def _moe_combine(y_slots, slot_of, wt, x1, modx, final_g):
    b, seq, d = x1.shape
    tm = ROW_TILE
    n_t = seq // tm
    n_steps = b * n_t
    slots = slot_of.reshape(n_steps, tm, MOE_TOP_K).transpose(0, 2, 1).reshape(n_steps, 1, MOE_TOP_K * tm)
    return pl.pallas_call(
        _moe_combine_kernel,
        out_shape=jax.ShapeDtypeStruct((b, seq, d), F32),
        grid=(b, n_t),
        in_specs=[pl.BlockSpec((1, 1, MOE_TOP_K * tm), lambda i, t: (i * n_t + t, 0, 0), memory_space=pltpu.SMEM),
                  pl.BlockSpec((1, 1, MOE_TOP_K * tm), lambda i, t: (jnp.minimum(i * n_t + t + 1, n_steps - 1), 0, 0),
                               memory_space=pltpu.SMEM),
                  pl.BlockSpec(memory_space=pl.ANY),
                  pl.BlockSpec((1, tm, d), lambda i, t: (i, t, 0)),
                  pl.BlockSpec((tm, MOE_TOP_K), lambda i, t: (i * n_t + t, 0)),
                  pl.BlockSpec((1, 6, d), lambda i, t: (i, 0, 0)),
                  pl.BlockSpec((1, d), lambda i, t: (0, 0))],
        out_specs=pl.BlockSpec((1, tm, d), lambda i, t: (i, t, 0)),
        scratch_shapes=[pltpu.VMEM((2, MOE_TOP_K * tm) + y_slots.shape[1:], F32), pltpu.SemaphoreType.DMA((2,))],
        compiler_params=_cparams("arbitrary", "arbitrary"),
        name="moe_combine_norm",
    )(slots, slots, y_slots, x1, wt, modx, final_g.reshape(1, d))


def _odd_layer(stream, modx, modc, n_ctx_tok, norm1_g, norm2_g, w_in, w_out, conv_w, conv_b, filt, hy_d, rpb,
               router, moe_w1, moe_w3, moe_w2, final_g):
    b, s, d = stream.shape
    seq = s - n_ctx_tok
    hy_w = hy_d.shape[1]
    n_heads = rpb.shape[0]
    na_w = n_heads * HEAD_DIM
    t_off = n_ctx_tok // ROW_TILE
    w_in_b = w_in.astype(BF16)
    proj = _in_proj((stream,), modx, modc, norm1_g, w_in_b, t_off=t_off, n_t=seq // ROW_TILE, ctx_rule=_never_ctx)
    proj_c = _in_proj((stream,), modx, modc, norm1_g, w_in_b[:, 3 * hy_w + na_w:], t_off=0, n_t=t_off,
                      ctx_rule=_always_ctx)
    cos, sin, alt = _dft_tables(seq)
    sre, sim, sny = _hy_spectrum(_hy_filters(seq, *filt), cos, sin, alt)
    o_hy = _hy_conv(proj, conv_w.astype(F32), conv_b.reshape(1, -1).astype(F32), cos, sin, alt, sre, sim, sny,
                    hy_d.astype(F32), hy_w)
    o_na = _na_attention(proj, proj_c, rpb, q_off=3 * hy_w, n_heads=n_heads)
    x1, h2, idx, wt = _mix_odd(stream, t_off, o_hy, o_na, modx, norm2_g, w_out.astype(BF16), router)
    slot_tok, block_e, n_used, slot_of = _moe_plan(idx.reshape(b * seq, MOE_TOP_K), router.shape[1])
    y_slots = _moe_ffn(h2.reshape((b * seq,) + h2.shape[2:]), slot_tok, block_e, n_used,
                       moe_w1.astype(BF16), moe_w3.astype(BF16), moe_w2.astype(BF16))
    return _moe_combine(y_slots, slot_of, wt.reshape(b * seq, MOE_TOP_K), x1, modx, final_g)


def kernel(x, c, ctx, c_ctx, mod_w, mod_b, norm1_g, norm2_g, ev_w_in, ev_w_out, s5_lam_re, s5_lam_im, s5_log_dt, s5_b_re, s5_b_im, s5_c_re, s5_c_im, s5_d, s5_w_glu, gqa_q_g, gqa_k_g, ffn_w1, ffn_w3, ffn_w2, od_w_in, od_w_out, hy_conv_w, hy_conv_b, hy_w1, hy_b1, hy_w2, hy_b2, hy_w3, hy_freq, hy_decay, hy_d, na_rpb, moe_router, moe_w1, moe_w3, moe_w2, final_g):
    b, seq, d = x.shape
    n_ctx_tok = ctx.shape[1]
    assert n_ctx_tok == ROW_TILE and seq % ROW_TILE == 0
    rows = 8 * ((b + 1 + 7) // 8)
    cvec = jnp.zeros((rows, d), F32).at[:b].set(c).at[b].set(c_ctx)
    m = _modulation(cvec, mod_w, mod_b)
    modx = [m[l, :b].reshape(b, 6, d) for l in range(2)]
    modc = [m[l, b].reshape(1, 6, d) for l in range(2)]
    s5p = (s5_lam_re[0], s5_lam_im[0], s5_log_dt[0], s5_b_re[0], s5_b_im[0], s5_c_re[0], s5_c_im[0], s5_d[0])
    stream = _even_layer((ctx, x), modx[0], modc[0], n_ctx_tok, norm1_g[0], norm2_g[0], ev_w_in[0], ev_w_out[0], s5p,
                         s5_w_glu[0], gqa_q_g[0], gqa_k_g[0], ffn_w1[0], ffn_w3[0], ffn_w2[0])
    filt = (hy_w1[0], hy_b1[0], hy_w2[0], hy_b2[0], hy_w3[0], hy_freq[0], hy_decay[0])
    return _odd_layer(stream, modx[1], modc[1], n_ctx_tok, norm1_g[1], norm2_g[1], od_w_in[0], od_w_out[0],
                      hy_conv_w[0], hy_conv_b[0], filt, hy_d[0], na_rpb[0], moe_router[0],
                      moe_w1[0], moe_w3[0], moe_w2[0], final_g)
```

```python
import functools
import math

import jax
import jax.numpy as jnp
import numpy as np
from jax import lax
from jax.experimental import pallas as pl
from jax.experimental.pallas import tpu as pltpu

F32 = jnp.float32
BF16 = jnp.bfloat16
EPS = 1e-6
HEAD_DIM = 64
GRID_W = 64
ROPE_FREQS = HEAD_DIM // 4
ROPE_BASE = 10000.0
ROW_TILE = 256
S5_GROUP_CH = 16
S5_STATE = 64
S5_CHUNK = 16
VMEM_LIMIT = 56 * 1024 * 1024
HIGHEST = lax.Precision.HIGHEST


def _cparams(*sem):
    return pltpu.CompilerParams(dimension_semantics=sem, vmem_limit_bytes=VMEM_LIMIT)


def _bdot(a, b):
    return jnp.dot(a.astype(BF16), b.astype(BF16), preferred_element_type=F32)


def _silu(a):
    return a * jax.nn.sigmoid(a)


def _norm_mod(x, g, shift, scale):
    y = x * lax.rsqrt(jnp.mean(x * x, axis=-1, keepdims=True) + EPS)
    return (y * g) * (1.0 + scale) + shift


SUBLANES = 8
LANES = 128


def _tile_shape(d):
    assert d % (SUBLANES * LANES) == 0
    return (d // LANES, LANES)


def _rows_to_tiles(a):
    return a.reshape(a.shape[0], a.shape[1] // LANES, LANES)


def _tiles_to_rows(a):
    return a.reshape(a.shape[0], a.shape[1] * a.shape[2])


def _pick_mod(modx_ref, modc_ref, is_ctx):
    return jnp.where(is_ctx, modc_ref[...], modx_ref[...])


def _mod_kernel(c_ref, w_ref, b_ref, o_ref):
    s = _silu(c_ref[...])
    o_ref[0] = jnp.dot(s, w_ref[0], preferred_element_type=F32, precision=HIGHEST) + b_ref[0]


def _modulation(cvec, mod_w, mod_b):
    depth, d, n = mod_w.shape
    rows = cvec.shape[0]
    tn = 1024
    return pl.pallas_call(
        _mod_kernel,
        out_shape=jax.ShapeDtypeStruct((depth, rows, n), F32),
        grid=(depth, n // tn),
        in_specs=[pl.BlockSpec((rows, d), lambda l, j: (0, 0)),
                  pl.BlockSpec((1, d, tn), lambda l, j: (l, 0, j)),
                  pl.BlockSpec((1, 1, tn), lambda l, j: (l, 0, j))],
        out_specs=pl.BlockSpec((1, rows, tn), lambda l, j: (l, 0, j)),
        compiler_params=_cparams("arbitrary", "arbitrary"),
        name="adaln_mod",
    )(cvec, mod_w, mod_b.reshape(depth, 1, n))


def _stream_specs(srcs, bb, t_off=0):
    d = srcs[0].shape[2]
    if len(srcs) == 1:
        return [pl.BlockSpec((bb, ROW_TILE, d), lambda i, t, *_: (i, t + t_off, 0))]
    return [pl.BlockSpec((bb, ROW_TILE, d), lambda i, t, *_: (i, 0, 0)),
            pl.BlockSpec((bb, ROW_TILE, d), lambda i, t, *_: (i, jnp.maximum(t - 1, 0), 0))]


def _stream_tile(refs, is_ctx):
    if len(refs) == 1:
        return refs[0][...]
    return jnp.where(is_ctx, refs[0][...], refs[1][...])


def _in_kernel(*refs, ctx_rule, n_src, cuts):
    src, (modx_ref, modc_ref, g_ref, w_ref), o_refs = refs[:n_src], refs[n_src:n_src + 4], refs[n_src + 4:]
    bb, tm, d = src[0].shape
    is_ctx = ctx_rule(pl.program_id(1))
    mod = _pick_mod(modx_ref, modc_ref, is_ctx)
    h = _norm_mod(_stream_tile(src, is_ctx), g_ref[...], mod[:, 0:1, :], mod[:, 1:2, :])
    o = _bdot(h.reshape(bb * tm, d), w_ref[...])
    for o_ref, (c0, c1) in zip(o_refs, cuts):
        o_ref[...] = o[:, c0:c1].reshape(bb, tm, c1 - c0).astype(o_ref.dtype)


def _in_proj(srcs, modx, modc, g, w, *, t_off, n_t, ctx_rule, outs=None, bb=2):
    b, _, d = srcs[0].shape
    n = w.shape[1]
    outs = outs or [(0, n, F32)]
    res = pl.pallas_call(
        functools.partial(_in_kernel, ctx_rule=ctx_rule, n_src=len(srcs), cuts=[(c0, c1) for c0, c1, _ in outs]),
        out_shape=[jax.ShapeDtypeStruct((b, n_t * ROW_TILE, c1 - c0), dt) for c0, c1, dt in outs],
        grid=(b // bb, n_t),
        in_specs=_stream_specs(srcs, bb, t_off) + [
            pl.BlockSpec((bb, 6, d), lambda i, t: (i, 0, 0)),
            pl.BlockSpec((1, 6, d), lambda i, t: (0, 0, 0)),
            pl.BlockSpec((1, d), lambda i, t: (0, 0)),
            pl.BlockSpec((d, n), lambda i, t: (0, 0))],
        out_specs=[pl.BlockSpec((bb, ROW_TILE, c1 - c0), lambda i, t: (i, t, 0)) for c0, c1, _ in outs],
        compiler_params=_cparams("parallel", "arbitrary"),
        name="in_proj",
    )(*srcs, modx, modc, g.reshape(1, d), w)
    return res if len(res) > 1 else res[0]


def _tile0_is_ctx(t):
    return t == 0


def _never_ctx(t):
    return t < 0


def _always_ctx(t):
    return t >= 0


def _s5_tables(lam_re, lam_im, log_dt, b_re, b_im, c_re, c_im, d_skip):
    t_len, n_st, p_ch = S5_CHUNK, S5_STATE, S5_GROUP_CH
    groups = lam_re.shape[1]
    dt = jnp.exp(log_dt.astype(F32))[..., None]
    lr = lam_re.astype(F32)
    li = lam_im.astype(F32)
    mag = jnp.exp(lr * dt)
    ar = mag * jnp.cos(li * dt)
    ai = mag * jnp.sin(li * dt)
    nr = ar - 1.0
    den = lr * lr + li * li
    kr = (nr * lr + ai * li) / den
    ki = (ai * lr - nr * li) / den
    br = b_re.astype(F32)
    bi = b_im.astype(F32)
    bbr = kr[..., None] * br - ki[..., None] * bi
    bbi = kr[..., None] * bi + ki[..., None] * br
    cr = c_re.astype(F32)
    ci = c_im.astype(F32)
    pr = [jnp.ones_like(ar)]
    pi = [jnp.zeros_like(ai)]
    for _ in range(t_len):
        pr.append(pr[-1] * ar - pi[-1] * ai)
        pi.append(pr[-2] * ai + pi[-1] * ar)
    pr = jnp.stack(pr)
    pi = jnp.stack(pi)
    er = cr[None] * pr[:, :, :, None, :] - ci[None] * pi[:, :, :, None, :]
    ei = cr[None] * pi[:, :, :, None, :] + ci[None] * pr[:, :, :, None, :]
    kern = (jnp.einsum('jdgqn,dgnp->jdgqp', er, bbr, precision=HIGHEST)
            - jnp.einsum('jdgqn,dgnp->jdgqp', ei, bbi, precision=HIGHEST))
    s_idx = np.arange(t_len)[:, None]
    t_idx = np.arange(t_len)[None, :]
    lag_f = np.clip(t_idx - s_idx, 0, t_len - 1)
    lag_b = np.clip(s_idx - t_idx, 0, t_len - 1)
    kf = kern[:, 0][lag_f]
    kb = kern[:, 1][lag_b]
    mask_f = jnp.asarray(s_idx <= t_idx, F32)[:, :, None, None, None]
    mask_b = jnp.asarray(s_idx >= t_idx, F32)[:, :, None, None, None]
    dmat = jnp.eye(p_ch, dtype=F32)[None] * d_skip.astype(F32).reshape(groups, 1, p_ch)
    eye_t = jnp.asarray(s_idx == t_idx, F32)[:, :, None, None, None]
    full = kf * mask_f + kb * mask_b + eye_t * dmat[None, None]
    toep = full.transpose(2, 0, 4, 1, 3).reshape(groups, t_len * p_ch, t_len * p_ch)
    def drive(pw_r, pw_i, d):
        re = pw_r[..., None] * bbr[d][None] - pw_i[..., None] * bbi[d][None]
        im = pw_r[..., None] * bbi[d][None] + pw_i[..., None] * bbr[d][None]
        return re.transpose(1, 0, 3, 2), im.transpose(1, 0, 3, 2)
    f_re, f_im = drive(pr[:t_len, 0][::-1], pi[:t_len, 0][::-1], 0)
    b_re2, b_im2 = drive(pr[:t_len, 1], pi[:t_len, 1], 1)
    wst = jnp.concatenate([f_re, f_im, b_re2, b_im2], axis=-1).reshape(groups, t_len * p_ch, 4 * n_st)
    def read(e_r, e_i):
        return e_r.transpose(1, 3, 0, 2), (-e_i).transpose(1, 3, 0, 2)
    of_re, of_im = read(er[1:, 0], ei[1:, 0])
    ob_re, ob_im = read(er[1:, 1][::-1], ei[1:, 1][::-1])
    wout = jnp.concatenate([of_re, of_im, ob_re, ob_im], axis=1).reshape(groups, 4 * n_st, t_len * p_ch)
    a_r = pr[t_len]
    a_i = pi[t_len]
    adec = jnp.stack([jnp.concatenate([a_r[0], a_r[0]], -1), jnp.concatenate([-a_i[0], a_i[0]], -1),
                      jnp.concatenate([a_r[1], a_r[1]], -1), jnp.concatenate([-a_i[1], a_i[1]], -1)], axis=1)
    return toep.astype(BF16), wst.astype(BF16), wout.astype(BF16), adec


def _s5_kernel(u_ref, toep_ref, wst_ref, wout_ref, a_ref, y_ref, s_scr, h_scr, y_scr, *, nb, n_ctx, n_chunks,
               rows_blk):
    rows = u_ref.shape[1]
    n2 = 2 * S5_STATE
    toep = toep_ref[0]
    wst = wst_ref[0]
    for r0 in range(0, rows, rows_blk):
        u = u_ref[0, r0:r0 + rows_blk, :].astype(BF16)
        y_scr[r0:r0 + rows_blk, :] = jnp.dot(u, toep, preferred_element_type=F32)
        s_scr[r0:r0 + rows_blk, :] = jnp.dot(u, wst, preferred_element_type=F32)
    a = a_ref[0]
    af1, af2, ab1, ab2 = a[0:1], a[1:2], a[2:3], a[3:4]

    def step(i, carry):
        hf, hb = carry
        cb = jnp.where(i < n_ctx, n_ctx - 1 - i, n_chunks - 1 - (i - n_ctx))
        rf = pl.multiple_of(i * nb, nb)
        rb = pl.multiple_of(cb * nb, nb)
        h_scr[pl.ds(rf, nb), 0:n2] = hf
        h_scr[pl.ds(rb, nb), n2:2 * n2] = hb
        sf = s_scr[pl.ds(rf, nb), 0:n2]
        sb = s_scr[pl.ds(rb, nb), n2:2 * n2]
        hf = af1 * hf + af2 * pltpu.roll(hf, S5_STATE, 1) + sf
        hb = ab1 * hb + ab2 * pltpu.roll(hb, S5_STATE, 1) + sb
        return hf, hb

    zero = jnp.zeros((nb, n2), F32)
    lax.fori_loop(0, n_chunks, step, (zero, zero))
    wout = wout_ref[0]
    for r0 in range(0, rows, rows_blk):
        h = h_scr[r0:r0 + rows_blk, :].astype(BF16)
        y = y_scr[r0:r0 + rows_blk, :] + jnp.dot(h, wout, preferred_element_type=F32)
        y_ref[0, r0:r0 + rows_blk, :] = y.astype(y_ref.dtype)


def _s5_scan(u, tables, n_ctx_tok):
    toep, wst, wout, adec = tables
    b, s, width = u.shape
    groups = width // S5_GROUP_CH
    n_chunks = s // S5_CHUNK
    cw = S5_CHUNK * S5_GROUP_CH
    rows = n_chunks * b
    ug = u.reshape(b, n_chunks, S5_CHUNK, groups, S5_GROUP_CH).transpose(3, 1, 0, 2, 4).reshape(groups, rows, cw)
    rows_blk = math.gcd(rows, 512)
    y = pl.pallas_call(
        functools.partial(_s5_kernel, nb=b, n_ctx=n_ctx_tok // S5_CHUNK, n_chunks=n_chunks, rows_blk=rows_blk),
        out_shape=jax.ShapeDtypeStruct((groups, rows, cw), BF16),
        grid=(groups,),
        in_specs=[pl.BlockSpec((1, rows, cw), lambda g: (g, 0, 0)),
                  pl.BlockSpec((1, cw, cw), lambda g: (g, 0, 0)),
                  pl.BlockSpec((1, cw, 4 * S5_STATE), lambda g: (g, 0, 0)),
                  pl.BlockSpec((1, 4 * S5_STATE, cw), lambda g: (g, 0, 0)),
                  pl.BlockSpec((1, 4, 2 * S5_STATE), lambda g: (g, 0, 0))],
        out_specs=pl.BlockSpec((1, rows, cw), lambda g: (g, 0, 0)),
        scratch_shapes=[pltpu.VMEM((rows, 4 * S5_STATE), F32), pltpu.VMEM((rows, 4 * S5_STATE), F32),
                        pltpu.VMEM((rows, cw), F32)],
        compiler_params=_cparams("parallel"),
        name="s5_scan",
    )(ug, toep, wst, wout, adec)
    return y.reshape(groups, n_chunks, b, S5_CHUNK, S5_GROUP_CH).transpose(2, 1, 3, 0, 4).reshape(b, s, width)


def _qk_prep_kernel(p_ref, cos_ref, sin_ref, qg_ref, kg_ref, avg_ref, q_ref, k_ref, v_ref, *, s5_w, n_q, n_kv):
    dq = n_q * HEAD_DIM
    dk = n_kv * HEAD_DIM
    avg = avg_ref[...]

    def head_norm(z, gain):
        sq = z * z
        hi = sq.astype(BF16)
        lo = (sq - hi.astype(F32)).astype(BF16)
        w = avg[:z.shape[1], :z.shape[1]]
        ms = jnp.dot(hi, w, preferred_element_type=F32) + jnp.dot(lo, w, preferred_element_type=F32)
        return z * lax.rsqrt(ms + EPS) * gain

    def rope(z, cos, sin):
        lane = lax.broadcasted_iota(jnp.int32, z.shape, 1)
        first = (lane % (2 * ROPE_FREQS)) < ROPE_FREQS
        width = z.shape[1]
        partner = jnp.where(first, pltpu.roll(z, width - ROPE_FREQS, 1), pltpu.roll(z, ROPE_FREQS, 1))
        return z * cos + partner * sin

    p = p_ref[0]
    q = rope(head_norm(p[:, s5_w:s5_w + dq], qg_ref[...]), cos_ref[...], sin_ref[...])
    k = rope(head_norm(p[:, s5_w + dq:s5_w + dq + dk], kg_ref[:, :dk]), cos_ref[:, :dk], sin_ref[:, :dk])
    v = p[:, s5_w + dq + dk:s5_w + dq + 2 * dk]
    q = q * (HEAD_DIM ** -0.5)
    for h in range(n_q):
        q_ref[0, h] = q[:, h * HEAD_DIM:(h + 1) * HEAD_DIM].astype(BF16)
    for h in range(n_kv):
        k_ref[0, h] = k[:, h * HEAD_DIM:(h + 1) * HEAD_DIM].astype(BF16)
        v_ref[0, h] = v[:, h * HEAD_DIM:(h + 1) * HEAD_DIM].astype(BF16)


def _rope_tables(seq, n_ctx_tok, width):
    t = np.arange(seq)
    pos = np.stack([t // GRID_W, t % GRID_W], axis=-1).astype(np.float32)
    inv = (ROPE_BASE ** (-np.arange(ROPE_FREQS, dtype=np.float32) / ROPE_FREQS)).astype(np.float32)
    ang = jnp.asarray(pos)[:, :, None] * jnp.asarray(inv)
    cos = jnp.cos(ang)
    sin = jnp.sin(ang)
    cos_h = jnp.concatenate([cos, cos], axis=-1).reshape(seq, HEAD_DIM)
    sin_h = jnp.concatenate([-sin, sin], axis=-1).reshape(seq, HEAD_DIM)
    cos_h = jnp.concatenate([jnp.ones((n_ctx_tok, HEAD_DIM), F32), cos_h], axis=0)
    sin_h = jnp.concatenate([jnp.zeros((n_ctx_tok, HEAD_DIM), F32), sin_h], axis=0)
    reps = width // HEAD_DIM
    return jnp.tile(cos_h, (1, reps)), jnp.tile(sin_h, (1, reps))


def _qk_prep(proj, q_g, k_g, n_ctx_tok, *, s5_w, n_q, n_kv):
    b, s, n = proj.shape
    dq = n_q * HEAD_DIM
    cos, sin = _rope_tables(s - n_ctx_tok, n_ctx_tok, dq)
    avg = jnp.asarray(np.kron(np.eye(n_q, dtype=np.float32), np.full((HEAD_DIM, HEAD_DIM), 1.0 / HEAD_DIM, np.float32)), BF16)
    qg = jnp.tile(q_g.astype(F32), n_q).reshape(1, dq)
    kg = jnp.tile(k_g.astype(F32), n_q).reshape(1, dq)
    tm = ROW_TILE
    return pl.pallas_call(
        functools.partial(_qk_prep_kernel, s5_w=s5_w, n_q=n_q, n_kv=n_kv),
        out_shape=(jax.ShapeDtypeStruct((b, n_q, s, HEAD_DIM), BF16),
                   jax.ShapeDtypeStruct((b, n_kv, s, HEAD_DIM), BF16),
                   jax.ShapeDtypeStruct((b, n_kv, s, HEAD_DIM), BF16)),
        grid=(b, s // tm),
        in_specs=[pl.BlockSpec((1, tm, n), lambda i, t: (i, t, 0)),
                  pl.BlockSpec((tm, dq), lambda i, t: (t, 0)),
                  pl.BlockSpec((tm, dq), lambda i, t: (t, 0)),
                  pl.BlockSpec((1, dq), lambda i, t: (0, 0)),
                  pl.BlockSpec((1, dq), lambda i, t: (0, 0)),
                  pl.BlockSpec((dq, dq), lambda i, t: (0, 0))],
        out_specs=(pl.BlockSpec((1, n_q, tm, HEAD_DIM), lambda i, t: (i, 0, t, 0)),
                   pl.BlockSpec((1, n_kv, tm, HEAD_DIM), lambda i, t: (i, 0, t, 0)),
                   pl.BlockSpec((1, n_kv, tm, HEAD_DIM), lambda i, t: (i, 0, t, 0))),
        compiler_params=_cparams("parallel", "arbitrary"),
        name="qk_prep",
    )(proj, cos, sin, qg, kg, avg)


def _softmax_pv(s, v):
    m = jnp.max(s, axis=-1, keepdims=True)
    p = jnp.exp(s - m)
    l = jnp.sum(p, axis=-1, keepdims=True)
    return jnp.dot(p.astype(BF16), v, preferred_element_type=F32) / l


def _gqa_kernel(q_ref, k_ref, v_ref, o_ref, *, n_kv, grp, n_ctx_tok):
    tq = q_ref.shape[2]
    nt = (((1,), (1,)), ((), ()))

    def attend(n_keys):
        for h in range(n_kv):
            q2 = q_ref[0, h * grp:(h + 1) * grp].reshape(grp * tq, HEAD_DIM)
            s = lax.dot_general(q2, k_ref[0, h, 0:n_keys, :], nt, preferred_element_type=F32)
            o = _softmax_pv(s, v_ref[0, h, 0:n_keys, :])
            for j in range(grp):
                c0 = (h * grp + j) * HEAD_DIM
                o_ref[0, :, c0:c0 + HEAD_DIM] = o[j * tq:(j + 1) * tq].astype(o_ref.dtype)

    @pl.when(pl.program_id(1) == 0)
    def _():
        attend(n_ctx_tok)

    @pl.when(pl.program_id(1) != 0)
    def _():
        attend(k_ref.shape[2])


def _gqa_attention(q, k, v, n_ctx_tok):
    b, n_q, s, _ = q.shape
    n_kv = k.shape[1]
    tq = ROW_TILE
    return pl.pallas_call(
        functools.partial(_gqa_kernel, n_kv=n_kv, grp=n_q // n_kv, n_ctx_tok=n_ctx_tok),
        out_shape=jax.ShapeDtypeStruct((b, s, n_q * HEAD_DIM), BF16),
        grid=(b, s // tq),
        in_specs=[pl.BlockSpec((1, n_q, tq, HEAD_DIM), lambda i, t: (i, 0, t, 0)),
                  pl.BlockSpec((1, n_kv, s, HEAD_DIM), lambda i, t: (i, 0, 0, 0)),
                  pl.BlockSpec((1, n_kv, s, HEAD_DIM), lambda i, t: (i, 0, 0, 0))],
        out_specs=pl.BlockSpec((1, tq, n_q * HEAD_DIM), lambda i, t: (i, t, 0)),
        compiler_params=_cparams("parallel", "arbitrary"),
        name="gqa_attention",
    )(q, k, v)


def _gelu_tanh(y):
    return 0.5 * y * (1.0 + jnp.tanh(math.sqrt(2.0 / math.pi) * (y + 0.044715 * (y * y * y))))


def _mix_even_kernel(*refs, n_src):
    src, (y_ref, att_ref, modx_ref, modc_ref, wglu_ref, wa_ref, wb_ref, o_ref) = refs[:n_src], refs[n_src:]
    bb, tm, d = src[0].shape
    is_ctx = pl.program_id(1) == 0
    mod = _pick_mod(modx_ref, modc_ref, is_ctx)
    y = y_ref[...].reshape(bb * tm, -1).astype(F32)
    g = _gelu_tanh(y)
    a = g * jax.nn.sigmoid(_bdot(g, wglu_ref[...]))
    ox = _bdot(a, wa_ref[...]) + jnp.dot(att_ref[...].reshape(bb * tm, -1), wb_ref[...], preferred_element_type=F32)
    o_ref[...] = _stream_tile(src, is_ctx) + mod[:, 2:3, :] * ox.reshape(bb, tm, d)


def _mix_even(srcs, y_s5, att, modx, modc, w_glu, w_out, bb=2):
    b, s, w5 = y_s5.shape
    d = srcs[0].shape[2]
    wa = att.shape[2]
    tm = ROW_TILE
    return pl.pallas_call(
        functools.partial(_mix_even_kernel, n_src=len(srcs)),
        out_shape=jax.ShapeDtypeStruct((b, s, d), F32),
        grid=(b // bb, s // tm),
        in_specs=_stream_specs(srcs, bb) + [
                  pl.BlockSpec((bb, tm, w5), lambda i, t: (i, t, 0)),
                  pl.BlockSpec((bb, tm, wa), lambda i, t: (i, t, 0)),
                  pl.BlockSpec((bb, 6, d), lambda i, t: (i, 0, 0)),
                  pl.BlockSpec((1, 6, d), lambda i, t: (0, 0, 0)),
                  pl.BlockSpec((w5, w5), lambda i, t: (0, 0)),
                  pl.BlockSpec((w5, d), lambda i, t: (0, 0)),
                  pl.BlockSpec((wa, d), lambda i, t: (1, 0))],
        out_specs=pl.BlockSpec((bb, tm, d), lambda i, t: (i, t, 0)),
        compiler_params=_cparams("parallel", "arbitrary"),
        name="mix_even",
    )(*srcs, y_s5, att, modx, modc, w_glu, w_out, w_out)


def _ffn_kernel(x_ref, modx_ref, modc_ref, g_ref, w1_ref, w3_ref, w2_ref, o_ref, h_scr, acc_scr):
    bb, tm, d = x_ref.shape
    j = pl.program_id(2)
    mod = _pick_mod(modx_ref, modc_ref, pl.program_id(1) == 0)

    @pl.when(j == 0)
    def _():
        h = _norm_mod(x_ref[...], g_ref[...], mod[:, 3:4, :], mod[:, 4:5, :])
        h_scr[...] = h.reshape(bb * tm, d).astype(BF16)
        acc_scr[...] = jnp.zeros_like(acc_scr)

    h = h_scr[...]
    a = jnp.dot(h, w1_ref[...], preferred_element_type=F32)
    g = jnp.dot(h, w3_ref[...], preferred_element_type=F32)
    acc_scr[...] += _bdot(_silu(a) * g, w2_ref[...])

    @pl.when(j == pl.num_programs(2) - 1)
    def _():
        o_ref[...] = x_ref[...] + mod[:, 5:6, :] * acc_scr[...].reshape(bb, tm, d)


def _ffn(stream, modx, modc, g, w1, w3, w2, bb=4, tf=512):
    b, s, d = stream.shape
    f = w1.shape[1]
    tm = ROW_TILE
    return pl.pallas_call(
        _ffn_kernel,
        out_shape=jax.ShapeDtypeStruct((b, s, d), F32),
        grid=(b // bb, s // tm, f // tf),
        in_specs=[pl.BlockSpec((bb, tm, d), lambda i, t, j: (i, t, 0)),
                  pl.BlockSpec((bb, 6, d), lambda i, t, j: (i, 0, 0)),
                  pl.BlockSpec((1, 6, d), lambda i, t, j: (0, 0, 0)),
                  pl.BlockSpec((1, d), lambda i, t, j: (0, 0)),
                  pl.BlockSpec((d, tf), lambda i, t, j: (0, j)),
                  pl.BlockSpec((d, tf), lambda i, t, j: (0, j)),
                  pl.BlockSpec((tf, d), lambda i, t, j: (j, 0))],
        out_specs=pl.BlockSpec((bb, tm, d), lambda i, t, j: (i, t, 0)),
        scratch_shapes=[pltpu.VMEM((bb * tm, d), BF16), pltpu.VMEM((bb * tm, d), F32)],
        compiler_params=_cparams("parallel", "arbitrary", "arbitrary"),
        name="ffn_dense",
    )(stream, modx, modc, g.reshape(1, d), w1, w3, w2)


def _even_layer(srcs, modx, modc, n_ctx_tok, norm1_g, norm2_g, w_in, w_out, s5_params, w_glu, q_g, k_g,
                ffn_w1, ffn_w3, ffn_w2):
    s5_w = w_glu.shape[0]
    n_q = s5_w // HEAD_DIM
    n_kv = n_q // 2
    n_t = sum(a.shape[1] for a in srcs) // ROW_TILE
    u, qkv = _in_proj(srcs, modx, modc, norm1_g, w_in.astype(BF16), t_off=0, n_t=n_t, ctx_rule=_tile0_is_ctx,
                      outs=[(0, s5_w, BF16), (s5_w, w_in.shape[1], F32)])
    y_s5 = _s5_scan(u, _s5_tables(*s5_params), n_ctx_tok)
    q, k, v = _qk_prep(qkv, q_g, k_g, n_ctx_tok, s5_w=0, n_q=n_q, n_kv=n_kv)
    att = _gqa_attention(q, k, v, n_ctx_tok)
    stream = _mix_even(srcs, y_s5, att, modx, modc, w_glu.astype(BF16), w_out.astype(BF16))
    return _ffn(stream, modx, modc, norm2_g, ffn_w1.astype(BF16), ffn_w3.astype(BF16), ffn_w2.astype(BF16))


HYENA_BANDS = 16
CONV_TILE = 256


def _hy_filter_kernel(z_ref, w1_ref, b1_ref, w2_ref, b2_ref, w3_ref, freq_ref, decay_ref, t_ref, h_ref):
    f = freq_ref[...]
    h = jnp.sin(f * (jnp.dot(z_ref[...], w1_ref[...], preferred_element_type=F32, precision=HIGHEST) + b1_ref[...]))
    h = jnp.sin(f * (jnp.dot(h, w2_ref[...], preferred_element_type=F32, precision=HIGHEST) + b2_ref[...]))
    h = jnp.dot(h, w3_ref[...], preferred_element_type=F32, precision=HIGHEST)
    h = h * jnp.exp(-t_ref[...] * jnp.abs(decay_ref[...]))
    h_ref[...] = h / (jnp.sum(jnp.abs(h), axis=0, keepdims=True) + EPS)


def _hy_filters(seq, f_w1, f_b1, f_w2, f_b2, f_w3, f_freq, f_decay):
    k = np.arange(seq, dtype=np.float32)
    t = k / max(seq - 1, 1)
    bands = np.linspace(1e-4, HYENA_BANDS - 1, HYENA_BANDS, dtype=np.float32)
    ang = jnp.asarray(np.float32(2.0 * math.pi / seq) * k[:, None] * bands[None, :])
    pos_dim, hidden = f_w1.shape
    zpad = 128
    z = jnp.concatenate([jnp.asarray(t)[:, None], jnp.cos(ang), -jnp.sin(ang),
                         jnp.zeros((seq, zpad - pos_dim), F32)], axis=-1)
    w1 = jnp.concatenate([f_w1.astype(F32), jnp.zeros((zpad - pos_dim, hidden), F32)], axis=0)
    n = f_w3.shape[1]
    tc = 512
    return pl.pallas_call(
        _hy_filter_kernel,
        out_shape=jax.ShapeDtypeStruct((seq, n), F32),
        grid=(n // tc,),
        in_specs=[pl.BlockSpec((seq, zpad), lambda j: (0, 0)),
                  pl.BlockSpec((zpad, hidden), lambda j: (0, 0)),
                  pl.BlockSpec((1, hidden), lambda j: (0, 0)),
                  pl.BlockSpec((hidden, hidden), lambda j: (0, 0)),
                  pl.BlockSpec((1, hidden), lambda j: (0, 0)),
                  pl.BlockSpec((hidden, tc), lambda j: (0, j)),
                  pl.BlockSpec((1, hidden), lambda j: (0, 0)),
                  pl.BlockSpec((1, tc), lambda j: (0, j)),
                  pl.BlockSpec((seq, 1), lambda j: (0, 0))],
        out_specs=pl.BlockSpec((seq, tc), lambda j: (0, j)),
        compiler_params=_cparams("arbitrary"),
        name="hyena_filter",
    )(z, w1, f_b1.reshape(1, hidden).astype(F32), f_w2.astype(F32), f_b2.reshape(1, hidden).astype(F32),
      f_w3.astype(F32), f_freq.reshape(1, hidden).astype(F32), f_decay.reshape(1, n).astype(F32),
      jnp.asarray(t)[:, None])


def _dft_tables(seq):
    idx = np.arange(seq, dtype=np.int64)
    m = jnp.asarray(((idx[:, None] * idx[None, :]) % (2 * seq)).astype(np.int32))
    ang = m.astype(F32) * np.float32(math.pi / seq)
    alt = jnp.asarray((1.0 - 2.0 * (idx % 2)).astype(np.float32))[:, None]
    return jnp.cos(ang).astype(BF16), jnp.sin(ang).astype(BF16), alt


def _split_bf16(a):
    hi = a.astype(BF16)
    return hi, (a - hi.astype(F32)).astype(BF16)


def _hy_block_kernels(h):
    seq, n2 = h.shape
    n = n2 // 2
    hh = seq // 2
    hf, hb = h[:, :n], h[:, n:]
    zero = jnp.zeros((1, n), h.dtype)
    t0_lo = hf[:hh].at[0].add(hb[0])
    t0_hi = jnp.concatenate([zero, hb[1:hh][::-1]], axis=0)
    t1_lo = hf[hh:]
    t1_hi = jnp.concatenate([zero, hf[1:hh]], axis=0)
    tm_lo = hb[1:hh + 1][::-1]
    tm_hi = jnp.concatenate([zero, hb[hh + 1:][::-1]], axis=0)
    return jnp.stack([t0_lo, t0_hi, t1_lo, t1_hi, tm_lo, tm_hi])


def _hy_spec_kernel(lo_ref, hi_ref, cos_ref, sin_ref, alt_ref, wk_ref, sre_ref, sim_ref, sny_ref, *, half_len):
    c = cos_ref[...]
    s = sin_ref[...]
    alt = alt_ref[...]
    wk = wk_ref[...]

    def dft(tab, a):
        a_hi, a_lo = _split_bf16(a)
        return jnp.dot(tab, a_hi, preferred_element_type=F32) + jnp.dot(tab, a_lo, preferred_element_type=F32)

    lo = lo_ref[0]
    hi = hi_ref[0]
    sre_ref[0] = wk * (dft(c, lo) + alt * dft(c, hi))
    sim_ref[0] = -wk * (dft(s, lo) + alt * dft(s, hi))
    sny_ref[0] = jnp.sum(alt * (lo + hi), axis=0, keepdims=True) * (0.5 / half_len)


def _hy_spectrum(kernels, cos, sin, alt):
    _, hh, n = kernels.shape
    tc = 128
    nb = n // tc
    wk = jnp.full((hh, 1), 1.0 / hh, F32).at[0, 0].set(0.5 / hh)
    return pl.pallas_call(
        functools.partial(_hy_spec_kernel, half_len=hh),
        out_shape=(jax.ShapeDtypeStruct((3, hh, n), F32), jax.ShapeDtypeStruct((3, hh, n), F32),
                   jax.ShapeDtypeStruct((3, 1, n), F32)),
        grid=(3, nb),
        in_specs=[pl.BlockSpec((1, hh, tc), lambda k, j: (2 * k, 0, j)),
                  pl.BlockSpec((1, hh, tc), lambda k, j: (2 * k + 1, 0, j)),
                  pl.BlockSpec((hh, hh), lambda k, j: (0, 0)),
                  pl.BlockSpec((hh, hh), lambda k, j: (0, 0)),
                  pl.BlockSpec((hh, 1), lambda k, j: (0, 0)),
                  pl.BlockSpec((hh, 1), lambda k, j: (0, 0))],
        out_specs=(pl.BlockSpec((1, hh, tc), lambda k, j: (k, 0, j)), pl.BlockSpec((1, hh, tc), lambda k, j: (k, 0, j)),
                   pl.BlockSpec((1, 1, tc), lambda k, j: (k, 0, j))),
        compiler_params=_cparams("arbitrary", "arbitrary"),
        name="hyena_spectrum",
    )(kernels, kernels, cos, sin, alt, wk)


def _hy_conv_kernel(z_ref, g_ref, cwz_ref, cbz_ref, cwg_ref, cbg_ref, cos_ref, sin_ref, alt_ref,
                    sre_ref, sim_ref, sny_ref, d_ref, o_ref, z_scr, zz_scr, yre_scr, yim_scr, *, conv_z, blk):
    seq = z_ref.shape[1]
    hh = seq // 2
    tc = z_ref.shape[2]
    row = lax.broadcasted_iota(jnp.int32, (blk, 1), 0)

    def short_conv(x_ref, w_ref, b_ref, l0):
        x = x_ref[0, l0:l0 + blk, :]
        w = w_ref[...]
        top = x_ref[0, l0 - 1:l0, :] if l0 > 0 else jnp.zeros((1, tc), F32)
        bot = x_ref[0, l0 + blk:l0 + blk + 1, :] if l0 + blk < seq else jnp.zeros((1, tc), F32)
        prev = jnp.where(row == 0, top, pltpu.roll(x, 1, 0))
        nxt = jnp.where(row == blk - 1, bot, pltpu.roll(x, blk - 1, 0))
        return prev * w[0:1] + x * w[1:2] + nxt * w[2:3] + b_ref[...]

    zny = [jnp.zeros((1, tc), F32), jnp.zeros((1, tc), F32)]
    for l0 in range(0, seq, blk):
        z = short_conv(z_ref, cwz_ref, cbz_ref, l0) if conv_z else z_ref[0, l0:l0 + blk, :]
        a, r0 = divmod(l0, hh)
        z_scr[l0:l0 + blk, :] = z
        zz_scr[r0:r0 + blk, a * tc:(a + 1) * tc] = z.astype(BF16)
        zny[a] = zny[a] + jnp.sum(alt_ref[r0:r0 + blk, :] * z, axis=0, keepdims=True)

    def spec_mul(zc, zs, kern, k0):
        sre = sre_ref[kern, k0:k0 + blk, :]
        sim = sim_ref[kern, k0:k0 + blk, :]
        return zc * sre + zs * sim, zc * sim - zs * sre

    for k0 in range(0, hh, blk):
        zc = jnp.dot(cos_ref[k0:k0 + blk, :], zz_scr[...], preferred_element_type=F32)
        zs = jnp.dot(sin_ref[k0:k0 + blk, :], zz_scr[...], preferred_element_type=F32)
        zc0, zc1, zs0, zs1 = zc[:, :tc], zc[:, tc:], zs[:, :tc], zs[:, tc:]
        for a, (k_first, k_second) in enumerate(((0, 2), (1, 0))):
            re0, im0 = spec_mul(zc0, zs0, k_first, k0)
            re1, im1 = spec_mul(zc1, zs1, k_second, k0)
            yre_scr[k0:k0 + blk, a * tc:(a + 1) * tc] = (re0 + re1).astype(BF16)
            yim_scr[k0:k0 + blk, a * tc:(a + 1) * tc] = (im0 + im1).astype(BF16)
    yny = [sny_ref[0] * zny[0] + sny_ref[2] * zny[1], sny_ref[1] * zny[0] + sny_ref[0] * zny[1]]
    d = d_ref[0]
    for r0 in range(0, hh, blk):
        y = (jnp.dot(cos_ref[r0:r0 + blk, :], yre_scr[...], preferred_element_type=F32)
             - jnp.dot(sin_ref[r0:r0 + blk, :], yim_scr[...], preferred_element_type=F32))
        for a in range(2):
            l0 = a * hh + r0
            ya = y[:, a * tc:(a + 1) * tc] + alt_ref[r0:r0 + blk, :] * yny[a]
            gate = short_conv(g_ref, cwg_ref, cbg_ref, l0)
            o_ref[0, l0:l0 + blk, :] = gate * (ya + d * z_scr[l0:l0 + blk, :])


def _hy_order(z_src, z_off, g_src, g_off, conv_w, conv_b, cos, sin, alt, sre, sim, sny, hy_d, order, width, conv_z):
    b, seq, _ = z_src.shape
    hh = seq // 2
    tc = CONV_TILE
    nb = width // tc
    zo, go = z_off // tc, g_off // tc

    def col(off):
        return lambda j, i: (0, j + off)

    return pl.pallas_call(
        functools.partial(_hy_conv_kernel, conv_z=conv_z, blk=512),
        out_shape=jax.ShapeDtypeStruct((b, seq, width), F32),
        grid=(nb, b),
        in_specs=[pl.BlockSpec((1, seq, tc), lambda j, i: (i, 0, j + zo)),
                  pl.BlockSpec((1, seq, tc), lambda j, i: (i, 0, j + go)),
                  pl.BlockSpec((3, tc), col(zo if conv_z else 0)), pl.BlockSpec((1, tc), col(zo if conv_z else 0)),
                  pl.BlockSpec((3, tc), col(go)), pl.BlockSpec((1, tc), col(go)),
                  pl.BlockSpec((hh, hh), lambda j, i: (0, 0)),
                  pl.BlockSpec((hh, hh), lambda j, i: (0, 0)),
                  pl.BlockSpec((hh, 1), lambda j, i: (0, 0)),
                  pl.BlockSpec((3, hh, tc), lambda j, i: (0, 0, j + order * nb)),
                  pl.BlockSpec((3, hh, tc), lambda j, i: (0, 0, j + order * nb)),
                  pl.BlockSpec((3, 1, tc), lambda j, i: (0, 0, j + order * nb)),
                  pl.BlockSpec((1, 1, tc), lambda j, i: (order, 0, j))],
        out_specs=pl.BlockSpec((1, seq, tc), lambda j, i: (i, 0, j)),
        scratch_shapes=[pltpu.VMEM((seq, tc), F32), pltpu.VMEM((hh, 2 * tc), BF16),
                        pltpu.VMEM((hh, 2 * tc), BF16), pltpu.VMEM((hh, 2 * tc), BF16)],
        compiler_params=_cparams("arbitrary", "arbitrary"),
        name="hyena_conv",
    )(z_src, g_src, conv_w, conv_b, conv_w, conv_b, cos, sin, alt, sre, sim, sny, hy_d.reshape(2, 1, width))


def _hy_conv(proj, conv_w, conv_b, cos, sin, alt, sre, sim, sny, hy_d, width):
    args = (conv_w, conv_b, cos, sin, alt, sre, sim, sny, hy_d)
    z1 = _hy_order(proj, 0, proj, width, *args, order=0, width=width, conv_z=True)
    return _hy_order(z1, 0, proj, 2 * width, *args, order=1, width=width, conv_z=False)


NA_WIN_ROWS = 8
NA_WIN_COLS = 16
NA_QROWS = ROW_TILE // GRID_W
NA_KTILES = 3


def _na_bias_tiles(rpb, rows):
    nqb = rows // NA_QROWS
    assert nqb >= 3 and rows >= NA_WIN_ROWS + NA_QROWS
    col = np.arange(GRID_W)
    col_start = np.clip(col - NA_WIN_COLS // 2, 0, GRID_W - NA_WIN_COLS)
    col_ok = (col[None, :] >= col_start[:, None]) & (col[None, :] < col_start[:, None] + NA_WIN_COLS)
    dc_idx = np.clip(col[None, :] - col[:, None] + NA_WIN_COLS - 1, 0, 2 * NA_WIN_COLS - 2)
    tiles = []
    for j in (0, 1, nqb - 1):
        kb0 = min(max(j - 1, 0), nqb - NA_KTILES)
        q_r = j * NA_QROWS + np.arange(NA_QROWS)
        k_r = kb0 * NA_QROWS + np.arange(NA_KTILES * NA_QROWS)
        r0 = np.clip(q_r - NA_WIN_ROWS // 2, 0, rows - NA_WIN_ROWS)
        row_ok = (k_r[None, :] >= r0[:, None]) & (k_r[None, :] < r0[:, None] + NA_WIN_ROWS)
        dr_idx = np.clip(k_r[None, :] - q_r[:, None] + NA_WIN_ROWS - 1, 0, 2 * NA_WIN_ROWS - 2)
        ok = row_ok[:, None, :, None] & col_ok[None, :, None, :]
        pick_r = jnp.asarray(dr_idx[:, :, None] == np.arange(2 * NA_WIN_ROWS - 1), F32)
        pick_c = jnp.asarray(dc_idx[:, :, None] == np.arange(2 * NA_WIN_COLS - 1), F32)
        by_row = jnp.einsum('hij,abi->habj', rpb.astype(F32), pick_r, precision=HIGHEST)
        bias = jnp.einsum('habj,cdj->hacbd', by_row, pick_c, precision=HIGHEST)
        tile = jnp.where(jnp.asarray(ok)[None], bias, -jnp.inf)
        tiles.append(tile.reshape(rpb.shape[0], ROW_TILE, NA_KTILES * ROW_TILE))
    return jnp.stack(tiles, axis=1)


def _na_kernel(q_ref, k0_ref, k1_ref, k2_ref, v0_ref, v1_ref, v2_ref, kc_ref, vc_ref, bias_ref, o_ref, *, n_heads):
    nt = (((1,), (1,)), ((), ()))
    q = (q_ref[0] * (HEAD_DIM ** -0.5)).astype(BF16)
    k = jnp.concatenate([k0_ref[0], k1_ref[0], k2_ref[0]], axis=0).astype(BF16)
    v = jnp.concatenate([v0_ref[0], v1_ref[0], v2_ref[0]], axis=0).astype(BF16)
    kc = kc_ref[0].astype(BF16)
    vc = vc_ref[0].astype(BF16)
    for h in range(n_heads):
        sl = slice(h * HEAD_DIM, (h + 1) * HEAD_DIM)
        s_loc = lax.dot_general(q[:, sl], k[:, sl], nt, preferred_element_type=F32) + bias_ref[h, 0]
        s_ctx = lax.dot_general(q[:, sl], kc[:, sl], nt, preferred_element_type=F32)
        m = jnp.maximum(jnp.max(s_loc, axis=-1, keepdims=True), jnp.max(s_ctx, axis=-1, keepdims=True))
        p_loc = jnp.exp(s_loc - m)
        p_ctx = jnp.exp(s_ctx - m)
        l = jnp.sum(p_loc, axis=-1, keepdims=True) + jnp.sum(p_ctx, axis=-1, keepdims=True)
        o = (jnp.dot(p_loc.astype(BF16), v[:, sl], preferred_element_type=F32)
             + jnp.dot(p_ctx.astype(BF16), vc[:, sl], preferred_element_type=F32)) / l
        o_ref[0, :, sl] = o.astype(o_ref.dtype)


def _na_attention(proj, proj_c, rpb, *, q_off, n_heads):
    b, seq, _ = proj.shape
    w = n_heads * HEAD_DIM
    tm = ROW_TILE
    nqb = seq // tm
    bias = _na_bias_tiles(rpb, seq // GRID_W)
    qc = q_off // w

    def kb0(j):
        return jnp.clip(j - 1, 0, nqb - NA_KTILES)

    def kv_spec(cb, off):
        return pl.BlockSpec((1, tm, w), lambda i, j: (i, kb0(j) + off, cb))

    def bias_type(j):
        return jnp.where(j == 0, 0, jnp.where(j == nqb - 1, 2, 1))

    return pl.pallas_call(
        functools.partial(_na_kernel, n_heads=n_heads),
        out_shape=jax.ShapeDtypeStruct((b, seq, w), BF16),
        grid=(b, nqb),
        in_specs=[pl.BlockSpec((1, tm, w), lambda i, j: (i, j, qc)),
                  kv_spec(qc + 1, 0), kv_spec(qc + 1, 1), kv_spec(qc + 1, 2),
                  kv_spec(qc + 2, 0), kv_spec(qc + 2, 1), kv_spec(qc + 2, 2),
                  pl.BlockSpec((1, proj_c.shape[1], w), lambda i, j: (i, 0, 0)),
                  pl.BlockSpec((1, proj_c.shape[1], w), lambda i, j: (i, 0, 1)),
                  pl.BlockSpec((n_heads, 1, tm, NA_KTILES * tm), lambda i, j: (0, bias_type(j), 0, 0))],
        out_specs=pl.BlockSpec((1, tm, w), lambda i, j: (i, j, 0)),
        compiler_params=_cparams("parallel", "arbitrary"),
        name="na_attention",
    )(proj, proj, proj, proj, proj, proj, proj, proj_c, proj_c, bias)


MOE_TOP_K = 2


def _mix_odd_kernel(x_ref, hy_ref, na_ref, modx_ref, g_ref, wa_ref, wb_ref, rt_ref, x1_ref, h2_ref, idx_ref, wt_ref,
                    *, n_e):
    bb, tm, d = x_ref.shape
    mod = modx_ref[...]
    ox = (_bdot(hy_ref[...].reshape(bb * tm, -1), wa_ref[...])
          + jnp.dot(na_ref[...].reshape(bb * tm, -1), wb_ref[...], preferred_element_type=F32))
    x1 = x_ref[...] + mod[:, 2:3, :] * ox.reshape(bb, tm, d)
    x1_ref[...] = x1
    h2 = _norm_mod(x1, g_ref[...], mod[:, 3:4, :], mod[:, 4:5, :])
    h2_ref[...] = _rows_to_tiles(h2.reshape(bb * tm, d)).reshape(h2_ref.shape)
    h_hi, h_lo = _split_bf16(h2.reshape(bb * tm, d))
    lg = (jnp.dot(h_hi, rt_ref[0], preferred_element_type=F32) + jnp.dot(h_lo, rt_ref[0], preferred_element_type=F32)
          + jnp.dot(h_hi, rt_ref[1], preferred_element_type=F32))
    eid = lax.broadcasted_iota(jnp.int32, lg.shape, 1)
    lg = jnp.where(eid < n_e, lg, -jnp.inf)
    m1 = jnp.max(lg, axis=1, keepdims=True)
    i1 = jnp.min(jnp.where(lg == m1, eid, n_e), axis=1, keepdims=True)
    lg2 = jnp.where(eid == i1, -jnp.inf, lg)
    m2 = jnp.max(lg2, axis=1, keepdims=True)
    i2 = jnp.min(jnp.where(lg2 == m2, eid, n_e), axis=1, keepdims=True)
    e2 = jnp.exp(m2 - m1)
    den = 1.0 + e2
    first = lax.broadcasted_iota(jnp.int32, (bb * tm, MOE_TOP_K), 1) == 0
    idx_ref[...] = jnp.where(first, i1, i2).reshape(bb, tm, MOE_TOP_K)
    wt_ref[...] = jnp.where(first, 1.0 / den, e2 / den).reshape(bb, tm, MOE_TOP_K)


ROUTER_LANES = 128


def _mix_odd(stream, t_off, o_hy, o_na, modx, g, w_out, router, bb=2):
    b, _, d = stream.shape
    seq = o_hy.shape[1]
    wh = o_hy.shape[2]
    wn = o_na.shape[2]
    n_e = router.shape[1]
    tm = ROW_TILE
    n_t = seq // tm
    router_pad = jnp.concatenate([router.astype(F32), jnp.zeros((d, ROUTER_LANES - n_e), F32)], axis=1)
    router_pad = jnp.stack(_split_bf16(router_pad))
    return pl.pallas_call(
        functools.partial(_mix_odd_kernel, n_e=n_e),
        out_shape=(jax.ShapeDtypeStruct((b, seq, d), F32), jax.ShapeDtypeStruct((b, seq) + _tile_shape(d), F32),
                   jax.ShapeDtypeStruct((b, seq, MOE_TOP_K), jnp.int32), jax.ShapeDtypeStruct((b, seq, MOE_TOP_K), F32)),
        grid=(b // bb, n_t),
        in_specs=[pl.BlockSpec((bb, tm, d), lambda i, t: (i, t + t_off, 0)),
                  pl.BlockSpec((bb, tm, wh), lambda i, t: (i, t, 0)),
                  pl.BlockSpec((bb, tm, wn), lambda i, t: (i, t, 0)),
                  pl.BlockSpec((bb, 6, d), lambda i, t: (i, 0, 0)),
                  pl.BlockSpec((1, d), lambda i, t: (0, 0)),
                  pl.BlockSpec((wh, d), lambda i, t: (0, 0)),
                  pl.BlockSpec((wn, d), lambda i, t: (1, 0)),
                  pl.BlockSpec((2, d, ROUTER_LANES), lambda i, t: (0, 0, 0))],
        out_specs=(pl.BlockSpec((bb, tm, d), lambda i, t: (i, t, 0)),
                   pl.BlockSpec((bb, tm) + _tile_shape(d), lambda i, t: (i, t, 0, 0)),
                   pl.BlockSpec((bb, tm, MOE_TOP_K), lambda i, t: (i, t, 0)),
                   pl.BlockSpec((bb, tm, MOE_TOP_K), lambda i, t: (i, t, 0))),
        compiler_params=_cparams("parallel", "arbitrary"),
        name="mix_odd_router",
    )(stream, o_hy, o_na, modx, g.reshape(1, d), w_out, w_out, router_pad)


MOE_BLOCK = 1024
MOE_FF_TILE = 896


def _moe_plan(idx, n_experts):
    t, k = idx.shape
    e_flat = idx.reshape(-1)
    onehot = (e_flat[:, None] == jnp.arange(n_experts)[None, :]).astype(jnp.int32)
    rank = jnp.sum((jnp.cumsum(onehot, axis=0) - onehot) * onehot, axis=1)
    counts = jnp.sum(onehot, axis=0)
    padded = (counts + MOE_BLOCK - 1) // MOE_BLOCK * MOE_BLOCK
    pend = jnp.cumsum(padded)
    pstart = pend - padded
    slot = pstart[e_flat] + rank
    n_blocks = (t * k) // MOE_BLOCK + n_experts
    block_e = jnp.clip(jnp.searchsorted(pend, jnp.arange(n_blocks) * MOE_BLOCK, side='right'), 0, n_experts - 1)
    order = jnp.argsort(e_flat, stable=True).astype(jnp.int32)
    pad_before = pstart - (jnp.cumsum(counts) - counts)
    src = jnp.arange(n_blocks * MOE_BLOCK, dtype=jnp.int32) - jnp.repeat(pad_before[block_e], MOE_BLOCK)
    slot_tok = order[jnp.clip(src, 0, t * k - 1)] // k
    n_used = (pend[-1] // MOE_BLOCK).astype(jnp.int32).reshape(1)
    return (slot_tok.reshape(n_blocks, 1, MOE_BLOCK), block_e.astype(jnp.int32), n_used,
            slot.reshape(t, k).astype(jnp.int32))


def _moe_ffn_kernel(be_ref, nu_ref, tok_ref, nxt_ref, h_hbm, w1_ref, w3_ref, w2_ref, y_ref, xg_scr, xb_scr, acc_scr,
                    sem, *, n_j):
    i = pl.program_id(0)
    j = pl.program_id(1)
    used = i < nu_ref[0]
    rows = xb_scr.shape[0]
    per_step = rows // n_j
    cur = i % 2
    nxt = 1 - cur

    def row_copy(idx_ref, r, buf):
        return pltpu.make_async_copy(h_hbm.at[pl.ds(idx_ref[0, 0, r], 1)], xg_scr.at[buf, pl.ds(r, 1)], sem.at[buf])

    def wait_rows(buf):
        pltpu.make_async_copy(h_hbm.at[pl.ds(0, rows)], xg_scr.at[buf], sem.at[buf]).wait()

    @pl.when(jnp.logical_and(i == 0, j == 0))
    def _():
        def start(r, carry):
            row_copy(tok_ref, r, 0).start()
            return carry
        lax.fori_loop(0, rows, start, 0)

    @pl.when(j == 0)
    def _():
        wait_rows(cur)
        xb_scr[...] = _tiles_to_rows(xg_scr[cur]).astype(BF16)
        acc_scr[...] = jnp.zeros_like(acc_scr)

    def prefetch_rows():
        base = j * per_step
        for k in range(per_step):
            row_copy(nxt_ref, base + k, nxt).start(priority=k % 2)

    @pl.when(used)
    def _():
        xb = xb_scr[...]
        a = jnp.dot(xb, w1_ref[0], preferred_element_type=F32)
        g = jnp.dot(xb, w3_ref[0], preferred_element_type=F32)
        acc_scr[...] += _bdot(_silu(a) * g, w2_ref[0])
        prefetch_rows()

    @pl.when(jnp.logical_not(used))
    def _():
        prefetch_rows()

    @pl.when(j == n_j - 1)
    def _():
        y_ref[...] = _rows_to_tiles(acc_scr[...])

    @pl.when(jnp.logical_and(i == pl.num_programs(0) - 1, j == n_j - 1))
    def _():
        wait_rows(nxt)


def _moe_ffn(h2, slot_tok, block_e, n_used, w1, w3, w2):
    tile = h2.shape[1:]
    d = tile[0] * tile[1]
    n_blocks = slot_tok.shape[0]
    f = w1.shape[2]
    mb = MOE_BLOCK
    tf = MOE_FF_TILE
    n_j = f // tf
    grid_spec = pltpu.PrefetchScalarGridSpec(
        num_scalar_prefetch=2,
        grid=(n_blocks, n_j),
        in_specs=[pl.BlockSpec((1, 1, mb), lambda i, j, be, nu: (i, 0, 0), memory_space=pltpu.SMEM),
                  pl.BlockSpec((1, 1, mb), lambda i, j, be, nu: (jnp.minimum(i + 1, n_blocks - 1), 0, 0),
                               memory_space=pltpu.SMEM),
                  pl.BlockSpec(memory_space=pl.ANY),
                  pl.BlockSpec((1, d, tf), lambda i, j, be, nu: (be[i], 0, j)),
                  pl.BlockSpec((1, d, tf), lambda i, j, be, nu: (be[i], 0, j)),
                  pl.BlockSpec((1, tf, d), lambda i, j, be, nu: (be[i], j, 0))],
        out_specs=pl.BlockSpec((mb,) + tile, lambda i, j, be, nu: (i, 0, 0)),
        scratch_shapes=[pltpu.VMEM((2, mb) + tile, F32), pltpu.VMEM((mb, d), BF16), pltpu.VMEM((mb, d), F32),
                        pltpu.SemaphoreType.DMA((2,))],
    )
    return pl.pallas_call(
        functools.partial(_moe_ffn_kernel, n_j=n_j),
        out_shape=jax.ShapeDtypeStruct((n_blocks * mb,) + tile, F32),
        grid_spec=grid_spec,
        compiler_params=_cparams("arbitrary", "arbitrary"),
        name="moe_expert_ffn",
    )(block_e, n_used, slot_tok, slot_tok, h2, w1, w3, w2)


def _moe_combine_kernel(slot_ref, nxt_ref, y_hbm, x1_ref, wt_ref, modx_ref, g_ref, o_ref, y_scr, sem):
    tm = x1_ref.shape[1]
    n_t = pl.num_programs(1)
    step = pl.program_id(0) * n_t + pl.program_id(1)
    n_steps = pl.num_programs(0) * n_t
    cur = step % 2
    nxt = 1 - cur

    def row_copy(idx_ref, r, buf):
        return pltpu.make_async_copy(y_hbm.at[pl.ds(idx_ref[0, 0, r], 1)], y_scr.at[buf, pl.ds(r, 1)], sem.at[buf])

    def wait_rows(buf):
        pltpu.make_async_copy(y_hbm.at[pl.ds(0, MOE_TOP_K * tm)], y_scr.at[buf], sem.at[buf]).wait()

    @pl.when(step == 0)
    def _():
        def start(r, carry):
            row_copy(slot_ref, r, 0).start()
            return carry
        lax.fori_loop(0, MOE_TOP_K * tm, start, 0)

    wait_rows(cur)
    for r in range(MOE_TOP_K * tm):
        row_copy(nxt_ref, r, nxt).start(priority=r % 2)
    wt = wt_ref[...]
    y = wt[:, 0:1] * _tiles_to_rows(y_scr[cur, 0:tm]) + wt[:, 1:2] * _tiles_to_rows(y_scr[cur, tm:2 * tm])
    x2 = x1_ref[0] + modx_ref[0, 5:6, :] * y
    o_ref[0] = (x2 * lax.rsqrt(jnp.mean(x2 * x2, axis=-1, keepdims=True) + EPS)) * g_ref[...]

    @pl.when(step == n_steps - 1)
    def _():
        wait_rows(nxt)


def _moe_combine(y_slots, slot_of, wt, x1, modx, final_g):
    b, seq, d = x1.shape
    tm = ROW_TILE
    n_t = seq // tm
    n_steps = b * n_t
    slots = slot_of.reshape(n_steps, tm, MOE_TOP_K).transpose(0, 2, 1).reshape(n_steps, 1, MOE_TOP_K * tm)
    return pl.pallas_call(
        _moe_combine_kernel,
        out_shape=jax.ShapeDtypeStruct((b, seq, d), F32),
        grid=(b, n_t),
        in_specs=[pl.BlockSpec((1, 1, MOE_TOP_K * tm), lambda i, t: (i * n_t + t, 0, 0), memory_space=pltpu.SMEM),
                  pl.BlockSpec((1, 1, MOE_TOP_K * tm), lambda i, t: (jnp.minimum(i * n_t + t + 1, n_steps - 1), 0, 0),
                               memory_space=pltpu.SMEM),
                  pl.BlockSpec(memory_space=pl.ANY),
                  pl.BlockSpec((1, tm, d), lambda i, t: (i, t, 0)),
                  pl.BlockSpec((tm, MOE_TOP_K), lambda i, t: (i * n_t + t, 0)),
                  pl.BlockSpec((1, 6, d), lambda i, t: (i, 0, 0)),
                  pl.BlockSpec((1, d), lambda i, t: (0, 0))],
        out_specs=pl.BlockSpec((1, tm, d), lambda i, t: (i, t, 0)),
        scratch_shapes=[pltpu.VMEM((2, MOE_TOP_K * tm) + y_slots.shape[1:], F32), pltpu.SemaphoreType.DMA((2,))],
        compiler_params=_cparams("arbitrary", "arbitrary"),
        name="moe_combine_norm",
    )(slots, slots, y_slots, x1, wt, modx, final_g.reshape(1, d))


def _odd_layer(stream, modx, modc, n_ctx_tok, norm1_g, norm2_g, w_in, w_out, conv_w, conv_b, filt, hy_d, rpb,
               router, moe_w1, moe_w3, moe_w2, final_g):
    b, s, d = stream.shape
    seq = s - n_ctx_tok
    hy_w = hy_d.shape[1]
    n_heads = rpb.shape[0]
    na_w = n_heads * HEAD_DIM
    t_off = n_ctx_tok // ROW_TILE
    w_in_b = w_in.astype(BF16)
    proj = _in_proj((stream,), modx, modc, norm1_g, w_in_b, t_off=t_off, n_t=seq // ROW_TILE, ctx_rule=_never_ctx)
    proj_c = _in_proj((stream,), modx, modc, norm1_g, w_in_b[:, 3 * hy_w + na_w:], t_off=0, n_t=t_off,
                      ctx_rule=_always_ctx)
    cos, sin, alt = _dft_tables(seq // 2)
    sre, sim, sny = _hy_spectrum(_hy_block_kernels(_hy_filters(seq, *filt)), cos, sin, alt)
    o_hy = _hy_conv(proj, conv_w.astype(F32), conv_b.reshape(1, -1).astype(F32), cos, sin, alt, sre, sim, sny,
                    hy_d.astype(F32), hy_w)
    o_na = _na_attention(proj, proj_c, rpb, q_off=3 * hy_w, n_heads=n_heads)
    x1, h2, idx, wt = _mix_odd(stream, t_off, o_hy, o_na, modx, norm2_g, w_out.astype(BF16), router)
    slot_tok, block_e, n_used, slot_of = _moe_plan(idx.reshape(b * seq, MOE_TOP_K), router.shape[1])
    y_slots = _moe_ffn(h2.reshape((b * seq,) + h2.shape[2:]), slot_tok, block_e, n_used,
                       moe_w1.astype(BF16), moe_w3.astype(BF16), moe_w2.astype(BF16))
    return _moe_combine(y_slots, slot_of, wt.reshape(b * seq, MOE_TOP_K), x1, modx, final_g)


def kernel(x, c, ctx, c_ctx, mod_w, mod_b, norm1_g, norm2_g, ev_w_in, ev_w_out, s5_lam_re, s5_lam_im, s5_log_dt, s5_b_re, s5_b_im, s5_c_re, s5_c_im, s5_d, s5_w_glu, gqa_q_g, gqa_k_g, ffn_w1, ffn_w3, ffn_w2, od_w_in, od_w_out, hy_conv_w, hy_conv_b, hy_w1, hy_b1, hy_w2, hy_b2, hy_w3, hy_freq, hy_decay, hy_d, na_rpb, moe_router, moe_w1, moe_w3, moe_w2, final_g):
    b, seq, d = x.shape
    n_ctx_tok = ctx.shape[1]
    assert n_ctx_tok == ROW_TILE and seq % ROW_TILE == 0
    rows = 8 * ((b + 1 + 7) // 8)
    cvec = jnp.zeros((rows, d), F32).at[:b].set(c).at[b].set(c_ctx)
    m = _modulation(cvec, mod_w, mod_b)
    modx = [m[l, :b].reshape(b, 6, d) for l in range(2)]
    modc = [m[l, b].reshape(1, 6, d) for l in range(2)]
    s5p = (s5_lam_re[0], s5_lam_im[0], s5_log_dt[0], s5_b_re[0], s5_b_im[0], s5_c_re[0], s5_c_im[0], s5_d[0])
    stream = _even_layer((ctx, x), modx[0], modc[0], n_ctx_tok, norm1_g[0], norm2_g[0], ev_w_in[0], ev_w_out[0], s5p,
                         s5_w_glu[0], gqa_q_g[0], gqa_k_g[0], ffn_w1[0], ffn_w3[0], ffn_w2[0])
    filt = (hy_w1[0], hy_b1[0], hy_w2[0], hy_b2[0], hy_w3[0], hy_freq[0], hy_decay[0])
    return _odd_layer(stream, modx[1], modc[1], n_ctx_tok, norm1_g[1], norm2_g[1], od_w_in[0], od_w_out[0],
                      hy_conv_w[0], hy_conv_b[0], filt, hy_d[0], na_rpb[0], moe_router[0],
                      moe_w1[0], moe_w3[0], moe_w2[0], final_g)
```

```python
import functools
import math

import jax
import jax.numpy as jnp
import numpy as np
from jax import lax
from jax.experimental import pallas as pl
from jax.experimental.pallas import tpu as pltpu

F32 = jnp.float32
BF16 = jnp.bfloat16
EPS = 1e-6
HEAD_DIM = 64
GRID_W = 64
ROPE_FREQS = HEAD_DIM // 4
ROPE_BASE = 10000.0
ROW_TILE = 256
S5_GROUP_CH = 16
S5_STATE = 64
S5_CHUNK = 16
VMEM_LIMIT = 56 * 1024 * 1024
HIGHEST = lax.Precision.HIGHEST


def _cparams(*sem):
    return pltpu.CompilerParams(dimension_semantics=sem, vmem_limit_bytes=VMEM_LIMIT)


def _bdot(a, b):
    return jnp.dot(a.astype(BF16), b.astype(BF16), preferred_element_type=F32)


def _silu(a):
    return a * jax.nn.sigmoid(a)


def _norm_mod(x, g, shift, scale):
    y = x * lax.rsqrt(jnp.mean(x * x, axis=-1, keepdims=True) + EPS)
    return (y * g) * (1.0 + scale) + shift


SUBLANES = 8
LANES = 128


def _tile_shape(d):
    assert d % (SUBLANES * LANES) == 0
    return (d // LANES, LANES)


def _rows_to_tiles(a):
    return a.reshape(a.shape[0], a.shape[1] // LANES, LANES)


def _tiles_to_rows(a):
    return a.reshape(a.shape[0], a.shape[1] * a.shape[2])


def _pick_mod(modx_ref, modc_ref, is_ctx):
    return jnp.where(is_ctx, modc_ref[...], modx_ref[...])


def _mod_kernel(c_ref, w_ref, b_ref, o_ref):
    s = _silu(c_ref[...])
    o_ref[0] = jnp.dot(s, w_ref[0], preferred_element_type=F32, precision=HIGHEST) + b_ref[0]


def _modulation(cvec, mod_w, mod_b):
    depth, d, n = mod_w.shape
    rows = cvec.shape[0]
    tn = 1024
    return pl.pallas_call(
        _mod_kernel,
        out_shape=jax.ShapeDtypeStruct((depth, rows, n), F32),
        grid=(depth, n // tn),
        in_specs=[pl.BlockSpec((rows, d), lambda l, j: (0, 0)),
                  pl.BlockSpec((1, d, tn), lambda l, j: (l, 0, j)),
                  pl.BlockSpec((1, 1, tn), lambda l, j: (l, 0, j))],
        out_specs=pl.BlockSpec((1, rows, tn), lambda l, j: (l, 0, j)),
        compiler_params=_cparams("arbitrary", "arbitrary"),
        name="adaln_mod",
    )(cvec, mod_w, mod_b.reshape(depth, 1, n))


def _stream_specs(srcs, bb, t_off=0):
    d = srcs[0].shape[2]
    if len(srcs) == 1:
        return [pl.BlockSpec((bb, ROW_TILE, d), lambda i, t, *_: (i, t + t_off, 0))]
    return [pl.BlockSpec((bb, ROW_TILE, d), lambda i, t, *_: (i, 0, 0)),
            pl.BlockSpec((bb, ROW_TILE, d), lambda i, t, *_: (i, jnp.maximum(t - 1, 0), 0))]


def _stream_tile(refs, is_ctx):
    if len(refs) == 1:
        return refs[0][...]
    return jnp.where(is_ctx, refs[0][...], refs[1][...])


def _in_kernel(*refs, ctx_rule, n_src):
    src, (modx_ref, modc_ref, g_ref, w_ref, o_ref) = refs[:n_src], refs[n_src:]
    bb, tm, d = src[0].shape
    is_ctx = ctx_rule(pl.program_id(1))
    mod = _pick_mod(modx_ref, modc_ref, is_ctx)
    h = _norm_mod(_stream_tile(src, is_ctx), g_ref[...], mod[:, 0:1, :], mod[:, 1:2, :])
    o = _bdot(h.reshape(bb * tm, d), w_ref[...])
    o_ref[...] = o.reshape(bb, tm, -1).astype(o_ref.dtype)


def _in_proj(srcs, modx, modc, g, w, *, t_off, n_t, ctx_rule, bb=2):
    b, _, d = srcs[0].shape
    n = w.shape[1]
    return pl.pallas_call(
        functools.partial(_in_kernel, ctx_rule=ctx_rule, n_src=len(srcs)),
        out_shape=jax.ShapeDtypeStruct((b, n_t * ROW_TILE, n), F32),
        grid=(b // bb, n_t),
        in_specs=_stream_specs(srcs, bb, t_off) + [
            pl.BlockSpec((bb, 6, d), lambda i, t: (i, 0, 0)),
            pl.BlockSpec((1, 6, d), lambda i, t: (0, 0, 0)),
            pl.BlockSpec((1, d), lambda i, t: (0, 0)),
            pl.BlockSpec((d, n), lambda i, t: (0, 0))],
        out_specs=pl.BlockSpec((bb, ROW_TILE, n), lambda i, t: (i, t, 0)),
        compiler_params=_cparams("parallel", "arbitrary"),
        name="in_proj",
    )(*srcs, modx, modc, g.reshape(1, d), w)


def _never_ctx(t):
    return t < 0


def _always_ctx(t):
    return t >= 0


def _s5_tables(lam_re, lam_im, log_dt, b_re, b_im, c_re, c_im, d_skip):
    t_len, n_st, p_ch = S5_CHUNK, S5_STATE, S5_GROUP_CH
    groups = lam_re.shape[1]
    dt = jnp.exp(log_dt.astype(F32))[..., None]
    lr = lam_re.astype(F32)
    li = lam_im.astype(F32)
    mag = jnp.exp(lr * dt)
    ar = mag * jnp.cos(li * dt)
    ai = mag * jnp.sin(li * dt)
    nr = ar - 1.0
    den = lr * lr + li * li
    kr = (nr * lr + ai * li) / den
    ki = (ai * lr - nr * li) / den
    br = b_re.astype(F32)
    bi = b_im.astype(F32)
    bbr = kr[..., None] * br - ki[..., None] * bi
    bbi = kr[..., None] * bi + ki[..., None] * br
    cr = c_re.astype(F32)
    ci = c_im.astype(F32)
    pr = [jnp.ones_like(ar)]
    pi = [jnp.zeros_like(ai)]
    for _ in range(t_len):
        pr.append(pr[-1] * ar - pi[-1] * ai)
        pi.append(pr[-2] * ai + pi[-1] * ar)
    pr = jnp.stack(pr)
    pi = jnp.stack(pi)
    er = cr[None] * pr[:, :, :, None, :] - ci[None] * pi[:, :, :, None, :]
    ei = cr[None] * pi[:, :, :, None, :] + ci[None] * pr[:, :, :, None, :]
    kern = (jnp.einsum('jdgqn,dgnp->jdgqp', er, bbr, precision=HIGHEST)
            - jnp.einsum('jdgqn,dgnp->jdgqp', ei, bbi, precision=HIGHEST))
    s_idx = np.arange(t_len)[:, None]
    t_idx = np.arange(t_len)[None, :]
    lag_f = np.clip(t_idx - s_idx, 0, t_len - 1)
    lag_b = np.clip(s_idx - t_idx, 0, t_len - 1)
    kf = kern[:, 0][lag_f]
    kb = kern[:, 1][lag_b]
    mask_f = jnp.asarray(s_idx <= t_idx, F32)[:, :, None, None, None]
    mask_b = jnp.asarray(s_idx >= t_idx, F32)[:, :, None, None, None]
    dmat = jnp.eye(p_ch, dtype=F32)[None] * d_skip.astype(F32).reshape(groups, 1, p_ch)
    eye_t = jnp.asarray(s_idx == t_idx, F32)[:, :, None, None, None]
    full = kf * mask_f + kb * mask_b + eye_t * dmat[None, None]
    toep = full.transpose(2, 0, 4, 1, 3).reshape(groups, t_len * p_ch, t_len * p_ch)
    def drive(pw_r, pw_i, d):
        re = pw_r[..., None] * bbr[d][None] - pw_i[..., None] * bbi[d][None]
        im = pw_r[..., None] * bbi[d][None] + pw_i[..., None] * bbr[d][None]
        return re.transpose(1, 0, 3, 2), im.transpose(1, 0, 3, 2)
    f_re, f_im = drive(pr[:t_len, 0][::-1], pi[:t_len, 0][::-1], 0)
    b_re2, b_im2 = drive(pr[:t_len, 1], pi[:t_len, 1], 1)
    wst = jnp.concatenate([f_re, f_im, b_re2, b_im2], axis=-1).reshape(groups, t_len * p_ch, 4 * n_st)
    def read(e_r, e_i):
        return e_r.transpose(1, 3, 0, 2), (-e_i).transpose(1, 3, 0, 2)
    of_re, of_im = read(er[1:, 0], ei[1:, 0])
    ob_re, ob_im = read(er[1:, 1][::-1], ei[1:, 1][::-1])
    wout = jnp.concatenate([of_re, of_im, ob_re, ob_im], axis=1).reshape(groups, 4 * n_st, t_len * p_ch)
    a_r = pr[t_len]
    a_i = pi[t_len]
    adec = jnp.stack([jnp.concatenate([a_r[0], a_r[0]], -1), jnp.concatenate([-a_i[0], a_i[0]], -1),
                      jnp.concatenate([a_r[1], a_r[1]], -1), jnp.concatenate([-a_i[1], a_i[1]], -1)], axis=1)
    return toep.astype(BF16), wst.astype(BF16), wout.astype(BF16), adec


def _s5_kernel(u_ref, toep_ref, wst_ref, wout_ref, a_ref, y_ref, s_scr, h_scr, y_scr, *, nb, n_ctx, n_chunks,
               rows_blk):
    rows = u_ref.shape[1]
    n2 = 2 * S5_STATE
    toep = toep_ref[0]
    wst = wst_ref[0]
    for r0 in range(0, rows, rows_blk):
        u = u_ref[0, r0:r0 + rows_blk, :].astype(BF16)
        y_scr[r0:r0 + rows_blk, :] = jnp.dot(u, toep, preferred_element_type=F32)
        s_scr[r0:r0 + rows_blk, :] = jnp.dot(u, wst, preferred_element_type=F32)
    a = a_ref[0]
    af1, af2, ab1, ab2 = a[0:1], a[1:2], a[2:3], a[3:4]

    def step(i, carry):
        hf, hb = carry
        cb = jnp.where(i < n_ctx, n_ctx - 1 - i, n_chunks - 1 - (i - n_ctx))
        rf = pl.multiple_of(i * nb, nb)
        rb = pl.multiple_of(cb * nb, nb)
        h_scr[pl.ds(rf, nb), 0:n2] = hf
        h_scr[pl.ds(rb, nb), n2:2 * n2] = hb
        sf = s_scr[pl.ds(rf, nb), 0:n2]
        sb = s_scr[pl.ds(rb, nb), n2:2 * n2]
        hf = af1 * hf + af2 * pltpu.roll(hf, S5_STATE, 1) + sf
        hb = ab1 * hb + ab2 * pltpu.roll(hb, S5_STATE, 1) + sb
        return hf, hb

    zero = jnp.zeros((nb, n2), F32)
    lax.fori_loop(0, n_chunks, step, (zero, zero))
    wout = wout_ref[0]
    for r0 in range(0, rows, rows_blk):
        h = h_scr[r0:r0 + rows_blk, :].astype(BF16)
        y = y_scr[r0:r0 + rows_blk, :] + jnp.dot(h, wout, preferred_element_type=F32)
        y_ref[0, r0:r0 + rows_blk, :] = y.astype(y_ref.dtype)


def _s5_scan(u, tables, n_ctx_tok):
    toep, wst, wout, adec = tables
    b, s, width = u.shape
    groups = width // S5_GROUP_CH
    n_chunks = s // S5_CHUNK
    cw = S5_CHUNK * S5_GROUP_CH
    rows = n_chunks * b
    ug = u.reshape(b, n_chunks, S5_CHUNK, groups, S5_GROUP_CH).transpose(3, 1, 0, 2, 4).reshape(groups, rows, cw)
    rows_blk = math.gcd(rows, 512)
    y = pl.pallas_call(
        functools.partial(_s5_kernel, nb=b, n_ctx=n_ctx_tok // S5_CHUNK, n_chunks=n_chunks, rows_blk=rows_blk),
        out_shape=jax.ShapeDtypeStruct((groups, rows, cw), BF16),
        grid=(groups,),
        in_specs=[pl.BlockSpec((1, rows, cw), lambda g: (g, 0, 0)),
                  pl.BlockSpec((1, cw, cw), lambda g: (g, 0, 0)),
                  pl.BlockSpec((1, cw, 4 * S5_STATE), lambda g: (g, 0, 0)),
                  pl.BlockSpec((1, 4 * S5_STATE, cw), lambda g: (g, 0, 0)),
                  pl.BlockSpec((1, 4, 2 * S5_STATE), lambda g: (g, 0, 0))],
        out_specs=pl.BlockSpec((1, rows, cw), lambda g: (g, 0, 0)),
        scratch_shapes=[pltpu.VMEM((rows, 4 * S5_STATE), F32), pltpu.VMEM((rows, 4 * S5_STATE), F32),
                        pltpu.VMEM((rows, cw), F32)],
        compiler_params=_cparams("parallel"),
        name="s5_scan",
    )(ug, toep, wst, wout, adec)
    return y.reshape(groups, n_chunks, b, S5_CHUNK, S5_GROUP_CH).transpose(2, 1, 3, 0, 4).reshape(b, s, width)


def _head_norm(z, gain, avg):
    sq = z * z
    hi = sq.astype(BF16)
    lo = (sq - hi.astype(F32)).astype(BF16)
    w = avg[:z.shape[1], :z.shape[1]]
    ms = jnp.dot(hi, w, preferred_element_type=F32) + jnp.dot(lo, w, preferred_element_type=F32)
    return z * lax.rsqrt(ms + EPS) * gain


def _rope(z, cos, sin):
    lane = lax.broadcasted_iota(jnp.int32, z.shape, 1)
    first = (lane % (2 * ROPE_FREQS)) < ROPE_FREQS
    width = z.shape[1]
    partner = jnp.where(first, pltpu.roll(z, width - ROPE_FREQS, 1), pltpu.roll(z, ROPE_FREQS, 1))
    return z * cos + partner * sin


def _in_qkv_kernel(*refs, n_src, s5_w, n_q, n_kv):
    src = refs[:n_src]
    (modx_ref, modc_ref, g_ref, w_ref, cos_ref, sin_ref, qg_ref, kg_ref, avg_ref,
     u_ref, q_ref, k_ref, v_ref) = refs[n_src:]
    bb, tm, d = src[0].shape
    dq = n_q * HEAD_DIM
    dk = n_kv * HEAD_DIM
    is_ctx = pl.program_id(1) == 0
    mod = _pick_mod(modx_ref, modc_ref, is_ctx)
    h = _norm_mod(_stream_tile(src, is_ctx), g_ref[...], mod[:, 0:1, :], mod[:, 1:2, :])
    o = _bdot(h.reshape(bb * tm, d), w_ref[...])
    u_ref[...] = o[:, 0:s5_w].reshape(bb, tm, s5_w).astype(u_ref.dtype)
    avg = avg_ref[...]
    for bi in range(bb):
        p = o[bi * tm:(bi + 1) * tm]
        q = _rope(_head_norm(p[:, s5_w:s5_w + dq], qg_ref[...], avg), cos_ref[...], sin_ref[...])
        k = _rope(_head_norm(p[:, s5_w + dq:s5_w + dq + dk], kg_ref[:, :dk], avg), cos_ref[:, :dk], sin_ref[:, :dk])
        v = p[:, s5_w + dq + dk:s5_w + dq + 2 * dk]
        q = q * (HEAD_DIM ** -0.5)
        for hd in range(n_q):
            q_ref[bi, hd] = q[:, hd * HEAD_DIM:(hd + 1) * HEAD_DIM].astype(BF16)
        for hd in range(n_kv):
            k_ref[bi, hd] = k[:, hd * HEAD_DIM:(hd + 1) * HEAD_DIM].astype(BF16)
            v_ref[bi, hd] = v[:, hd * HEAD_DIM:(hd + 1) * HEAD_DIM].astype(BF16)


def _rope_tables(seq, n_ctx_tok, width):
    t = np.arange(seq)
    pos = np.stack([t // GRID_W, t % GRID_W], axis=-1).astype(np.float32)
    inv = (ROPE_BASE ** (-np.arange(ROPE_FREQS, dtype=np.float32) / ROPE_FREQS)).astype(np.float32)
    ang = jnp.asarray(pos)[:, :, None] * jnp.asarray(inv)
    cos = jnp.cos(ang)
    sin = jnp.sin(ang)
    cos_h = jnp.concatenate([cos, cos], axis=-1).reshape(seq, HEAD_DIM)
    sin_h = jnp.concatenate([-sin, sin], axis=-1).reshape(seq, HEAD_DIM)
    cos_h = jnp.concatenate([jnp.ones((n_ctx_tok, HEAD_DIM), F32), cos_h], axis=0)
    sin_h = jnp.concatenate([jnp.zeros((n_ctx_tok, HEAD_DIM), F32), sin_h], axis=0)
    reps = width // HEAD_DIM
    return jnp.tile(cos_h, (1, reps)), jnp.tile(sin_h, (1, reps))


def _in_proj_qkv(srcs, modx, modc, g, w, q_g, k_g, n_ctx_tok, *, s5_w, n_q, n_kv, bb=2):
    b, _, d = srcs[0].shape
    s = sum(a.shape[1] for a in srcs)
    n = w.shape[1]
    dq = n_q * HEAD_DIM
    cos, sin = _rope_tables(s - n_ctx_tok, n_ctx_tok, dq)
    avg = jnp.asarray(np.kron(np.eye(n_q, dtype=np.float32), np.full((HEAD_DIM, HEAD_DIM), 1.0 / HEAD_DIM, np.float32)), BF16)
    qg = jnp.tile(q_g.astype(F32), n_q).reshape(1, dq)
    kg = jnp.tile(k_g.astype(F32), n_q).reshape(1, dq)
    tm = ROW_TILE

    def heads(n_h):
        return pl.BlockSpec((bb, n_h, tm, HEAD_DIM), lambda i, t: (i, 0, t, 0))

    return pl.pallas_call(
        functools.partial(_in_qkv_kernel, n_src=len(srcs), s5_w=s5_w, n_q=n_q, n_kv=n_kv),
        out_shape=(jax.ShapeDtypeStruct((b, s, s5_w), BF16),
                   jax.ShapeDtypeStruct((b, n_q, s, HEAD_DIM), BF16),
                   jax.ShapeDtypeStruct((b, n_kv, s, HEAD_DIM), BF16),
                   jax.ShapeDtypeStruct((b, n_kv, s, HEAD_DIM), BF16)),
        grid=(b // bb, s // tm),
        in_specs=_stream_specs(srcs, bb) + [
            pl.BlockSpec((bb, 6, d), lambda i, t: (i, 0, 0)),
            pl.BlockSpec((1, 6, d), lambda i, t: (0, 0, 0)),
            pl.BlockSpec((1, d), lambda i, t: (0, 0)),
            pl.BlockSpec((d, n), lambda i, t: (0, 0)),
            pl.BlockSpec((tm, dq), lambda i, t: (t, 0)),
            pl.BlockSpec((tm, dq), lambda i, t: (t, 0)),
            pl.BlockSpec((1, dq), lambda i, t: (0, 0)),
            pl.BlockSpec((1, dq), lambda i, t: (0, 0)),
            pl.BlockSpec((dq, dq), lambda i, t: (0, 0))],
        out_specs=(pl.BlockSpec((bb, tm, s5_w), lambda i, t: (i, t, 0)), heads(n_q), heads(n_kv), heads(n_kv)),
        compiler_params=_cparams("parallel", "arbitrary"),
        name="in_proj_qkv",
    )(*srcs, modx, modc, g.reshape(1, d), w, cos, sin, qg, kg, avg)


def _softmax_pv(s, v):
    m = jnp.max(s, axis=-1, keepdims=True)
    p = jnp.exp(s - m)
    l = jnp.sum(p, axis=-1, keepdims=True)
    return jnp.dot(p.astype(BF16), v, preferred_element_type=F32) / l


def _gqa_kernel(q_ref, k_ref, v_ref, o_ref, *, n_kv, grp, n_ctx_tok):
    tq = q_ref.shape[2]
    nt = (((1,), (1,)), ((), ()))

    def attend(n_keys):
        for h in range(n_kv):
            q2 = q_ref[0, h * grp:(h + 1) * grp].reshape(grp * tq, HEAD_DIM)
            s = lax.dot_general(q2, k_ref[0, h, 0:n_keys, :], nt, preferred_element_type=F32)
            o = _softmax_pv(s, v_ref[0, h, 0:n_keys, :])
            for j in range(grp):
                c0 = (h * grp + j) * HEAD_DIM
                o_ref[0, :, c0:c0 + HEAD_DIM] = o[j * tq:(j + 1) * tq].astype(o_ref.dtype)

    @pl.when(pl.program_id(1) == 0)
    def _():
        attend(n_ctx_tok)

    @pl.when(pl.program_id(1) != 0)
    def _():
        attend(k_ref.shape[2])


def _gqa_attention(q, k, v, n_ctx_tok):
    b, n_q, s, _ = q.shape
    n_kv = k.shape[1]
    tq = ROW_TILE
    return pl.pallas_call(
        functools.partial(_gqa_kernel, n_kv=n_kv, grp=n_q // n_kv, n_ctx_tok=n_ctx_tok),
        out_shape=jax.ShapeDtypeStruct((b, s, n_q * HEAD_DIM), BF16),
        grid=(b, s // tq),
        in_specs=[pl.BlockSpec((1, n_q, tq, HEAD_DIM), lambda i, t: (i, 0, t, 0)),
                  pl.BlockSpec((1, n_kv, s, HEAD_DIM), lambda i, t: (i, 0, 0, 0)),
                  pl.BlockSpec((1, n_kv, s, HEAD_DIM), lambda i, t: (i, 0, 0, 0))],
        out_specs=pl.BlockSpec((1, tq, n_q * HEAD_DIM), lambda i, t: (i, t, 0)),
        compiler_params=_cparams("parallel", "arbitrary"),
        name="gqa_attention",
    )(q, k, v)


def _gelu_tanh(y):
    return 0.5 * y * (1.0 + jnp.tanh(math.sqrt(2.0 / math.pi) * (y + 0.044715 * (y * y * y))))


def _mix_even_kernel(*refs, n_src):
    src, (y_ref, att_ref, modx_ref, modc_ref, wglu_ref, wa_ref, wb_ref, o_ref) = refs[:n_src], refs[n_src:]
    bb, tm, d = src[0].shape
    is_ctx = pl.program_id(1) == 0
    mod = _pick_mod(modx_ref, modc_ref, is_ctx)
    y = y_ref[...].reshape(bb * tm, -1).astype(F32)
    g = _gelu_tanh(y)
    a = g * jax.nn.sigmoid(_bdot(g, wglu_ref[...]))
    ox = _bdot(a, wa_ref[...]) + jnp.dot(att_ref[...].reshape(bb * tm, -1), wb_ref[...], preferred_element_type=F32)
    o_ref[...] = _stream_tile(src, is_ctx) + mod[:, 2:3, :] * ox.reshape(bb, tm, d)


def _mix_even(srcs, y_s5, att, modx, modc, w_glu, w_out, bb=2):
    b, s, w5 = y_s5.shape
    d = srcs[0].shape[2]
    wa = att.shape[2]
    tm = ROW_TILE
    return pl.pallas_call(
        functools.partial(_mix_even_kernel, n_src=len(srcs)),
        out_shape=jax.ShapeDtypeStruct((b, s, d), F32),
        grid=(b // bb, s // tm),
        in_specs=_stream_specs(srcs, bb) + [
                  pl.BlockSpec((bb, tm, w5), lambda i, t: (i, t, 0)),
                  pl.BlockSpec((bb, tm, wa), lambda i, t: (i, t, 0)),
                  pl.BlockSpec((bb, 6, d), lambda i, t: (i, 0, 0)),
                  pl.BlockSpec((1, 6, d), lambda i, t: (0, 0, 0)),
                  pl.BlockSpec((w5, w5), lambda i, t: (0, 0)),
                  pl.BlockSpec((w5, d), lambda i, t: (0, 0)),
                  pl.BlockSpec((wa, d), lambda i, t: (1, 0))],
        out_specs=pl.BlockSpec((bb, tm, d), lambda i, t: (i, t, 0)),
        compiler_params=_cparams("parallel", "arbitrary"),
        name="mix_even",
    )(*srcs, y_s5, att, modx, modc, w_glu, w_out, w_out)


def _ffn_kernel(x_ref, modx_ref, modc_ref, g_ref, w1_ref, w3_ref, w2_ref, o_ref, h_scr, acc_scr):
    bb, tm, d = x_ref.shape
    j = pl.program_id(2)
    mod = _pick_mod(modx_ref, modc_ref, pl.program_id(1) == 0)

    @pl.when(j == 0)
    def _():
        h = _norm_mod(x_ref[...], g_ref[...], mod[:, 3:4, :], mod[:, 4:5, :])
        h_scr[...] = h.reshape(bb * tm, d).astype(BF16)
        acc_scr[...] = jnp.zeros_like(acc_scr)

    h = h_scr[...]
    a = jnp.dot(h, w1_ref[...], preferred_element_type=F32)
    g = jnp.dot(h, w3_ref[...], preferred_element_type=F32)
    acc_scr[...] += _bdot(_silu(a) * g, w2_ref[...])

    @pl.when(j == pl.num_programs(2) - 1)
    def _():
        o_ref[...] = x_ref[...] + mod[:, 5:6, :] * acc_scr[...].reshape(bb, tm, d)


def _ffn(stream, modx, modc, g, w1, w3, w2, bb=4, tf=896):
    b, s, d = stream.shape
    f = w1.shape[1]
    tm = ROW_TILE
    return pl.pallas_call(
        _ffn_kernel,
        out_shape=jax.ShapeDtypeStruct((b, s, d), F32),
        grid=(b // bb, s // tm, f // tf),
        in_specs=[pl.BlockSpec((bb, tm, d), lambda i, t, j: (i, t, 0)),
                  pl.BlockSpec((bb, 6, d), lambda i, t, j: (i, 0, 0)),
                  pl.BlockSpec((1, 6, d), lambda i, t, j: (0, 0, 0)),
                  pl.BlockSpec((1, d), lambda i, t, j: (0, 0)),
                  pl.BlockSpec((d, tf), lambda i, t, j: (0, j)),
                  pl.BlockSpec((d, tf), lambda i, t, j: (0, j)),
                  pl.BlockSpec((tf, d), lambda i, t, j: (j, 0))],
        out_specs=pl.BlockSpec((bb, tm, d), lambda i, t, j: (i, t, 0)),
        scratch_shapes=[pltpu.VMEM((bb * tm, d), BF16), pltpu.VMEM((bb * tm, d), F32)],
        compiler_params=_cparams("parallel", "arbitrary", "arbitrary"),
        name="ffn_dense",
    )(stream, modx, modc, g.reshape(1, d), w1, w3, w2)


def _even_layer(srcs, modx, modc, n_ctx_tok, norm1_g, norm2_g, w_in, w_out, s5_params, w_glu, q_g, k_g,
                ffn_w1, ffn_w3, ffn_w2):
    s5_w = w_glu.shape[0]
    n_q = s5_w // HEAD_DIM
    n_kv = n_q // 2
    u, q, k, v = _in_proj_qkv(srcs, modx, modc, norm1_g, w_in.astype(BF16), q_g, k_g, n_ctx_tok,
                              s5_w=s5_w, n_q=n_q, n_kv=n_kv)
    y_s5 = _s5_scan(u, _s5_tables(*s5_params), n_ctx_tok)
    att = _gqa_attention(q, k, v, n_ctx_tok)
    stream = _mix_even(srcs, y_s5, att, modx, modc, w_glu.astype(BF16), w_out.astype(BF16))
    return _ffn(stream, modx, modc, norm2_g, ffn_w1.astype(BF16), ffn_w3.astype(BF16), ffn_w2.astype(BF16))


HYENA_BANDS = 16
CONV_TILE = 256


def _hy_filter_kernel(z_ref, w1_ref, b1_ref, w2_ref, b2_ref, w3_ref, freq_ref, decay_ref, t_ref, h_ref):
    f = freq_ref[...]
    h = jnp.sin(f * (jnp.dot(z_ref[...], w1_ref[...], preferred_element_type=F32, precision=HIGHEST) + b1_ref[...]))
    h = jnp.sin(f * (jnp.dot(h, w2_ref[...], preferred_element_type=F32, precision=HIGHEST) + b2_ref[...]))
    h = jnp.dot(h, w3_ref[...], preferred_element_type=F32, precision=HIGHEST)
    h = h * jnp.exp(-t_ref[...] * jnp.abs(decay_ref[...]))
    h_ref[...] = h / (jnp.sum(jnp.abs(h), axis=0, keepdims=True) + EPS)


def _hy_filters(seq, f_w1, f_b1, f_w2, f_b2, f_w3, f_freq, f_decay):
    k = np.arange(seq, dtype=np.float32)
    t = k / max(seq - 1, 1)
    bands = np.linspace(1e-4, HYENA_BANDS - 1, HYENA_BANDS, dtype=np.float32)
    ang = jnp.asarray(np.float32(2.0 * math.pi / seq) * k[:, None] * bands[None, :])
    pos_dim, hidden = f_w1.shape
    zpad = 128
    z = jnp.concatenate([jnp.asarray(t)[:, None], jnp.cos(ang), -jnp.sin(ang),
                         jnp.zeros((seq, zpad - pos_dim), F32)], axis=-1)
    w1 = jnp.concatenate([f_w1.astype(F32), jnp.zeros((zpad - pos_dim, hidden), F32)], axis=0)
    n = f_w3.shape[1]
    tc = 512
    return pl.pallas_call(
        _hy_filter_kernel,
        out_shape=jax.ShapeDtypeStruct((seq, n), F32),
        grid=(n // tc,),
        in_specs=[pl.BlockSpec((seq, zpad), lambda j: (0, 0)),
                  pl.BlockSpec((zpad, hidden), lambda j: (0, 0)),
                  pl.BlockSpec((1, hidden), lambda j: (0, 0)),
                  pl.BlockSpec((hidden, hidden), lambda j: (0, 0)),
                  pl.BlockSpec((1, hidden), lambda j: (0, 0)),
                  pl.BlockSpec((hidden, tc), lambda j: (0, j)),
                  pl.BlockSpec((1, hidden), lambda j: (0, 0)),
                  pl.BlockSpec((1, tc), lambda j: (0, j)),
                  pl.BlockSpec((seq, 1), lambda j: (0, 0))],
        out_specs=pl.BlockSpec((seq, tc), lambda j: (0, j)),
        compiler_params=_cparams("arbitrary"),
        name="hyena_filter",
    )(z, w1, f_b1.reshape(1, hidden).astype(F32), f_w2.astype(F32), f_b2.reshape(1, hidden).astype(F32),
      f_w3.astype(F32), f_freq.reshape(1, hidden).astype(F32), f_decay.reshape(1, n).astype(F32),
      jnp.asarray(t)[:, None])


def _dft_tables(seq):
    idx = np.arange(seq, dtype=np.int64)
    m = jnp.asarray(((idx[:, None] * idx[None, :]) % (2 * seq)).astype(np.int32))
    ang = m.astype(F32) * np.float32(math.pi / seq)
    alt = jnp.asarray((1.0 - 2.0 * (idx % 2)).astype(np.float32))[:, None]
    return jnp.cos(ang).astype(BF16), jnp.sin(ang).astype(BF16), alt


def _split_bf16(a):
    hi = a.astype(BF16)
    return hi, (a - hi.astype(F32)).astype(BF16)


def _hy_block_kernels(h):
    seq, n2 = h.shape
    n = n2 // 2
    hh = seq // 2
    hf, hb = h[:, :n], h[:, n:]
    zero = jnp.zeros((1, n), h.dtype)
    t0_lo = hf[:hh].at[0].add(hb[0])
    t0_hi = jnp.concatenate([zero, hb[1:hh][::-1]], axis=0)
    t1_lo = hf[hh:]
    t1_hi = jnp.concatenate([zero, hf[1:hh]], axis=0)
    tm_lo = hb[1:hh + 1][::-1]
    tm_hi = jnp.concatenate([zero, hb[hh + 1:][::-1]], axis=0)
    return jnp.stack([t0_lo, t0_hi, t1_lo, t1_hi, tm_lo, tm_hi])


def _hy_spec_kernel(lo_ref, hi_ref, cos_ref, sin_ref, alt_ref, wk_ref, sre_ref, sim_ref, sny_ref, *, half_len):
    c = cos_ref[...]
    s = sin_ref[...]
    alt = alt_ref[...]
    wk = wk_ref[...]

    def dft(tab, a):
        a_hi, a_lo = _split_bf16(a)
        return jnp.dot(tab, a_hi, preferred_element_type=F32) + jnp.dot(tab, a_lo, preferred_element_type=F32)

    lo = lo_ref[0]
    hi = hi_ref[0]
    sre_ref[0] = wk * (dft(c, lo) + alt * dft(c, hi))
    sim_ref[0] = -wk * (dft(s, lo) + alt * dft(s, hi))
    sny_ref[0] = jnp.sum(alt * (lo + hi), axis=0, keepdims=True) * (0.5 / half_len)


def _hy_spectrum(kernels, cos, sin, alt):
    _, hh, n = kernels.shape
    tc = 128
    nb = n // tc
    wk = jnp.full((hh, 1), 1.0 / hh, F32).at[0, 0].set(0.5 / hh)
    return pl.pallas_call(
        functools.partial(_hy_spec_kernel, half_len=hh),
        out_shape=(jax.ShapeDtypeStruct((3, hh, n), F32), jax.ShapeDtypeStruct((3, hh, n), F32),
                   jax.ShapeDtypeStruct((3, 1, n), F32)),
        grid=(3, nb),
        in_specs=[pl.BlockSpec((1, hh, tc), lambda k, j: (2 * k, 0, j)),
                  pl.BlockSpec((1, hh, tc), lambda k, j: (2 * k + 1, 0, j)),
                  pl.BlockSpec((hh, hh), lambda k, j: (0, 0)),
                  pl.BlockSpec((hh, hh), lambda k, j: (0, 0)),
                  pl.BlockSpec((hh, 1), lambda k, j: (0, 0)),
                  pl.BlockSpec((hh, 1), lambda k, j: (0, 0))],
        out_specs=(pl.BlockSpec((1, hh, tc), lambda k, j: (k, 0, j)), pl.BlockSpec((1, hh, tc), lambda k, j: (k, 0, j)),
                   pl.BlockSpec((1, 1, tc), lambda k, j: (k, 0, j))),
        compiler_params=_cparams("arbitrary", "arbitrary"),
        name="hyena_spectrum",
    )(kernels, kernels, cos, sin, alt, wk)


def _hy_conv_kernel(z_ref, g_ref, cwz_ref, cbz_ref, cwg_ref, cbg_ref, cos_ref, sin_ref, alt_ref,
                    sre_ref, sim_ref, sny_ref, d_ref, o_ref, z_scr, zz_scr, yre_scr, yim_scr, *, conv_z, blk):
    seq = z_ref.shape[1]
    hh = seq // 2
    tc = z_ref.shape[2]
    row = lax.broadcasted_iota(jnp.int32, (blk, 1), 0)

    def short_conv(x_ref, w_ref, b_ref, l0):
        x = x_ref[0, l0:l0 + blk, :]
        w = w_ref[...]
        top = x_ref[0, l0 - 1:l0, :] if l0 > 0 else jnp.zeros((1, tc), F32)
        bot = x_ref[0, l0 + blk:l0 + blk + 1, :] if l0 + blk < seq else jnp.zeros((1, tc), F32)
        prev = jnp.where(row == 0, top, pltpu.roll(x, 1, 0))
        nxt = jnp.where(row == blk - 1, bot, pltpu.roll(x, blk - 1, 0))
        return prev * w[0:1] + x * w[1:2] + nxt * w[2:3] + b_ref[...]

    zny = [jnp.zeros((1, tc), F32), jnp.zeros((1, tc), F32)]
    for l0 in range(0, seq, blk):
        z = short_conv(z_ref, cwz_ref, cbz_ref, l0) if conv_z else z_ref[0, l0:l0 + blk, :]
        a, r0 = divmod(l0, hh)
        z_scr[l0:l0 + blk, :] = z
        zz_scr[r0:r0 + blk, a * tc:(a + 1) * tc] = z.astype(BF16)
        zny[a] = zny[a] + jnp.sum(alt_ref[r0:r0 + blk, :] * z, axis=0, keepdims=True)

    def spec_mul(zc, zs, kern, k0):
        sre = sre_ref[kern, k0:k0 + blk, :]
        sim = sim_ref[kern, k0:k0 + blk, :]
        return zc * sre + zs * sim, zc * sim - zs * sre

    for k0 in range(0, hh, blk):
        zc = jnp.dot(cos_ref[k0:k0 + blk, :], zz_scr[...], preferred_element_type=F32)
        zs = jnp.dot(sin_ref[k0:k0 + blk, :], zz_scr[...], preferred_element_type=F32)
        zc0, zc1, zs0, zs1 = zc[:, :tc], zc[:, tc:], zs[:, :tc], zs[:, tc:]
        for a, (k_first, k_second) in enumerate(((0, 2), (1, 0))):
            re0, im0 = spec_mul(zc0, zs0, k_first, k0)
            re1, im1 = spec_mul(zc1, zs1, k_second, k0)
            yre_scr[k0:k0 + blk, a * tc:(a + 1) * tc] = (re0 + re1).astype(BF16)
            yim_scr[k0:k0 + blk, a * tc:(a + 1) * tc] = (im0 + im1).astype(BF16)
    yny = [sny_ref[0] * zny[0] + sny_ref[2] * zny[1], sny_ref[1] * zny[0] + sny_ref[0] * zny[1]]
    d = d_ref[0]
    for r0 in range(0, hh, blk):
        y = (jnp.dot(cos_ref[r0:r0 + blk, :], yre_scr[...], preferred_element_type=F32)
             - jnp.dot(sin_ref[r0:r0 + blk, :], yim_scr[...], preferred_element_type=F32))
        for a in range(2):
            l0 = a * hh + r0
            ya = y[:, a * tc:(a + 1) * tc] + alt_ref[r0:r0 + blk, :] * yny[a]
            gate = short_conv(g_ref, cwg_ref, cbg_ref, l0)
            o_ref[0, l0:l0 + blk, :] = gate * (ya + d * z_scr[l0:l0 + blk, :])


def _hy_order(z_src, z_off, g_src, g_off, conv_w, conv_b, cos, sin, alt, sre, sim, sny, hy_d, order, width, conv_z):
    b, seq, _ = z_src.shape
    hh = seq // 2
    tc = CONV_TILE
    nb = width // tc
    zo, go = z_off // tc, g_off // tc

    def col(off):
        return lambda j, i: (0, j + off)

    return pl.pallas_call(
        functools.partial(_hy_conv_kernel, conv_z=conv_z, blk=512),
        out_shape=jax.ShapeDtypeStruct((b, seq, width), F32),
        grid=(nb, b),
        in_specs=[pl.BlockSpec((1, seq, tc), lambda j, i: (i, 0, j + zo)),
                  pl.BlockSpec((1, seq, tc), lambda j, i: (i, 0, j + go)),
                  pl.BlockSpec((3, tc), col(zo if conv_z else 0)), pl.BlockSpec((1, tc), col(zo if conv_z else 0)),
                  pl.BlockSpec((3, tc), col(go)), pl.BlockSpec((1, tc), col(go)),
                  pl.BlockSpec((hh, hh), lambda j, i: (0, 0)),
                  pl.BlockSpec((hh, hh), lambda j, i: (0, 0)),
                  pl.BlockSpec((hh, 1), lambda j, i: (0, 0)),
                  pl.BlockSpec((3, hh, tc), lambda j, i: (0, 0, j + order * nb)),
                  pl.BlockSpec((3, hh, tc), lambda j, i: (0, 0, j + order * nb)),
                  pl.BlockSpec((3, 1, tc), lambda j, i: (0, 0, j + order * nb)),
                  pl.BlockSpec((1, 1, tc), lambda j, i: (order, 0, j))],
        out_specs=pl.BlockSpec((1, seq, tc), lambda j, i: (i, 0, j)),
        scratch_shapes=[pltpu.VMEM((seq, tc), F32), pltpu.VMEM((hh, 2 * tc), BF16),
                        pltpu.VMEM((hh, 2 * tc), BF16), pltpu.VMEM((hh, 2 * tc), BF16)],
        compiler_params=_cparams("arbitrary", "arbitrary"),
        name="hyena_conv",
    )(z_src, g_src, conv_w, conv_b, conv_w, conv_b, cos, sin, alt, sre, sim, sny, hy_d.reshape(2, 1, width))


def _hy_conv(proj, conv_w, conv_b, cos, sin, alt, sre, sim, sny, hy_d, width):
    args = (conv_w, conv_b, cos, sin, alt, sre, sim, sny, hy_d)
    z1 = _hy_order(proj, 0, proj, width, *args, order=0, width=width, conv_z=True)
    return _hy_order(z1, 0, proj, 2 * width, *args, order=1, width=width, conv_z=False)


NA_WIN_ROWS = 8
NA_WIN_COLS = 16
NA_QROWS = ROW_TILE // GRID_W
NA_KTILES = 3


def _na_bias_tiles(rpb, rows):
    nqb = rows // NA_QROWS
    assert nqb >= 3 and rows >= NA_WIN_ROWS + NA_QROWS
    col = np.arange(GRID_W)
    col_start = np.clip(col - NA_WIN_COLS // 2, 0, GRID_W - NA_WIN_COLS)
    col_ok = (col[None, :] >= col_start[:, None]) & (col[None, :] < col_start[:, None] + NA_WIN_COLS)
    dc_idx = np.clip(col[None, :] - col[:, None] + NA_WIN_COLS - 1, 0, 2 * NA_WIN_COLS - 2)
    tiles = []
    for j in (0, 1, nqb - 1):
        kb0 = min(max(j - 1, 0), nqb - NA_KTILES)
        q_r = j * NA_QROWS + np.arange(NA_QROWS)
        k_r = kb0 * NA_QROWS + np.arange(NA_KTILES * NA_QROWS)
        r0 = np.clip(q_r - NA_WIN_ROWS // 2, 0, rows - NA_WIN_ROWS)
        row_ok = (k_r[None, :] >= r0[:, None]) & (k_r[None, :] < r0[:, None] + NA_WIN_ROWS)
        dr_idx = np.clip(k_r[None, :] - q_r[:, None] + NA_WIN_ROWS - 1, 0, 2 * NA_WIN_ROWS - 2)
        ok = row_ok[:, None, :, None] & col_ok[None, :, None, :]
        pick_r = jnp.asarray(dr_idx[:, :, None] == np.arange(2 * NA_WIN_ROWS - 1), F32)
        pick_c = jnp.asarray(dc_idx[:, :, None] == np.arange(2 * NA_WIN_COLS - 1), F32)
        by_row = jnp.einsum('hij,abi->habj', rpb.astype(F32), pick_r, precision=HIGHEST)
        bias = jnp.einsum('habj,cdj->hacbd', by_row, pick_c, precision=HIGHEST)
        tile = jnp.where(jnp.asarray(ok)[None], bias, -jnp.inf)
        tiles.append(tile.reshape(rpb.shape[0], ROW_TILE, NA_KTILES * ROW_TILE))
    return jnp.stack(tiles, axis=1)


def _na_kernel(q_ref, k0_ref, k1_ref, k2_ref, v0_ref, v1_ref, v2_ref, kc_ref, vc_ref, bias_ref, o_ref, *, n_heads):
    nt = (((1,), (1,)), ((), ()))
    q = (q_ref[0] * (HEAD_DIM ** -0.5)).astype(BF16)
    k = jnp.concatenate([k0_ref[0], k1_ref[0], k2_ref[0]], axis=0).astype(BF16)
    v = jnp.concatenate([v0_ref[0], v1_ref[0], v2_ref[0]], axis=0).astype(BF16)
    kc = kc_ref[0].astype(BF16)
    vc = vc_ref[0].astype(BF16)
    for h in range(n_heads):
        sl = slice(h * HEAD_DIM, (h + 1) * HEAD_DIM)
        s_loc = lax.dot_general(q[:, sl], k[:, sl], nt, preferred_element_type=F32) + bias_ref[h, 0]
        s_ctx = lax.dot_general(q[:, sl], kc[:, sl], nt, preferred_element_type=F32)
        m = jnp.maximum(jnp.max(s_loc, axis=-1, keepdims=True), jnp.max(s_ctx, axis=-1, keepdims=True))
        p_loc = jnp.exp(s_loc - m)
        p_ctx = jnp.exp(s_ctx - m)
        l = jnp.sum(p_loc, axis=-1, keepdims=True) + jnp.sum(p_ctx, axis=-1, keepdims=True)
        o = (jnp.dot(p_loc.astype(BF16), v[:, sl], preferred_element_type=F32)
             + jnp.dot(p_ctx.astype(BF16), vc[:, sl], preferred_element_type=F32)) / l
        o_ref[0, :, sl] = o.astype(o_ref.dtype)


def _na_attention(proj, proj_c, rpb, *, q_off, n_heads):
    b, seq, _ = proj.shape
    w = n_heads * HEAD_DIM
    tm = ROW_TILE
    nqb = seq // tm
    bias = _na_bias_tiles(rpb, seq // GRID_W)
    qc = q_off // w

    def kb0(j):
        return jnp.clip(j - 1, 0, nqb - NA_KTILES)

    def kv_spec(cb, off):
        return pl.BlockSpec((1, tm, w), lambda i, j: (i, kb0(j) + off, cb))

    def bias_type(j):
        return jnp.where(j == 0, 0, jnp.where(j == nqb - 1, 2, 1))

    return pl.pallas_call(
        functools.partial(_na_kernel, n_heads=n_heads),
        out_shape=jax.ShapeDtypeStruct((b, seq, w), BF16),
        grid=(b, nqb),
        in_specs=[pl.BlockSpec((1, tm, w), lambda i, j: (i, j, qc)),
                  kv_spec(qc + 1, 0), kv_spec(qc + 1, 1), kv_spec(qc + 1, 2),
                  kv_spec(qc + 2, 0), kv_spec(qc + 2, 1), kv_spec(qc + 2, 2),
                  pl.BlockSpec((1, proj_c.shape[1], w), lambda i, j: (i, 0, 0)),
                  pl.BlockSpec((1, proj_c.shape[1], w), lambda i, j: (i, 0, 1)),
                  pl.BlockSpec((n_heads, 1, tm, NA_KTILES * tm), lambda i, j: (0, bias_type(j), 0, 0))],
        out_specs=pl.BlockSpec((1, tm, w), lambda i, j: (i, j, 0)),
        compiler_params=_cparams("parallel", "arbitrary"),
        name="na_attention",
    )(proj, proj, proj, proj, proj, proj, proj, proj_c, proj_c, bias)


MOE_TOP_K = 2


def _mix_odd_kernel(x_ref, hy_ref, na_ref, modx_ref, g_ref, wa_ref, wb_ref, rt_ref, x1_ref, h2_ref, idx_ref, wt_ref,
                    *, n_e):
    bb, tm, d = x_ref.shape
    mod = modx_ref[...]
    ox = (_bdot(hy_ref[...].reshape(bb * tm, -1), wa_ref[...])
          + jnp.dot(na_ref[...].reshape(bb * tm, -1), wb_ref[...], preferred_element_type=F32))
    x1 = x_ref[...] + mod[:, 2:3, :] * ox.reshape(bb, tm, d)
    x1_ref[...] = x1
    h2 = _norm_mod(x1, g_ref[...], mod[:, 3:4, :], mod[:, 4:5, :])
    h2_ref[...] = _rows_to_tiles(h2.reshape(bb * tm, d)).reshape(h2_ref.shape)
    h_hi, h_lo = _split_bf16(h2.reshape(bb * tm, d))
    lg = (jnp.dot(h_hi, rt_ref[0], preferred_element_type=F32) + jnp.dot(h_lo, rt_ref[0], preferred_element_type=F32)
          + jnp.dot(h_hi, rt_ref[1], preferred_element_type=F32))
    eid = lax.broadcasted_iota(jnp.int32, lg.shape, 1)
    lg = jnp.where(eid < n_e, lg, -jnp.inf)
    m1 = jnp.max(lg, axis=1, keepdims=True)
    i1 = jnp.min(jnp.where(lg == m1, eid, n_e), axis=1, keepdims=True)
    lg2 = jnp.where(eid == i1, -jnp.inf, lg)
    m2 = jnp.max(lg2, axis=1, keepdims=True)
    i2 = jnp.min(jnp.where(lg2 == m2, eid, n_e), axis=1, keepdims=True)
    e2 = jnp.exp(m2 - m1)
    den = 1.0 + e2
    first = lax.broadcasted_iota(jnp.int32, (bb * tm, MOE_TOP_K), 1) == 0
    idx_ref[...] = jnp.where(first, i1, i2).reshape(bb, tm, MOE_TOP_K)
    wt_ref[...] = jnp.where(first, 1.0 / den, e2 / den).reshape(bb, tm, MOE_TOP_K)


ROUTER_LANES = 128


def _mix_odd(stream, t_off, o_hy, o_na, modx, g, w_out, router, bb=2):
    b, _, d = stream.shape
    seq = o_hy.shape[1]
    wh = o_hy.shape[2]
    wn = o_na.shape[2]
    n_e = router.shape[1]
    tm = ROW_TILE
    n_t = seq // tm
    router_pad = jnp.concatenate([router.astype(F32), jnp.zeros((d, ROUTER_LANES - n_e), F32)], axis=1)
    router_pad = jnp.stack(_split_bf16(router_pad))
    return pl.pallas_call(
        functools.partial(_mix_odd_kernel, n_e=n_e),
        out_shape=(jax.ShapeDtypeStruct((b, seq, d), F32), jax.ShapeDtypeStruct((b, seq) + _tile_shape(d), F32),
                   jax.ShapeDtypeStruct((b, seq, MOE_TOP_K), jnp.int32), jax.ShapeDtypeStruct((b, seq, MOE_TOP_K), F32)),
        grid=(b // bb, n_t),
        in_specs=[pl.BlockSpec((bb, tm, d), lambda i, t: (i, t + t_off, 0)),
                  pl.BlockSpec((bb, tm, wh), lambda i, t: (i, t, 0)),
                  pl.BlockSpec((bb, tm, wn), lambda i, t: (i, t, 0)),
                  pl.BlockSpec((bb, 6, d), lambda i, t: (i, 0, 0)),
                  pl.BlockSpec((1, d), lambda i, t: (0, 0)),
                  pl.BlockSpec((wh, d), lambda i, t: (0, 0)),
                  pl.BlockSpec((wn, d), lambda i, t: (1, 0)),
                  pl.BlockSpec((2, d, ROUTER_LANES), lambda i, t: (0, 0, 0))],
        out_specs=(pl.BlockSpec((bb, tm, d), lambda i, t: (i, t, 0)),
                   pl.BlockSpec((bb, tm) + _tile_shape(d), lambda i, t: (i, t, 0, 0)),
                   pl.BlockSpec((bb, tm, MOE_TOP_K), lambda i, t: (i, t, 0)),
                   pl.BlockSpec((bb, tm, MOE_TOP_K), lambda i, t: (i, t, 0))),
        compiler_params=_cparams("parallel", "arbitrary"),
        name="mix_odd_router",
    )(stream, o_hy, o_na, modx, g.reshape(1, d), w_out, w_out, router_pad)


MOE_BLOCK = 1024
MOE_FF_TILE = 896


def _moe_plan(idx, n_experts):
    t, k = idx.shape
    e_flat = idx.reshape(-1)
    onehot = (e_flat[:, None] == jnp.arange(n_experts)[None, :]).astype(jnp.int32)
    rank = jnp.sum((jnp.cumsum(onehot, axis=0) - onehot) * onehot, axis=1)
    counts = jnp.sum(onehot, axis=0)
    padded = (counts + MOE_BLOCK - 1) // MOE_BLOCK * MOE_BLOCK
    pend = jnp.cumsum(padded)
    pstart = pend - padded
    slot = pstart[e_flat] + rank
    n_blocks = (t * k) // MOE_BLOCK + n_experts
    block_e = jnp.clip(jnp.searchsorted(pend, jnp.arange(n_blocks) * MOE_BLOCK, side='right'), 0, n_experts - 1)
    order = jnp.argsort(e_flat, stable=True).astype(jnp.int32)
    pad_before = pstart - (jnp.cumsum(counts) - counts)
    src = jnp.arange(n_blocks * MOE_BLOCK, dtype=jnp.int32) - jnp.repeat(pad_before[block_e], MOE_BLOCK)
    slot_tok = order[jnp.clip(src, 0, t * k - 1)] // k
    n_used = (pend[-1] // MOE_BLOCK).astype(jnp.int32).reshape(1)
    return (slot_tok.reshape(n_blocks, 1, MOE_BLOCK), block_e.astype(jnp.int32), n_used,
            slot.reshape(t, k).astype(jnp.int32))


def _moe_ffn_kernel(be_ref, nu_ref, tok_ref, nxt_ref, h_hbm, w1_ref, w3_ref, w2_ref, y_ref, xg_scr, xb_scr, acc_scr,
                    sem, *, n_j):
    i = pl.program_id(0)
    j = pl.program_id(1)
    used = i < nu_ref[0]
    rows = xb_scr.shape[0]
    per_step = rows // n_j
    cur = i % 2
    nxt = 1 - cur

    def row_copy(idx_ref, r, buf):
        return pltpu.make_async_copy(h_hbm.at[pl.ds(idx_ref[0, 0, r], 1)], xg_scr.at[buf, pl.ds(r, 1)], sem.at[buf])

    def wait_rows(buf):
        pltpu.make_async_copy(h_hbm.at[pl.ds(0, rows)], xg_scr.at[buf], sem.at[buf]).wait()

    @pl.when(jnp.logical_and(i == 0, j == 0))
    def _():
        def start(r, carry):
            row_copy(tok_ref, r, 0).start()
            return carry
        lax.fori_loop(0, rows, start, 0)

    @pl.when(j == 0)
    def _():
        wait_rows(cur)
        xb_scr[...] = _tiles_to_rows(xg_scr[cur]).astype(BF16)
        acc_scr[...] = jnp.zeros_like(acc_scr)

    def prefetch_rows():
        base = j * per_step
        for k in range(per_step):
            row_copy(nxt_ref, base + k, nxt).start(priority=k % 2)

    @pl.when(used)
    def _():
        xb = xb_scr[...]
        a = jnp.dot(xb, w1_ref[0], preferred_element_type=F32)
        g = jnp.dot(xb, w3_ref[0], preferred_element_type=F32)
        acc_scr[...] += _bdot(_silu(a) * g, w2_ref[0])
        prefetch_rows()

    @pl.when(jnp.logical_not(used))
    def _():
        prefetch_rows()

    @pl.when(j == n_j - 1)
    def _():
        y_ref[...] = _rows_to_tiles(acc_scr[...])

    @pl.when(jnp.logical_and(i == pl.num_programs(0) - 1, j == n_j - 1))
    def _():
        wait_rows(nxt)


def _moe_ffn(h2, slot_tok, block_e, n_used, w1, w3, w2):
    tile = h2.shape[1:]
    d = tile[0] * tile[1]
    n_blocks = slot_tok.shape[0]
    f = w1.shape[2]
    mb = MOE_BLOCK
    tf = MOE_FF_TILE
    n_j = f // tf
    grid_spec = pltpu.PrefetchScalarGridSpec(
        num_scalar_prefetch=2,
        grid=(n_blocks, n_j),
        in_specs=[pl.BlockSpec((1, 1, mb), lambda i, j, be, nu: (i, 0, 0), memory_space=pltpu.SMEM),
                  pl.BlockSpec((1, 1, mb), lambda i, j, be, nu: (jnp.minimum(i + 1, n_blocks - 1), 0, 0),
                               memory_space=pltpu.SMEM),
                  pl.BlockSpec(memory_space=pl.ANY),
                  pl.BlockSpec((1, d, tf), lambda i, j, be, nu: (be[i], 0, j)),
                  pl.BlockSpec((1, d, tf), lambda i, j, be, nu: (be[i], 0, j)),
                  pl.BlockSpec((1, tf, d), lambda i, j, be, nu: (be[i], j, 0))],
        out_specs=pl.BlockSpec((mb,) + tile, lambda i, j, be, nu: (i, 0, 0)),
        scratch_shapes=[pltpu.VMEM((2, mb) + tile, F32), pltpu.VMEM((mb, d), BF16), pltpu.VMEM((mb, d), F32),
                        pltpu.SemaphoreType.DMA((2,))],
    )
    return pl.pallas_call(
        functools.partial(_moe_ffn_kernel, n_j=n_j),
        out_shape=jax.ShapeDtypeStruct((n_blocks * mb,) + tile, F32),
        grid_spec=grid_spec,
        compiler_params=_cparams("arbitrary", "arbitrary"),
        name="moe_expert_ffn",
    )(block_e, n_used, slot_tok, slot_tok, h2, w1, w3, w2)


def _moe_combine_kernel(slot_ref, nxt_ref, y_hbm, x1_ref, wt_ref, modx_ref, g_ref, o_ref, y_scr, sem):
    tm = x1_ref.shape[1]
    n_t = pl.num_programs(1)
    step = pl.program_id(0) * n_t + pl.program_id(1)
    n_steps = pl.num_programs(0) * n_t
    cur = step % 2
    nxt = 1 - cur

    def row_copy(idx_ref, r, buf):
        return pltpu.make_async_copy(y_hbm.at[pl.ds(idx_ref[0, 0, r], 1)], y_scr.at[buf, pl.ds(r, 1)], sem.at[buf])

    def wait_rows(buf):
        pltpu.make_async_copy(y_hbm.at[pl.ds(0, MOE_TOP_K * tm)], y_scr.at[buf], sem.at[buf]).wait()

    @pl.when(step == 0)
    def _():
        def start(r, carry):
            row_copy(slot_ref, r, 0).start()
            return carry
        lax.fori_loop(0, MOE_TOP_K * tm, start, 0)

    wait_rows(cur)
    for r in range(MOE_TOP_K * tm):
        row_copy(nxt_ref, r, nxt).start(priority=r % 2)
    wt = wt_ref[...]
    y = wt[:, 0:1] * _tiles_to_rows(y_scr[cur, 0:tm]) + wt[:, 1:2] * _tiles_to_rows(y_scr[cur, tm:2 * tm])
    x2 = x1_ref[0] + modx_ref[0, 5:6, :] * y
    o_ref[0] = (x2 * lax.rsqrt(jnp.mean(x2 * x2, axis=-1, keepdims=True) + EPS)) * g_ref[...]

    @pl.when(step == n_steps - 1)
    def _():
        wait_rows(nxt)


def _moe_combine(y_slots, slot_of, wt, x1, modx, final_g):
    b, seq, d = x1.shape
    tm = ROW_TILE
    n_t = seq // tm
    n_steps = b * n_t
    slots = slot_of.reshape(n_steps, tm, MOE_TOP_K).transpose(0, 2, 1).reshape(n_steps, 1, MOE_TOP_K * tm)
    return pl.pallas_call(
        _moe_combine_kernel,
        out_shape=jax.ShapeDtypeStruct((b, seq, d), F32),
        grid=(b, n_t),
        in_specs=[pl.BlockSpec((1, 1, MOE_TOP_K * tm), lambda i, t: (i * n_t + t, 0, 0), memory_space=pltpu.SMEM),
                  pl.BlockSpec((1, 1, MOE_TOP_K * tm), lambda i, t: (jnp.minimum(i * n_t + t + 1, n_steps - 1), 0, 0),
                               memory_space=pltpu.SMEM),
                  pl.BlockSpec(memory_space=pl.ANY),
                  pl.BlockSpec((1, tm, d), lambda i, t: (i, t, 0)),
                  pl.BlockSpec((tm, MOE_TOP_K), lambda i, t: (i * n_t + t, 0)),
                  pl.BlockSpec((1, 6, d), lambda i, t: (i, 0, 0)),
                  pl.BlockSpec((1, d), lambda i, t: (0, 0))],
        out_specs=pl.BlockSpec((1, tm, d), lambda i, t: (i, t, 0)),
        scratch_shapes=[pltpu.VMEM((2, MOE_TOP_K * tm) + y_slots.shape[1:], F32), pltpu.SemaphoreType.DMA((2,))],
        compiler_params=_cparams("arbitrary", "arbitrary"),
        name="moe_combine_norm",
    )(slots, slots, y_slots, x1, wt, modx, final_g.reshape(1, d))


def _odd_layer(stream, modx, modc, n_ctx_tok, norm1_g, norm2_g, w_in, w_out, conv_w, conv_b, filt, hy_d, rpb,
               router, moe_w1, moe_w3, moe_w2, final_g):
    b, s, d = stream.shape
    seq = s - n_ctx_tok
    hy_w = hy_d.shape[1]
    n_heads = rpb.shape[0]
    na_w = n_heads * HEAD_DIM
    t_off = n_ctx_tok // ROW_TILE
    w_in_b = w_in.astype(BF16)
    proj = _in_proj((stream,), modx, modc, norm1_g, w_in_b, t_off=t_off, n_t=seq // ROW_TILE, ctx_rule=_never_ctx)
    proj_c = _in_proj((stream,), modx, modc, norm1_g, w_in_b[:, 3 * hy_w + na_w:], t_off=0, n_t=t_off,
                      ctx_rule=_always_ctx)
    cos, sin, alt = _dft_tables(seq // 2)
    sre, sim, sny = _hy_spectrum(_hy_block_kernels(_hy_filters(seq, *filt)), cos, sin, alt)
    o_hy = _hy_conv(proj, conv_w.astype(F32), conv_b.reshape(1, -1).astype(F32), cos, sin, alt, sre, sim, sny,
                    hy_d.astype(F32), hy_w)
    o_na = _na_attention(proj, proj_c, rpb, q_off=3 * hy_w, n_heads=n_heads)
    x1, h2, idx, wt = _mix_odd(stream, t_off, o_hy, o_na, modx, norm2_g, w_out.astype(BF16), router)
    slot_tok, block_e, n_used, slot_of = _moe_plan(idx.reshape(b * seq, MOE_TOP_K), router.shape[1])
    y_slots = _moe_ffn(h2.reshape((b * seq,) + h2.shape[2:]), slot_tok, block_e, n_used,
                       moe_w1.astype(BF16), moe_w3.astype(BF16), moe_w2.astype(BF16))
    return _moe_combine(y_slots, slot_of, wt.reshape(b * seq, MOE_TOP_K), x1, modx, final_g)


def kernel(x, c, ctx, c_ctx, mod_w, mod_b, norm1_g, norm2_g, ev_w_in, ev_w_out, s5_lam_re, s5_lam_im, s5_log_dt, s5_b_re, s5_b_im, s5_c_re, s5_c_im, s5_d, s5_w_glu, gqa_q_g, gqa_k_g, ffn_w1, ffn_w3, ffn_w2, od_w_in, od_w_out, hy_conv_w, hy_conv_b, hy_w1, hy_b1, hy_w2, hy_b2, hy_w3, hy_freq, hy_decay, hy_d, na_rpb, moe_router, moe_w1, moe_w3, moe_w2, final_g):
    b, seq, d = x.shape
    n_ctx_tok = ctx.shape[1]
    assert n_ctx_tok == ROW_TILE and seq % ROW_TILE == 0
    rows = 8 * ((b + 1 + 7) // 8)
    cvec = jnp.zeros((rows, d), F32).at[:b].set(c).at[b].set(c_ctx)
    m = _modulation(cvec, mod_w, mod_b)
    modx = [m[l, :b].reshape(b, 6, d) for l in range(2)]
    modc = [m[l, b].reshape(1, 6, d) for l in range(2)]
    s5p = (s5_lam_re[0], s5_lam_im[0], s5_log_dt[0], s5_b_re[0], s5_b_im[0], s5_c_re[0], s5_c_im[0], s5_d[0])
    stream = _even_layer((ctx, x), modx[0], modc[0], n_ctx_tok, norm1_g[0], norm2_g[0], ev_w_in[0], ev_w_out[0], s5p,
                         s5_w_glu[0], gqa_q_g[0], gqa_k_g[0], ffn_w1[0], ffn_w3[0], ffn_w2[0])
    filt = (hy_w1[0], hy_b1[0], hy_w2[0], hy_b2[0], hy_w3[0], hy_freq[0], hy_decay[0])
    return _odd_layer(stream, modx[1], modc[1], n_ctx_tok, norm1_g[1], norm2_g[1], od_w_in[0], od_w_out[0],
                      hy_conv_w[0], hy_conv_b[0], filt, hy_d[0], na_rpb[0], moe_router[0],
                      moe_w1[0], moe_w3[0], moe_w2[0], final_g)
```

```python
import functools
import math

import jax
import jax.numpy as jnp
import numpy as np
from jax import lax
from jax.experimental import pallas as pl
from jax.experimental.pallas import tpu as pltpu

F32 = jnp.float32
BF16 = jnp.bfloat16
EPS = 1e-6
HEAD_DIM = 64
GRID_W = 64
ROPE_FREQS = HEAD_DIM // 4
ROPE_BASE = 10000.0
ROW_TILE = 256
S5_GROUP_CH = 16
S5_STATE = 64
S5_CHUNK = 16
VMEM_LIMIT = 56 * 1024 * 1024
HIGHEST = lax.Precision.HIGHEST


def _cparams(*sem):
    return pltpu.CompilerParams(dimension_semantics=sem, vmem_limit_bytes=VMEM_LIMIT)


def _bdot(a, b):
    return jnp.dot(a.astype(BF16), b.astype(BF16), preferred_element_type=F32)


def _silu(a):
    return a * jax.nn.sigmoid(a)


def _norm_mod(x, g, shift, scale):
    y = x * lax.rsqrt(jnp.mean(x * x, axis=-1, keepdims=True) + EPS)
    return (y * g) * (1.0 + scale) + shift


SUBLANES = 8
LANES = 128


def _tile_shape(d):
    assert d % (SUBLANES * LANES) == 0
    return (d // LANES, LANES)


def _rows_to_tiles(a):
    return a.reshape(a.shape[0], a.shape[1] // LANES, LANES)


def _tiles_to_rows(a):
    return a.reshape(a.shape[0], a.shape[1] * a.shape[2])


def _pick_mod(modx_ref, modc_ref, is_ctx):
    return jnp.where(is_ctx, modc_ref[...], modx_ref[...])


def _mod_kernel(c_ref, w_ref, b_ref, o_ref):
    s = _silu(c_ref[...])
    o_ref[0] = jnp.dot(s, w_ref[0], preferred_element_type=F32, precision=HIGHEST) + b_ref[0]


def _modulation(cvec, mod_w, mod_b):
    depth, d, n = mod_w.shape
    rows = cvec.shape[0]
    tn = 1024
    return pl.pallas_call(
        _mod_kernel,
        out_shape=jax.ShapeDtypeStruct((depth, rows, n), F32),
        grid=(depth, n // tn),
        in_specs=[pl.BlockSpec((rows, d), lambda l, j: (0, 0)),
                  pl.BlockSpec((1, d, tn), lambda l, j: (l, 0, j)),
                  pl.BlockSpec((1, 1, tn), lambda l, j: (l, 0, j))],
        out_specs=pl.BlockSpec((1, rows, tn), lambda l, j: (l, 0, j)),
        compiler_params=_cparams("arbitrary", "arbitrary"),
        name="adaln_mod",
    )(cvec, mod_w, mod_b.reshape(depth, 1, n))


def _stream_specs(srcs, bb, t_off=0):
    d = srcs[0].shape[2]
    if len(srcs) == 1:
        return [pl.BlockSpec((bb, ROW_TILE, d), lambda i, t, *_: (i, t + t_off, 0))]
    return [pl.BlockSpec((bb, ROW_TILE, d), lambda i, t, *_: (i, 0, 0)),
            pl.BlockSpec((bb, ROW_TILE, d), lambda i, t, *_: (i, jnp.maximum(t - 1, 0), 0))]


def _stream_tile(refs, is_ctx):
    if len(refs) == 1:
        return refs[0][...]
    return jnp.where(is_ctx, refs[0][...], refs[1][...])


def _in_kernel(*refs, ctx_rule, n_src):
    src, (modx_ref, modc_ref, g_ref, w_ref, o_ref) = refs[:n_src], refs[n_src:]
    bb, tm, d = src[0].shape
    is_ctx = ctx_rule(pl.program_id(1))
    mod = _pick_mod(modx_ref, modc_ref, is_ctx)
    h = _norm_mod(_stream_tile(src, is_ctx), g_ref[...], mod[:, 0:1, :], mod[:, 1:2, :])
    o = _bdot(h.reshape(bb * tm, d), w_ref[...])
    o_ref[...] = o.reshape(bb, tm, -1).astype(o_ref.dtype)


def _in_proj(srcs, modx, modc, g, w, *, t_off, n_t, ctx_rule, bb=2):
    b, _, d = srcs[0].shape
    n = w.shape[1]
    return pl.pallas_call(
        functools.partial(_in_kernel, ctx_rule=ctx_rule, n_src=len(srcs)),
        out_shape=jax.ShapeDtypeStruct((b, n_t * ROW_TILE, n), F32),
        grid=(b // bb, n_t),
        in_specs=_stream_specs(srcs, bb, t_off) + [
            pl.BlockSpec((bb, 6, d), lambda i, t: (i, 0, 0)),
            pl.BlockSpec((1, 6, d), lambda i, t: (0, 0, 0)),
            pl.BlockSpec((1, d), lambda i, t: (0, 0)),
            pl.BlockSpec((d, n), lambda i, t: (0, 0))],
        out_specs=pl.BlockSpec((bb, ROW_TILE, n), lambda i, t: (i, t, 0)),
        compiler_params=_cparams("parallel", "arbitrary"),
        name="in_proj",
    )(*srcs, modx, modc, g.reshape(1, d), w)


def _never_ctx(t):
    return t < 0


def _always_ctx(t):
    return t >= 0


def _s5_tables(lam_re, lam_im, log_dt, b_re, b_im, c_re, c_im, d_skip):
    t_len, n_st, p_ch = S5_CHUNK, S5_STATE, S5_GROUP_CH
    groups = lam_re.shape[1]
    dt = jnp.exp(log_dt.astype(F32))[..., None]
    lr = lam_re.astype(F32)
    li = lam_im.astype(F32)
    mag = jnp.exp(lr * dt)
    ar = mag * jnp.cos(li * dt)
    ai = mag * jnp.sin(li * dt)
    nr = ar - 1.0
    den = lr * lr + li * li
    kr = (nr * lr + ai * li) / den
    ki = (ai * lr - nr * li) / den
    br = b_re.astype(F32)
    bi = b_im.astype(F32)
    bbr = kr[..., None] * br - ki[..., None] * bi
    bbi = kr[..., None] * bi + ki[..., None] * br
    cr = c_re.astype(F32)
    ci = c_im.astype(F32)
    pr = [jnp.ones_like(ar)]
    pi = [jnp.zeros_like(ai)]
    for _ in range(t_len):
        pr.append(pr[-1] * ar - pi[-1] * ai)
        pi.append(pr[-2] * ai + pi[-1] * ar)
    pr = jnp.stack(pr)
    pi = jnp.stack(pi)
    er = cr[None] * pr[:, :, :, None, :] - ci[None] * pi[:, :, :, None, :]
    ei = cr[None] * pi[:, :, :, None, :] + ci[None] * pr[:, :, :, None, :]
    kern = (jnp.einsum('jdgqn,dgnp->jdgqp', er, bbr, precision=HIGHEST)
            - jnp.einsum('jdgqn,dgnp->jdgqp', ei, bbi, precision=HIGHEST))
    s_idx = np.arange(t_len)[:, None]
    t_idx = np.arange(t_len)[None, :]
    lag_f = np.clip(t_idx - s_idx, 0, t_len - 1)
    lag_b = np.clip(s_idx - t_idx, 0, t_len - 1)
    kf = kern[:, 0][lag_f]
    kb = kern[:, 1][lag_b]
    mask_f = jnp.asarray(s_idx <= t_idx, F32)[:, :, None, None, None]
    mask_b = jnp.asarray(s_idx >= t_idx, F32)[:, :, None, None, None]
    dmat = jnp.eye(p_ch, dtype=F32)[None] * d_skip.astype(F32).reshape(groups, 1, p_ch)
    eye_t = jnp.asarray(s_idx == t_idx, F32)[:, :, None, None, None]
    full = kf * mask_f + kb * mask_b + eye_t * dmat[None, None]
    toep = full.transpose(2, 0, 4, 1, 3).reshape(groups, t_len * p_ch, t_len * p_ch)
    def drive(pw_r, pw_i, d):
        re = pw_r[..., None] * bbr[d][None] - pw_i[..., None] * bbi[d][None]
        im = pw_r[..., None] * bbi[d][None] + pw_i[..., None] * bbr[d][None]
        return re.transpose(1, 0, 3, 2), im.transpose(1, 0, 3, 2)
    f_re, f_im = drive(pr[:t_len, 0][::-1], pi[:t_len, 0][::-1], 0)
    b_re2, b_im2 = drive(pr[:t_len, 1], pi[:t_len, 1], 1)
    wst = jnp.concatenate([f_re, f_im, b_re2, b_im2, f_im, f_re, b_im2, b_re2],
                          axis=-1).reshape(groups, t_len * p_ch, 8 * n_st)
    def read(e_r, e_i):
        return e_r.transpose(1, 3, 0, 2), (-e_i).transpose(1, 3, 0, 2)
    of_re, of_im = read(er[1:, 0], ei[1:, 0])
    ob_re, ob_im = read(er[1:, 1][::-1], ei[1:, 1][::-1])
    wout = jnp.concatenate([of_re, of_im, ob_re, ob_im], axis=1).reshape(groups, 4 * n_st, t_len * p_ch)
    a_r = pr[t_len]
    a_i = pi[t_len]
    adec = jnp.stack([jnp.concatenate([a_r[0], a_r[0]], -1), jnp.concatenate([-a_i[0], a_i[0]], -1),
                      jnp.concatenate([a_r[1], a_r[1]], -1), jnp.concatenate([-a_i[1], a_i[1]], -1)], axis=1)
    return toep.astype(BF16), wst.astype(BF16), wout.astype(BF16), adec


def _s5_kernel(u_ref, toep_ref, wst_ref, wout_ref, a_ref, y_ref, s_scr, h_scr, y_scr, *, nb, n_ctx, n_chunks,
               rows_blk):
    rows = u_ref.shape[1]
    n2 = 2 * S5_STATE
    toep = toep_ref[0]
    wst = wst_ref[0]
    for r0 in range(0, rows, rows_blk):
        u = u_ref[0, r0:r0 + rows_blk, :].astype(BF16)
        y_scr[r0:r0 + rows_blk, :] = jnp.dot(u, toep, preferred_element_type=F32)
        s_scr[r0:r0 + rows_blk, :] = jnp.dot(u, wst, preferred_element_type=F32)
    a = a_ref[0]
    af1, af2, ab1, ab2 = a[0:1], a[1:2], a[2:3], a[3:4]

    def step(i, carry):
        pf, qf, pb, qb = carry
        cb = jnp.where(i < n_ctx, n_ctx - 1 - i, n_chunks - 1 - (i - n_ctx))
        rf = pl.multiple_of(i * nb, nb)
        rb = pl.multiple_of(cb * nb, nb)
        h_scr[pl.ds(rf, nb), 0:n2] = pf
        h_scr[pl.ds(rb, nb), n2:2 * n2] = pb
        spf = s_scr[pl.ds(rf, nb), 0:n2]
        spb = s_scr[pl.ds(rb, nb), n2:2 * n2]
        sqf = s_scr[pl.ds(rf, nb), 2 * n2:3 * n2]
        sqb = s_scr[pl.ds(rb, nb), 3 * n2:4 * n2]
        return (af1 * pf + af2 * qf + spf, af1 * qf - af2 * pf + sqf,
                ab1 * pb + ab2 * qb + spb, ab1 * qb - ab2 * pb + sqb)

    zero = jnp.zeros((nb, n2), F32)
    lax.fori_loop(0, n_chunks, step, (zero, zero, zero, zero))
    wout = wout_ref[0]
    for r0 in range(0, rows, rows_blk):
        h = h_scr[r0:r0 + rows_blk, :].astype(BF16)
        y = y_scr[r0:r0 + rows_blk, :] + jnp.dot(h, wout, preferred_element_type=F32)
        y_ref[0, r0:r0 + rows_blk, :] = y.astype(y_ref.dtype)


def _s5_scan(u, tables, n_ctx_tok):
    toep, wst, wout, adec = tables
    b, s, width = u.shape
    groups = width // S5_GROUP_CH
    n_chunks = s // S5_CHUNK
    cw = S5_CHUNK * S5_GROUP_CH
    rows = n_chunks * b
    ug = u.reshape(b, n_chunks, S5_CHUNK, groups, S5_GROUP_CH).transpose(3, 1, 0, 2, 4).reshape(groups, rows, cw)
    rows_blk = math.gcd(rows, 512)
    y = pl.pallas_call(
        functools.partial(_s5_kernel, nb=b, n_ctx=n_ctx_tok // S5_CHUNK, n_chunks=n_chunks, rows_blk=rows_blk),
        out_shape=jax.ShapeDtypeStruct((groups, rows, cw), BF16),
        grid=(groups,),
        in_specs=[pl.BlockSpec((1, rows, cw), lambda g: (g, 0, 0)),
                  pl.BlockSpec((1, cw, cw), lambda g: (g, 0, 0)),
                  pl.BlockSpec((1, cw, 8 * S5_STATE), lambda g: (g, 0, 0)),
                  pl.BlockSpec((1, 4 * S5_STATE, cw), lambda g: (g, 0, 0)),
                  pl.BlockSpec((1, 4, 2 * S5_STATE), lambda g: (g, 0, 0))],
        out_specs=pl.BlockSpec((1, rows, cw), lambda g: (g, 0, 0)),
        scratch_shapes=[pltpu.VMEM((rows, 8 * S5_STATE), F32), pltpu.VMEM((rows, 4 * S5_STATE), F32),
                        pltpu.VMEM((rows, cw), F32)],
        compiler_params=_cparams("parallel"),
        name="s5_scan",
    )(ug, toep, wst, wout, adec)
    return y.reshape(groups, n_chunks, b, S5_CHUNK, S5_GROUP_CH).transpose(2, 1, 3, 0, 4).reshape(b, s, width)


def _head_norm(z, gain, avg):
    sq = z * z
    hi = sq.astype(BF16)
    lo = (sq - hi.astype(F32)).astype(BF16)
    w = avg[:z.shape[1], :z.shape[1]]
    ms = jnp.dot(hi, w, preferred_element_type=F32) + jnp.dot(lo, w, preferred_element_type=F32)
    return z * lax.rsqrt(ms + EPS) * gain


def _rope(z, cos, sin):
    lane = lax.broadcasted_iota(jnp.int32, z.shape, 1)
    first = (lane % (2 * ROPE_FREQS)) < ROPE_FREQS
    width = z.shape[1]
    partner = jnp.where(first, pltpu.roll(z, width - ROPE_FREQS, 1), pltpu.roll(z, ROPE_FREQS, 1))
    return z * cos + partner * sin


def _in_qkv_kernel(*refs, n_src, s5_w, n_q, n_kv):
    src = refs[:n_src]
    (modx_ref, modc_ref, g_ref, w_ref, cos_ref, sin_ref, qg_ref, kg_ref, avg_ref,
     u_ref, q_ref, k_ref, v_ref) = refs[n_src:]
    bb, tm, d = src[0].shape
    dq = n_q * HEAD_DIM
    dk = n_kv * HEAD_DIM
    is_ctx = pl.program_id(1) == 0
    mod = _pick_mod(modx_ref, modc_ref, is_ctx)
    h = _norm_mod(_stream_tile(src, is_ctx), g_ref[...], mod[:, 0:1, :], mod[:, 1:2, :])
    o = _bdot(h.reshape(bb * tm, d), w_ref[...])
    u_ref[...] = o[:, 0:s5_w].reshape(bb, tm, s5_w).astype(u_ref.dtype)
    avg = avg_ref[...]
    for bi in range(bb):
        p = o[bi * tm:(bi + 1) * tm]
        q = _rope(_head_norm(p[:, s5_w:s5_w + dq], qg_ref[...], avg), cos_ref[...], sin_ref[...])
        k = _rope(_head_norm(p[:, s5_w + dq:s5_w + dq + dk], kg_ref[:, :dk], avg), cos_ref[:, :dk], sin_ref[:, :dk])
        v = p[:, s5_w + dq + dk:s5_w + dq + 2 * dk]
        q = q * (HEAD_DIM ** -0.5)
        for hd in range(n_q):
            q_ref[bi, hd] = q[:, hd * HEAD_DIM:(hd + 1) * HEAD_DIM].astype(BF16)
        for hd in range(n_kv):
            k_ref[bi, hd] = k[:, hd * HEAD_DIM:(hd + 1) * HEAD_DIM].astype(BF16)
            v_ref[bi, hd] = v[:, hd * HEAD_DIM:(hd + 1) * HEAD_DIM].astype(BF16)


def _rope_tables(seq, n_ctx_tok, width):
    t = np.arange(seq)
    pos = np.stack([t // GRID_W, t % GRID_W], axis=-1).astype(np.float32)
    inv = (ROPE_BASE ** (-np.arange(ROPE_FREQS, dtype=np.float32) / ROPE_FREQS)).astype(np.float32)
    ang = jnp.asarray(pos)[:, :, None] * jnp.asarray(inv)
    cos = jnp.cos(ang)
    sin = jnp.sin(ang)
    cos_h = jnp.concatenate([cos, cos], axis=-1).reshape(seq, HEAD_DIM)
    sin_h = jnp.concatenate([-sin, sin], axis=-1).reshape(seq, HEAD_DIM)
    cos_h = jnp.concatenate([jnp.ones((n_ctx_tok, HEAD_DIM), F32), cos_h], axis=0)
    sin_h = jnp.concatenate([jnp.zeros((n_ctx_tok, HEAD_DIM), F32), sin_h], axis=0)
    reps = width // HEAD_DIM
    return jnp.tile(cos_h, (1, reps)), jnp.tile(sin_h, (1, reps))


def _in_proj_qkv(srcs, modx, modc, g, w, q_g, k_g, n_ctx_tok, *, s5_w, n_q, n_kv, bb=2):
    b, _, d = srcs[0].shape
    s = sum(a.shape[1] for a in srcs)
    n = w.shape[1]
    dq = n_q * HEAD_DIM
    cos, sin = _rope_tables(s - n_ctx_tok, n_ctx_tok, dq)
    avg = jnp.asarray(np.kron(np.eye(n_q, dtype=np.float32), np.full((HEAD_DIM, HEAD_DIM), 1.0 / HEAD_DIM, np.float32)), BF16)
    qg = jnp.tile(q_g.astype(F32), n_q).reshape(1, dq)
    kg = jnp.tile(k_g.astype(F32), n_q).reshape(1, dq)
    tm = ROW_TILE

    def heads(n_h):
        return pl.BlockSpec((bb, n_h, tm, HEAD_DIM), lambda i, t: (i, 0, t, 0))

    return pl.pallas_call(
        functools.partial(_in_qkv_kernel, n_src=len(srcs), s5_w=s5_w, n_q=n_q, n_kv=n_kv),
        out_shape=(jax.ShapeDtypeStruct((b, s, s5_w), BF16),
                   jax.ShapeDtypeStruct((b, n_q, s, HEAD_DIM), BF16),
                   jax.ShapeDtypeStruct((b, n_kv, s, HEAD_DIM), BF16),
                   jax.ShapeDtypeStruct((b, n_kv, s, HEAD_DIM), BF16)),
        grid=(b // bb, s // tm),
        in_specs=_stream_specs(srcs, bb) + [
            pl.BlockSpec((bb, 6, d), lambda i, t: (i, 0, 0)),
            pl.BlockSpec((1, 6, d), lambda i, t: (0, 0, 0)),
            pl.BlockSpec((1, d), lambda i, t: (0, 0)),
            pl.BlockSpec((d, n), lambda i, t: (0, 0)),
            pl.BlockSpec((tm, dq), lambda i, t: (t, 0)),
            pl.BlockSpec((tm, dq), lambda i, t: (t, 0)),
            pl.BlockSpec((1, dq), lambda i, t: (0, 0)),
            pl.BlockSpec((1, dq), lambda i, t: (0, 0)),
            pl.BlockSpec((dq, dq), lambda i, t: (0, 0))],
        out_specs=(pl.BlockSpec((bb, tm, s5_w), lambda i, t: (i, t, 0)), heads(n_q), heads(n_kv), heads(n_kv)),
        compiler_params=_cparams("parallel", "arbitrary"),
        name="in_proj_qkv",
    )(*srcs, modx, modc, g.reshape(1, d), w, cos, sin, qg, kg, avg)


def _softmax_pv(s, v):
    m = jnp.max(s, axis=-1, keepdims=True)
    p = jnp.exp(s - m)
    l = jnp.sum(p, axis=-1, keepdims=True)
    return jnp.dot(p.astype(BF16), v, preferred_element_type=F32) / l


def _gqa_kernel(q_ref, k_ref, v_ref, o_ref, *, n_kv, grp, n_ctx_tok):
    tq = q_ref.shape[2]
    nt = (((1,), (1,)), ((), ()))

    def attend(n_keys):
        for h in range(n_kv):
            q2 = q_ref[0, h * grp:(h + 1) * grp].reshape(grp * tq, HEAD_DIM)
            s = lax.dot_general(q2, k_ref[0, h, 0:n_keys, :], nt, preferred_element_type=F32)
            o = _softmax_pv(s, v_ref[0, h, 0:n_keys, :])
            for j in range(grp):
                c0 = (h * grp + j) * HEAD_DIM
                o_ref[0, :, c0:c0 + HEAD_DIM] = o[j * tq:(j + 1) * tq].astype(o_ref.dtype)

    @pl.when(pl.program_id(1) == 0)
    def _():
        attend(n_ctx_tok)

    @pl.when(pl.program_id(1) != 0)
    def _():
        attend(k_ref.shape[2])


def _gqa_attention(q, k, v, n_ctx_tok):
    b, n_q, s, _ = q.shape
    n_kv = k.shape[1]
    tq = ROW_TILE
    return pl.pallas_call(
        functools.partial(_gqa_kernel, n_kv=n_kv, grp=n_q // n_kv, n_ctx_tok=n_ctx_tok),
        out_shape=jax.ShapeDtypeStruct((b, s, n_q * HEAD_DIM), BF16),
        grid=(b, s // tq),
        in_specs=[pl.BlockSpec((1, n_q, tq, HEAD_DIM), lambda i, t: (i, 0, t, 0)),
                  pl.BlockSpec((1, n_kv, s, HEAD_DIM), lambda i, t: (i, 0, 0, 0)),
                  pl.BlockSpec((1, n_kv, s, HEAD_DIM), lambda i, t: (i, 0, 0, 0))],
        out_specs=pl.BlockSpec((1, tq, n_q * HEAD_DIM), lambda i, t: (i, t, 0)),
        compiler_params=_cparams("parallel", "arbitrary"),
        name="gqa_attention",
    )(q, k, v)


def _gelu_tanh(y):
    return 0.5 * y * (1.0 + jnp.tanh(math.sqrt(2.0 / math.pi) * (y + 0.044715 * (y * y * y))))


def _mix_even_kernel(*refs, n_src):
    src, (y_ref, att_ref, modx_ref, modc_ref, wglu_ref, wa_ref, wb_ref, o_ref) = refs[:n_src], refs[n_src:]
    bb, tm, d = src[0].shape
    is_ctx = pl.program_id(1) == 0
    mod = _pick_mod(modx_ref, modc_ref, is_ctx)
    y = y_ref[...].reshape(bb * tm, -1).astype(F32)
    g = _gelu_tanh(y)
    a = g * jax.nn.sigmoid(_bdot(g, wglu_ref[...]))
    ox = _bdot(a, wa_ref[...]) + jnp.dot(att_ref[...].reshape(bb * tm, -1), wb_ref[...], preferred_element_type=F32)
    o_ref[...] = _stream_tile(src, is_ctx) + mod[:, 2:3, :] * ox.reshape(bb, tm, d)


def _mix_even(srcs, y_s5, att, modx, modc, w_glu, w_out, bb=2):
    b, s, w5 = y_s5.shape
    d = srcs[0].shape[2]
    wa = att.shape[2]
    tm = ROW_TILE
    return pl.pallas_call(
        functools.partial(_mix_even_kernel, n_src=len(srcs)),
        out_shape=jax.ShapeDtypeStruct((b, s, d), F32),
        grid=(b // bb, s // tm),
        in_specs=_stream_specs(srcs, bb) + [
                  pl.BlockSpec((bb, tm, w5), lambda i, t: (i, t, 0)),
                  pl.BlockSpec((bb, tm, wa), lambda i, t: (i, t, 0)),
                  pl.BlockSpec((bb, 6, d), lambda i, t: (i, 0, 0)),
                  pl.BlockSpec((1, 6, d), lambda i, t: (0, 0, 0)),
                  pl.BlockSpec((w5, w5), lambda i, t: (0, 0)),
                  pl.BlockSpec((w5, d), lambda i, t: (0, 0)),
                  pl.BlockSpec((wa, d), lambda i, t: (1, 0))],
        out_specs=pl.BlockSpec((bb, tm, d), lambda i, t: (i, t, 0)),
        compiler_params=_cparams("parallel", "arbitrary"),
        name="mix_even",
    )(*srcs, y_s5, att, modx, modc, w_glu, w_out, w_out)


def _ffn_kernel(x_ref, modx_ref, modc_ref, g_ref, w1_ref, w3_ref, w2_ref, o_ref, h_scr, acc_scr):
    bb, tm, d = x_ref.shape
    j = pl.program_id(2)
    mod = _pick_mod(modx_ref, modc_ref, pl.program_id(1) == 0)

    @pl.when(j == 0)
    def _():
        h = _norm_mod(x_ref[...], g_ref[...], mod[:, 3:4, :], mod[:, 4:5, :])
        h_scr[...] = h.reshape(bb * tm, d).astype(BF16)
        acc_scr[...] = jnp.zeros_like(acc_scr)

    h = h_scr[...]
    a = jnp.dot(h, w1_ref[...], preferred_element_type=F32)
    g = jnp.dot(h, w3_ref[...], preferred_element_type=F32)
    acc_scr[...] += _bdot(_silu(a) * g, w2_ref[...])

    @pl.when(j == pl.num_programs(2) - 1)
    def _():
        o_ref[...] = x_ref[...] + mod[:, 5:6, :] * acc_scr[...].reshape(bb, tm, d)


def _ffn(stream, modx, modc, g, w1, w3, w2, bb=4, tf=512):
    b, s, d = stream.shape
    f = w1.shape[1]
    tm = ROW_TILE
    return pl.pallas_call(
        _ffn_kernel,
        out_shape=jax.ShapeDtypeStruct((b, s, d), F32),
        grid=(b // bb, s // tm, f // tf),
        in_specs=[pl.BlockSpec((bb, tm, d), lambda i, t, j: (i, t, 0)),
                  pl.BlockSpec((bb, 6, d), lambda i, t, j: (i, 0, 0)),
                  pl.BlockSpec((1, 6, d), lambda i, t, j: (0, 0, 0)),
                  pl.BlockSpec((1, d), lambda i, t, j: (0, 0)),
                  pl.BlockSpec((d, tf), lambda i, t, j: (0, j)),
                  pl.BlockSpec((d, tf), lambda i, t, j: (0, j)),
                  pl.BlockSpec((tf, d), lambda i, t, j: (j, 0))],
        out_specs=pl.BlockSpec((bb, tm, d), lambda i, t, j: (i, t, 0)),
        scratch_shapes=[pltpu.VMEM((bb * tm, d), BF16), pltpu.VMEM((bb * tm, d), F32)],
        compiler_params=_cparams("parallel", "arbitrary", "arbitrary"),
        name="ffn_dense",
    )(stream, modx, modc, g.reshape(1, d), w1, w3, w2)


def _even_layer(srcs, modx, modc, n_ctx_tok, norm1_g, norm2_g, w_in, w_out, s5_params, w_glu, q_g, k_g,
                ffn_w1, ffn_w3, ffn_w2):
    s5_w = w_glu.shape[0]
    n_q = s5_w // HEAD_DIM
    n_kv = n_q // 2
    u, q, k, v = _in_proj_qkv(srcs, modx, modc, norm1_g, w_in.astype(BF16), q_g, k_g, n_ctx_tok,
                              s5_w=s5_w, n_q=n_q, n_kv=n_kv)
    y_s5 = _s5_scan(u, _s5_tables(*s5_params), n_ctx_tok)
    att = _gqa_attention(q, k, v, n_ctx_tok)
    stream = _mix_even(srcs, y_s5, att, modx, modc, w_glu.astype(BF16), w_out.astype(BF16))
    return _ffn(stream, modx, modc, norm2_g, ffn_w1.astype(BF16), ffn_w3.astype(BF16), ffn_w2.astype(BF16))


HYENA_BANDS = 16
CONV_TILE = 256


def _hy_filter_kernel(z_ref, w1_ref, b1_ref, w2_ref, b2_ref, w3_ref, freq_ref, decay_ref, t_ref, h_ref):
    f = freq_ref[...]
    h = jnp.sin(f * (jnp.dot(z_ref[...], w1_ref[...], preferred_element_type=F32, precision=HIGHEST) + b1_ref[...]))
    h = jnp.sin(f * (jnp.dot(h, w2_ref[...], preferred_element_type=F32, precision=HIGHEST) + b2_ref[...]))
    h = jnp.dot(h, w3_ref[...], preferred_element_type=F32, precision=HIGHEST)
    h = h * jnp.exp(-t_ref[...] * jnp.abs(decay_ref[...]))
    h_ref[...] = h / (jnp.sum(jnp.abs(h), axis=0, keepdims=True) + EPS)


def _hy_filters(seq, f_w1, f_b1, f_w2, f_b2, f_w3, f_freq, f_decay):
    k = np.arange(seq, dtype=np.float32)
    t = k / max(seq - 1, 1)
    bands = np.linspace(1e-4, HYENA_BANDS - 1, HYENA_BANDS, dtype=np.float32)
    ang = jnp.asarray(np.float32(2.0 * math.pi / seq) * k[:, None] * bands[None, :])
    pos_dim, hidden = f_w1.shape
    zpad = 128
    z = jnp.concatenate([jnp.asarray(t)[:, None], jnp.cos(ang), -jnp.sin(ang),
                         jnp.zeros((seq, zpad - pos_dim), F32)], axis=-1)
    w1 = jnp.concatenate([f_w1.astype(F32), jnp.zeros((zpad - pos_dim, hidden), F32)], axis=0)
    n = f_w3.shape[1]
    tc = 512
    return pl.pallas_call(
        _hy_filter_kernel,
        out_shape=jax.ShapeDtypeStruct((seq, n), F32),
        grid=(n // tc,),
        in_specs=[pl.BlockSpec((seq, zpad), lambda j: (0, 0)),
                  pl.BlockSpec((zpad, hidden), lambda j: (0, 0)),
                  pl.BlockSpec((1, hidden), lambda j: (0, 0)),
                  pl.BlockSpec((hidden, hidden), lambda j: (0, 0)),
                  pl.BlockSpec((1, hidden), lambda j: (0, 0)),
                  pl.BlockSpec((hidden, tc), lambda j: (0, j)),
                  pl.BlockSpec((1, hidden), lambda j: (0, 0)),
                  pl.BlockSpec((1, tc), lambda j: (0, j)),
                  pl.BlockSpec((seq, 1), lambda j: (0, 0))],
        out_specs=pl.BlockSpec((seq, tc), lambda j: (0, j)),
        compiler_params=_cparams("arbitrary"),
        name="hyena_filter",
    )(z, w1, f_b1.reshape(1, hidden).astype(F32), f_w2.astype(F32), f_b2.reshape(1, hidden).astype(F32),
      f_w3.astype(F32), f_freq.reshape(1, hidden).astype(F32), f_decay.reshape(1, n).astype(F32),
      jnp.asarray(t)[:, None])


def _dft_tables(seq):
    idx = np.arange(seq, dtype=np.int64)
    m = jnp.asarray(((idx[:, None] * idx[None, :]) % (2 * seq)).astype(np.int32))
    ang = m.astype(F32) * np.float32(math.pi / seq)
    alt = jnp.asarray((1.0 - 2.0 * (idx % 2)).astype(np.float32))[:, None]
    return jnp.cos(ang).astype(BF16), jnp.sin(ang).astype(BF16), alt


def _split_bf16(a):
    hi = a.astype(BF16)
    return hi, (a - hi.astype(F32)).astype(BF16)


def _hy_block_kernels(h):
    seq, n2 = h.shape
    n = n2 // 2
    hh = seq // 2
    hf, hb = h[:, :n], h[:, n:]
    zero = jnp.zeros((1, n), h.dtype)
    t0_lo = hf[:hh].at[0].add(hb[0])
    t0_hi = jnp.concatenate([zero, hb[1:hh][::-1]], axis=0)
    t1_lo = hf[hh:]
    t1_hi = jnp.concatenate([zero, hf[1:hh]], axis=0)
    tm_lo = hb[1:hh + 1][::-1]
    tm_hi = jnp.concatenate([zero, hb[hh + 1:][::-1]], axis=0)
    return jnp.stack([t0_lo, t0_hi, t1_lo, t1_hi, tm_lo, tm_hi])


def _hy_spec_kernel(lo_ref, hi_ref, cos_ref, sin_ref, alt_ref, wk_ref, sre_ref, sim_ref, sny_ref, *, half_len):
    c = cos_ref[...]
    s = sin_ref[...]
    alt = alt_ref[...]
    wk = wk_ref[...]

    def dft(tab, a):
        a_hi, a_lo = _split_bf16(a)
        return jnp.dot(tab, a_hi, preferred_element_type=F32) + jnp.dot(tab, a_lo, preferred_element_type=F32)

    lo = lo_ref[0]
    hi = hi_ref[0]
    sre_ref[0] = wk * (dft(c, lo) + alt * dft(c, hi))
    sim_ref[0] = -wk * (dft(s, lo) + alt * dft(s, hi))
    sny_ref[0] = jnp.sum(alt * (lo + hi), axis=0, keepdims=True) * (0.5 / half_len)


def _hy_spectrum(kernels, cos, sin, alt):
    _, hh, n = kernels.shape
    tc = 128
    nb = n // tc
    wk = jnp.full((hh, 1), 1.0 / hh, F32).at[0, 0].set(0.5 / hh)
    return pl.pallas_call(
        functools.partial(_hy_spec_kernel, half_len=hh),
        out_shape=(jax.ShapeDtypeStruct((3, hh, n), F32), jax.ShapeDtypeStruct((3, hh, n), F32),
                   jax.ShapeDtypeStruct((3, 1, n), F32)),
        grid=(3, nb),
        in_specs=[pl.BlockSpec((1, hh, tc), lambda k, j: (2 * k, 0, j)),
                  pl.BlockSpec((1, hh, tc), lambda k, j: (2 * k + 1, 0, j)),
                  pl.BlockSpec((hh, hh), lambda k, j: (0, 0)),
                  pl.BlockSpec((hh, hh), lambda k, j: (0, 0)),
                  pl.BlockSpec((hh, 1), lambda k, j: (0, 0)),
                  pl.BlockSpec((hh, 1), lambda k, j: (0, 0))],
        out_specs=(pl.BlockSpec((1, hh, tc), lambda k, j: (k, 0, j)), pl.BlockSpec((1, hh, tc), lambda k, j: (k, 0, j)),
                   pl.BlockSpec((1, 1, tc), lambda k, j: (k, 0, j))),
        compiler_params=_cparams("arbitrary", "arbitrary"),
        name="hyena_spectrum",
    )(kernels, kernels, cos, sin, alt, wk)


def _hy_conv_kernel(z_ref, g_ref, cwz_ref, cbz_ref, cwg_ref, cbg_ref, cos_ref, sin_ref, alt_ref,
                    sre_ref, sim_ref, sny_ref, d_ref, o_ref, z_scr, zz_scr, yre_scr, yim_scr, *, conv_z, blk):
    seq = z_ref.shape[1]
    hh = seq // 2
    tc = z_ref.shape[2]
    row = lax.broadcasted_iota(jnp.int32, (blk, 1), 0)

    def short_conv(x_ref, w_ref, b_ref, l0):
        x = x_ref[0, l0:l0 + blk, :]
        w = w_ref[...]
        top = x_ref[0, l0 - 1:l0, :] if l0 > 0 else jnp.zeros((1, tc), F32)
        bot = x_ref[0, l0 + blk:l0 + blk + 1, :] if l0 + blk < seq else jnp.zeros((1, tc), F32)
        prev = jnp.where(row == 0, top, pltpu.roll(x, 1, 0))
        nxt = jnp.where(row == blk - 1, bot, pltpu.roll(x, blk - 1, 0))
        return prev * w[0:1] + x * w[1:2] + nxt * w[2:3] + b_ref[...]

    zny = [jnp.zeros((1, tc), F32), jnp.zeros((1, tc), F32)]
    for l0 in range(0, seq, blk):
        z = short_conv(z_ref, cwz_ref, cbz_ref, l0) if conv_z else z_ref[0, l0:l0 + blk, :]
        a, r0 = divmod(l0, hh)
        z_scr[l0:l0 + blk, :] = z
        zz_scr[r0:r0 + blk, a * tc:(a + 1) * tc] = z.astype(BF16)
        zny[a] = zny[a] + jnp.sum(alt_ref[r0:r0 + blk, :] * z, axis=0, keepdims=True)

    def spec_mul(zc, zs, kern, k0):
        sre = sre_ref[kern, k0:k0 + blk, :]
        sim = sim_ref[kern, k0:k0 + blk, :]
        return zc * sre + zs * sim, zc * sim - zs * sre

    for k0 in range(0, hh, blk):
        zc = jnp.dot(cos_ref[k0:k0 + blk, :], zz_scr[...], preferred_element_type=F32)
        zs = jnp.dot(sin_ref[k0:k0 + blk, :], zz_scr[...], preferred_element_type=F32)
        zc0, zc1, zs0, zs1 = zc[:, :tc], zc[:, tc:], zs[:, :tc], zs[:, tc:]
        for a, (k_first, k_second) in enumerate(((0, 2), (1, 0))):
            re0, im0 = spec_mul(zc0, zs0, k_first, k0)
            re1, im1 = spec_mul(zc1, zs1, k_second, k0)
            yre_scr[k0:k0 + blk, a * tc:(a + 1) * tc] = (re0 + re1).astype(BF16)
            yim_scr[k0:k0 + blk, a * tc:(a + 1) * tc] = (im0 + im1).astype(BF16)
    yny = [sny_ref[0] * zny[0] + sny_ref[2] * zny[1], sny_ref[1] * zny[0] + sny_ref[0] * zny[1]]
    d = d_ref[0]
    for r0 in range(0, hh, blk):
        y = (jnp.dot(cos_ref[r0:r0 + blk, :], yre_scr[...], preferred_element_type=F32)
             - jnp.dot(sin_ref[r0:r0 + blk, :], yim_scr[...], preferred_element_type=F32))
        for a in range(2):
            l0 = a * hh + r0
            ya = y[:, a * tc:(a + 1) * tc] + alt_ref[r0:r0 + blk, :] * yny[a]
            gate = short_conv(g_ref, cwg_ref, cbg_ref, l0)
            o_ref[0, l0:l0 + blk, :] = gate * (ya + d * z_scr[l0:l0 + blk, :])


def _hy_order(z_src, z_off, g_src, g_off, conv_w, conv_b, cos, sin, alt, sre, sim, sny, hy_d, order, width, conv_z):
    b, seq, _ = z_src.shape
    hh = seq // 2
    tc = CONV_TILE
    nb = width // tc
    zo, go = z_off // tc, g_off // tc

    def col(off):
        return lambda j, i: (0, j + off)

    return pl.pallas_call(
        functools.partial(_hy_conv_kernel, conv_z=conv_z, blk=512),
        out_shape=jax.ShapeDtypeStruct((b, seq, width), F32),
        grid=(nb, b),
        in_specs=[pl.BlockSpec((1, seq, tc), lambda j, i: (i, 0, j + zo)),
                  pl.BlockSpec((1, seq, tc), lambda j, i: (i, 0, j + go)),
                  pl.BlockSpec((3, tc), col(zo if conv_z else 0)), pl.BlockSpec((1, tc), col(zo if conv_z else 0)),
                  pl.BlockSpec((3, tc), col(go)), pl.BlockSpec((1, tc), col(go)),
                  pl.BlockSpec((hh, hh), lambda j, i: (0, 0)),
                  pl.BlockSpec((hh, hh), lambda j, i: (0, 0)),
                  pl.BlockSpec((hh, 1), lambda j, i: (0, 0)),
                  pl.BlockSpec((3, hh, tc), lambda j, i: (0, 0, j + order * nb)),
                  pl.BlockSpec((3, hh, tc), lambda j, i: (0, 0, j + order * nb)),
                  pl.BlockSpec((3, 1, tc), lambda j, i: (0, 0, j + order * nb)),
                  pl.BlockSpec((1, 1, tc), lambda j, i: (order, 0, j))],
        out_specs=pl.BlockSpec((1, seq, tc), lambda j, i: (i, 0, j)),
        scratch_shapes=[pltpu.VMEM((seq, tc), F32), pltpu.VMEM((hh, 2 * tc), BF16),
                        pltpu.VMEM((hh, 2 * tc), BF16), pltpu.VMEM((hh, 2 * tc), BF16)],
        compiler_params=_cparams("arbitrary", "arbitrary"),
        name="hyena_conv",
    )(z_src, g_src, conv_w, conv_b, conv_w, conv_b, cos, sin, alt, sre, sim, sny, hy_d.reshape(2, 1, width))


def _hy_conv(proj, conv_w, conv_b, cos, sin, alt, sre, sim, sny, hy_d, width):
    args = (conv_w, conv_b, cos, sin, alt, sre, sim, sny, hy_d)
    z1 = _hy_order(proj, 0, proj, width, *args, order=0, width=width, conv_z=True)
    return _hy_order(z1, 0, proj, 2 * width, *args, order=1, width=width, conv_z=False)


NA_WIN_ROWS = 8
NA_WIN_COLS = 16
NA_QROWS = ROW_TILE // GRID_W
NA_KTILES = 3


def _na_bias_tiles(rpb, rows):
    nqb = rows // NA_QROWS
    assert nqb >= 3 and rows >= NA_WIN_ROWS + NA_QROWS
    col = np.arange(GRID_W)
    col_start = np.clip(col - NA_WIN_COLS // 2, 0, GRID_W - NA_WIN_COLS)
    col_ok = (col[None, :] >= col_start[:, None]) & (col[None, :] < col_start[:, None] + NA_WIN_COLS)
    dc_idx = np.clip(col[None, :] - col[:, None] + NA_WIN_COLS - 1, 0, 2 * NA_WIN_COLS - 2)
    tiles = []
    for j in (0, 1, nqb - 1):
        kb0 = min(max(j - 1, 0), nqb - NA_KTILES)
        q_r = j * NA_QROWS + np.arange(NA_QROWS)
        k_r = kb0 * NA_QROWS + np.arange(NA_KTILES * NA_QROWS)
        r0 = np.clip(q_r - NA_WIN_ROWS // 2, 0, rows - NA_WIN_ROWS)
        row_ok = (k_r[None, :] >= r0[:, None]) & (k_r[None, :] < r0[:, None] + NA_WIN_ROWS)
        dr_idx = np.clip(k_r[None, :] - q_r[:, None] + NA_WIN_ROWS - 1, 0, 2 * NA_WIN_ROWS - 2)
        ok = row_ok[:, None, :, None] & col_ok[None, :, None, :]
        pick_r = jnp.asarray(dr_idx[:, :, None] == np.arange(2 * NA_WIN_ROWS - 1), F32)
        pick_c = jnp.asarray(dc_idx[:, :, None] == np.arange(2 * NA_WIN_COLS - 1), F32)
        by_row = jnp.einsum('hij,abi->habj', rpb.astype(F32), pick_r, precision=HIGHEST)
        bias = jnp.einsum('habj,cdj->hacbd', by_row, pick_c, precision=HIGHEST)
        tile = jnp.where(jnp.asarray(ok)[None], bias, -jnp.inf)
        tiles.append(tile.reshape(rpb.shape[0], ROW_TILE, NA_KTILES * ROW_TILE))
    return jnp.stack(tiles, axis=1)


def _na_kernel(q_ref, k0_ref, k1_ref, k2_ref, v0_ref, v1_ref, v2_ref, kc_ref, vc_ref, bias_ref, o_ref, *, n_heads):
    nt = (((1,), (1,)), ((), ()))
    q = (q_ref[0] * (HEAD_DIM ** -0.5)).astype(BF16)
    k = jnp.concatenate([k0_ref[0], k1_ref[0], k2_ref[0]], axis=0).astype(BF16)
    v = jnp.concatenate([v0_ref[0], v1_ref[0], v2_ref[0]], axis=0).astype(BF16)
    kc = kc_ref[0].astype(BF16)
    vc = vc_ref[0].astype(BF16)
    for h in range(n_heads):
        sl = slice(h * HEAD_DIM, (h + 1) * HEAD_DIM)
        s_loc = lax.dot_general(q[:, sl], k[:, sl], nt, preferred_element_type=F32) + bias_ref[h, 0]
        s_ctx = lax.dot_general(q[:, sl], kc[:, sl], nt, preferred_element_type=F32)
        m = jnp.maximum(jnp.max(s_loc, axis=-1, keepdims=True), jnp.max(s_ctx, axis=-1, keepdims=True))
        p_loc = jnp.exp(s_loc - m)
        p_ctx = jnp.exp(s_ctx - m)
        l = jnp.sum(p_loc, axis=-1, keepdims=True) + jnp.sum(p_ctx, axis=-1, keepdims=True)
        o = (jnp.dot(p_loc.astype(BF16), v[:, sl], preferred_element_type=F32)
             + jnp.dot(p_ctx.astype(BF16), vc[:, sl], preferred_element_type=F32)) / l
        o_ref[0, :, sl] = o.astype(o_ref.dtype)


def _na_attention(proj, proj_c, rpb, *, q_off, n_heads):
    b, seq, _ = proj.shape
    w = n_heads * HEAD_DIM
    tm = ROW_TILE
    nqb = seq // tm
    bias = _na_bias_tiles(rpb, seq // GRID_W)
    qc = q_off // w

    def kb0(j):
        return jnp.clip(j - 1, 0, nqb - NA_KTILES)

    def kv_spec(cb, off):
        return pl.BlockSpec((1, tm, w), lambda i, j: (i, kb0(j) + off, cb))

    def bias_type(j):
        return jnp.where(j == 0, 0, jnp.where(j == nqb - 1, 2, 1))

    return pl.pallas_call(
        functools.partial(_na_kernel, n_heads=n_heads),
        out_shape=jax.ShapeDtypeStruct((b, seq, w), BF16),
        grid=(b, nqb),
        in_specs=[pl.BlockSpec((1, tm, w), lambda i, j: (i, j, qc)),
                  kv_spec(qc + 1, 0), kv_spec(qc + 1, 1), kv_spec(qc + 1, 2),
                  kv_spec(qc + 2, 0), kv_spec(qc + 2, 1), kv_spec(qc + 2, 2),
                  pl.BlockSpec((1, proj_c.shape[1], w), lambda i, j: (i, 0, 0)),
                  pl.BlockSpec((1, proj_c.shape[1], w), lambda i, j: (i, 0, 1)),
                  pl.BlockSpec((n_heads, 1, tm, NA_KTILES * tm), lambda i, j: (0, bias_type(j), 0, 0))],
        out_specs=pl.BlockSpec((1, tm, w), lambda i, j: (i, j, 0)),
        compiler_params=_cparams("parallel", "arbitrary"),
        name="na_attention",
    )(proj, proj, proj, proj, proj, proj, proj, proj_c, proj_c, bias)


MOE_TOP_K = 2


def _mix_odd_kernel(x_ref, hy_ref, na_ref, modx_ref, g_ref, wa_ref, wb_ref, rt_ref, x1_ref, h2_ref, idx_ref, wt_ref,
                    *, n_e):
    bb, tm, d = x_ref.shape
    mod = modx_ref[...]
    ox = (_bdot(hy_ref[...].reshape(bb * tm, -1), wa_ref[...])
          + jnp.dot(na_ref[...].reshape(bb * tm, -1), wb_ref[...], preferred_element_type=F32))
    x1 = x_ref[...] + mod[:, 2:3, :] * ox.reshape(bb, tm, d)
    x1_ref[...] = x1
    h2 = _norm_mod(x1, g_ref[...], mod[:, 3:4, :], mod[:, 4:5, :])
    h2_ref[...] = _rows_to_tiles(h2.reshape(bb * tm, d)).reshape(h2_ref.shape)
    h_hi, h_lo = _split_bf16(h2.reshape(bb * tm, d))
    lg = (jnp.dot(h_hi, rt_ref[0], preferred_element_type=F32) + jnp.dot(h_lo, rt_ref[0], preferred_element_type=F32)
          + jnp.dot(h_hi, rt_ref[1], preferred_element_type=F32))
    eid = lax.broadcasted_iota(jnp.int32, lg.shape, 1)
    lg = jnp.where(eid < n_e, lg, -jnp.inf)
    m1 = jnp.max(lg, axis=1, keepdims=True)
    i1 = jnp.min(jnp.where(lg == m1, eid, n_e), axis=1, keepdims=True)
    lg2 = jnp.where(eid == i1, -jnp.inf, lg)
    m2 = jnp.max(lg2, axis=1, keepdims=True)
    i2 = jnp.min(jnp.where(lg2 == m2, eid, n_e), axis=1, keepdims=True)
    e2 = jnp.exp(m2 - m1)
    den = 1.0 + e2
    first = lax.broadcasted_iota(jnp.int32, (bb * tm, MOE_TOP_K), 1) == 0
    idx_ref[...] = jnp.where(first, i1, i2).reshape(bb, tm, MOE_TOP_K)
    wt_ref[...] = jnp.where(first, 1.0 / den, e2 / den).reshape(bb, tm, MOE_TOP_K)


ROUTER_LANES = 128


def _mix_odd(stream, t_off, o_hy, o_na, modx, g, w_out, router, bb=2):
    b, _, d = stream.shape
    seq = o_hy.shape[1]
    wh = o_hy.shape[2]
    wn = o_na.shape[2]
    n_e = router.shape[1]
    tm = ROW_TILE
    n_t = seq // tm
    router_pad = jnp.concatenate([router.astype(F32), jnp.zeros((d, ROUTER_LANES - n_e), F32)], axis=1)
    router_pad = jnp.stack(_split_bf16(router_pad))
    return pl.pallas_call(
        functools.partial(_mix_odd_kernel, n_e=n_e),
        out_shape=(jax.ShapeDtypeStruct((b, seq, d), F32), jax.ShapeDtypeStruct((b, seq) + _tile_shape(d), F32),
                   jax.ShapeDtypeStruct((b, seq, MOE_TOP_K), jnp.int32), jax.ShapeDtypeStruct((b, seq, MOE_TOP_K), F32)),
        grid=(b // bb, n_t),
        in_specs=[pl.BlockSpec((bb, tm, d), lambda i, t: (i, t + t_off, 0)),
                  pl.BlockSpec((bb, tm, wh), lambda i, t: (i, t, 0)),
                  pl.BlockSpec((bb, tm, wn), lambda i, t: (i, t, 0)),
                  pl.BlockSpec((bb, 6, d), lambda i, t: (i, 0, 0)),
                  pl.BlockSpec((1, d), lambda i, t: (0, 0)),
                  pl.BlockSpec((wh, d), lambda i, t: (0, 0)),
                  pl.BlockSpec((wn, d), lambda i, t: (1, 0)),
                  pl.BlockSpec((2, d, ROUTER_LANES), lambda i, t: (0, 0, 0))],
        out_specs=(pl.BlockSpec((bb, tm, d), lambda i, t: (i, t, 0)),
                   pl.BlockSpec((bb, tm) + _tile_shape(d), lambda i, t: (i, t, 0, 0)),
                   pl.BlockSpec((bb, tm, MOE_TOP_K), lambda i, t: (i, t, 0)),
                   pl.BlockSpec((bb, tm, MOE_TOP_K), lambda i, t: (i, t, 0))),
        compiler_params=_cparams("parallel", "arbitrary"),
        name="mix_odd_router",
    )(stream, o_hy, o_na, modx, g.reshape(1, d), w_out, w_out, router_pad)


MOE_BLOCK = 1024
MOE_FF_TILE = 896


def _moe_plan(idx, n_experts):
    t, k = idx.shape
    e_flat = idx.reshape(-1)
    onehot = (e_flat[:, None] == jnp.arange(n_experts)[None, :]).astype(jnp.int32)
    rank = jnp.sum((jnp.cumsum(onehot, axis=0) - onehot) * onehot, axis=1)
    counts = jnp.sum(onehot, axis=0)
    padded = (counts + MOE_BLOCK - 1) // MOE_BLOCK * MOE_BLOCK
    pend = jnp.cumsum(padded)
    pstart = pend - padded
    slot = pstart[e_flat] + rank
    n_blocks = (t * k) // MOE_BLOCK + n_experts
    block_e = jnp.clip(jnp.searchsorted(pend, jnp.arange(n_blocks) * MOE_BLOCK, side='right'), 0, n_experts - 1)
    order = jnp.argsort(e_flat, stable=True).astype(jnp.int32)
    pad_before = pstart - (jnp.cumsum(counts) - counts)
    src = jnp.arange(n_blocks * MOE_BLOCK, dtype=jnp.int32) - jnp.repeat(pad_before[block_e], MOE_BLOCK)
    slot_tok = order[jnp.clip(src, 0, t * k - 1)] // k
    n_used = (pend[-1] // MOE_BLOCK).astype(jnp.int32).reshape(1)
    return (slot_tok.reshape(n_blocks, 1, MOE_BLOCK), block_e.astype(jnp.int32), n_used,
            slot.reshape(t, k).astype(jnp.int32))


def _moe_ffn_kernel(be_ref, nu_ref, tok_ref, nxt_ref, h_hbm, w1_ref, w3_ref, w2_ref, y_ref, xg_scr, xb_scr, acc_scr,
                    sem, *, n_j):
    i = pl.program_id(0)
    j = pl.program_id(1)
    used = i < nu_ref[0]
    rows = xb_scr.shape[0]
    per_step = rows // n_j
    cur = i % 2
    nxt = 1 - cur

    def row_copy(idx_ref, r, buf):
        return pltpu.make_async_copy(h_hbm.at[pl.ds(idx_ref[0, 0, r], 1)], xg_scr.at[buf, pl.ds(r, 1)], sem.at[buf])

    def wait_rows(buf):
        pltpu.make_async_copy(h_hbm.at[pl.ds(0, rows)], xg_scr.at[buf], sem.at[buf]).wait()

    @pl.when(jnp.logical_and(i == 0, j == 0))
    def _():
        def start(r, carry):
            row_copy(tok_ref, r, 0).start()
            return carry
        lax.fori_loop(0, rows, start, 0)

    @pl.when(j == 0)
    def _():
        wait_rows(cur)
        xb_scr[...] = _tiles_to_rows(xg_scr[cur]).astype(BF16)
        acc_scr[...] = jnp.zeros_like(acc_scr)

    def prefetch_rows():
        base = j * per_step
        for k in range(per_step):
            row_copy(nxt_ref, base + k, nxt).start(priority=k % 2)

    @pl.when(used)
    def _():
        xb = xb_scr[...]
        a = jnp.dot(xb, w1_ref[0], preferred_element_type=F32)
        g = jnp.dot(xb, w3_ref[0], preferred_element_type=F32)
        acc_scr[...] += _bdot(_silu(a) * g, w2_ref[0])
        prefetch_rows()

    @pl.when(jnp.logical_not(used))
    def _():
        prefetch_rows()

    @pl.when(j == n_j - 1)
    def _():
        y_ref[...] = _rows_to_tiles(acc_scr[...])

    @pl.when(jnp.logical_and(i == pl.num_programs(0) - 1, j == n_j - 1))
    def _():
        wait_rows(nxt)


def _moe_ffn(h2, slot_tok, block_e, n_used, w1, w3, w2):
    tile = h2.shape[1:]
    d = tile[0] * tile[1]
    n_blocks = slot_tok.shape[0]
    f = w1.shape[2]
    mb = MOE_BLOCK
    tf = MOE_FF_TILE
    n_j = f // tf
    grid_spec = pltpu.PrefetchScalarGridSpec(
        num_scalar_prefetch=2,
        grid=(n_blocks, n_j),
        in_specs=[pl.BlockSpec((1, 1, mb), lambda i, j, be, nu: (i, 0, 0), memory_space=pltpu.SMEM),
                  pl.BlockSpec((1, 1, mb), lambda i, j, be, nu: (jnp.minimum(i + 1, n_blocks - 1), 0, 0),
                               memory_space=pltpu.SMEM),
                  pl.BlockSpec(memory_space=pl.ANY),
                  pl.BlockSpec((1, d, tf), lambda i, j, be, nu: (be[i], 0, j)),
                  pl.BlockSpec((1, d, tf), lambda i, j, be, nu: (be[i], 0, j)),
                  pl.BlockSpec((1, tf, d), lambda i, j, be, nu: (be[i], j, 0))],
        out_specs=pl.BlockSpec((mb,) + tile, lambda i, j, be, nu: (i, 0, 0)),
        scratch_shapes=[pltpu.VMEM((2, mb) + tile, F32), pltpu.VMEM((mb, d), BF16), pltpu.VMEM((mb, d), F32),
                        pltpu.SemaphoreType.DMA((2,))],
    )
    return pl.pallas_call(
        functools.partial(_moe_ffn_kernel, n_j=n_j),
        out_shape=jax.ShapeDtypeStruct((n_blocks * mb,) + tile, F32),
        grid_spec=grid_spec,
        compiler_params=_cparams("arbitrary", "arbitrary"),
        name="moe_expert_ffn",
    )(block_e, n_used, slot_tok, slot_tok, h2, w1, w3, w2)


def _moe_combine_kernel(slot_ref, nxt_ref, y_hbm, x1_ref, wt_ref, modx_ref, g_ref, o_ref, y_scr, sem):
    tm = x1_ref.shape[1]
    n_t = pl.num_programs(1)
    step = pl.program_id(0) * n_t + pl.program_id(1)
    n_steps = pl.num_programs(0) * n_t
    cur = step % 2
    nxt = 1 - cur

    def row_copy(idx_ref, r, buf):
        return pltpu.make_async_copy(y_hbm.at[pl.ds(idx_ref[0, 0, r], 1)], y_scr.at[buf, pl.ds(r, 1)], sem.at[buf])

    def wait_rows(buf):
        pltpu.make_async_copy(y_hbm.at[pl.ds(0, MOE_TOP_K * tm)], y_scr.at[buf], sem.at[buf]).wait()

    @pl.when(step == 0)
    def _():
        def start(r, carry):
            row_copy(slot_ref, r, 0).start()
            return carry
        lax.fori_loop(0, MOE_TOP_K * tm, start, 0)

    wait_rows(cur)
    for r in range(MOE_TOP_K * tm):
        row_copy(nxt_ref, r, nxt).start(priority=r % 2)
    wt = wt_ref[...]
    y = wt[:, 0:1] * _tiles_to_rows(y_scr[cur, 0:tm]) + wt[:, 1:2] * _tiles_to_rows(y_scr[cur, tm:2 * tm])
    x2 = x1_ref[0] + modx_ref[0, 5:6, :] * y
    o_ref[0] = (x2 * lax.rsqrt(jnp.mean(x2 * x2, axis=-1, keepdims=True) + EPS)) * g_ref[...]

    @pl.when(step == n_steps - 1)
    def _():
        wait_rows(nxt)


def _moe_combine(y_slots, slot_of, wt, x1, modx, final_g):
    b, seq, d = x1.shape
    tm = ROW_TILE
    n_t = seq // tm
    n_steps = b * n_t
    slots = slot_of.reshape(n_steps, tm, MOE_TOP_K).transpose(0, 2, 1).reshape(n_steps, 1, MOE_TOP_K * tm)
    return pl.pallas_call(
        _moe_combine_kernel,
        out_shape=jax.ShapeDtypeStruct((b, seq, d), F32),
        grid=(b, n_t),
        in_specs=[pl.BlockSpec((1, 1, MOE_TOP_K * tm), lambda i, t: (i * n_t + t, 0, 0), memory_space=pltpu.SMEM),
                  pl.BlockSpec((1, 1, MOE_TOP_K * tm), lambda i, t: (jnp.minimum(i * n_t + t + 1, n_steps - 1), 0, 0),
                               memory_space=pltpu.SMEM),
                  pl.BlockSpec(memory_space=pl.ANY),
                  pl.BlockSpec((1, tm, d), lambda i, t: (i, t, 0)),
                  pl.BlockSpec((tm, MOE_TOP_K), lambda i, t: (i * n_t + t, 0)),
                  pl.BlockSpec((1, 6, d), lambda i, t: (i, 0, 0)),
                  pl.BlockSpec((1, d), lambda i, t: (0, 0))],
        out_specs=pl.BlockSpec((1, tm, d), lambda i, t: (i, t, 0)),
        scratch_shapes=[pltpu.VMEM((2, MOE_TOP_K * tm) + y_slots.shape[1:], F32), pltpu.SemaphoreType.DMA((2,))],
        compiler_params=_cparams("arbitrary", "arbitrary"),
        name="moe_combine_norm",
    )(slots, slots, y_slots, x1, wt, modx, final_g.reshape(1, d))


def _odd_layer(stream, modx, modc, n_ctx_tok, norm1_g, norm2_g, w_in, w_out, conv_w, conv_b, filt, hy_d, rpb,
               router, moe_w1, moe_w3, moe_w2, final_g):
    b, s, d = stream.shape
    seq = s - n_ctx_tok
    hy_w = hy_d.shape[1]
    n_heads = rpb.shape[0]
    na_w = n_heads * HEAD_DIM
    t_off = n_ctx_tok // ROW_TILE
    w_in_b = w_in.astype(BF16)
    proj = _in_proj((stream,), modx, modc, norm1_g, w_in_b, t_off=t_off, n_t=seq // ROW_TILE, ctx_rule=_never_ctx)
    proj_c = _in_proj((stream,), modx, modc, norm1_g, w_in_b[:, 3 * hy_w + na_w:], t_off=0, n_t=t_off,
                      ctx_rule=_always_ctx)
    cos, sin, alt = _dft_tables(seq // 2)
    sre, sim, sny = _hy_spectrum(_hy_block_kernels(_hy_filters(seq, *filt)), cos, sin, alt)
    o_hy = _hy_conv(proj, conv_w.astype(F32), conv_b.reshape(1, -1).astype(F32), cos, sin, alt, sre, sim, sny,
                    hy_d.astype(F32), hy_w)
    o_na = _na_attention(proj, proj_c, rpb, q_off=3 * hy_w, n_heads=n_heads)
    x1, h2, idx, wt = _mix_odd(stream, t_off, o_hy, o_na, modx, norm2_g, w_out.astype(BF16), router)
    slot_tok, block_e, n_used, slot_of = _moe_plan(idx.reshape(b * seq, MOE_TOP_K), router.shape[1])
    y_slots = _moe_ffn(h2.reshape((b * seq,) + h2.shape[2:]), slot_tok, block_e, n_used,
                       moe_w1.astype(BF16), moe_w3.astype(BF16), moe_w2.astype(BF16))
    return _moe_combine(y_slots, slot_of, wt.reshape(b * seq, MOE_TOP_K), x1, modx, final_g)


def kernel(x, c, ctx, c_ctx, mod_w, mod_b, norm1_g, norm2_g, ev_w_in, ev_w_out, s5_lam_re, s5_lam_im, s5_log_dt, s5_b_re, s5_b_im, s5_c_re, s5_c_im, s5_d, s5_w_glu, gqa_q_g, gqa_k_g, ffn_w1, ffn_w3, ffn_w2, od_w_in, od_w_out, hy_conv_w, hy_conv_b, hy_w1, hy_b1, hy_w2, hy_b2, hy_w3, hy_freq, hy_decay, hy_d, na_rpb, moe_router, moe_w1, moe_w3, moe_w2, final_g):
    b, seq, d = x.shape
    n_ctx_tok = ctx.shape[1]
    assert n_ctx_tok == ROW_TILE and seq % ROW_TILE == 0
    rows = 8 * ((b + 1 + 7) // 8)
    cvec = jnp.zeros((rows, d), F32).at[:b].set(c).at[b].set(c_ctx)
    m = _modulation(cvec, mod_w, mod_b)
    modx = [m[l, :b].reshape(b, 6, d) for l in range(2)]
    modc = [m[l, b].reshape(1, 6, d) for l in range(2)]
    s5p = (s5_lam_re[0], s5_lam_im[0], s5_log_dt[0], s5_b_re[0], s5_b_im[0], s5_c_re[0], s5_c_im[0], s5_d[0])
    stream = _even_layer((ctx, x), modx[0], modc[0], n_ctx_tok, norm1_g[0], norm2_g[0], ev_w_in[0], ev_w_out[0], s5p,
                         s5_w_glu[0], gqa_q_g[0], gqa_k_g[0], ffn_w1[0], ffn_w3[0], ffn_w2[0])
    filt = (hy_w1[0], hy_b1[0], hy_w2[0], hy_b2[0], hy_w3[0], hy_freq[0], hy_decay[0])
    return _odd_layer(stream, modx[1], modc[1], n_ctx_tok, norm1_g[1], norm2_g[1], od_w_in[0], od_w_out[0],
                      hy_conv_w[0], hy_conv_b[0], filt, hy_d[0], na_rpb[0], moe_router[0],
                      moe_w1[0], moe_w3[0], moe_w2[0], final_g)
```
